```python
import jax
import jax.numpy as jnp
from jax import lax
import numpy as np

D_MODEL = 1024
BATCH = 8
SEQ = 2048
DEPTH = 4
DEC_BATCH = 128
DEC_SEQ = 8
PAST_LEN = 2048
PAGE_SIZE = 128

HEAD_DIM = 64
ROPE_DIM = HEAD_DIM // 4
ROPE_THETA = 500000.0
NSA_HEADS = 8
NSA_KV_HEADS = 2
NSA_GROUP = NSA_HEADS // NSA_KV_HEADS
CMP_BLOCK = 32
CMP_STRIDE = 16
SEL_BLOCK = 64
N_SEL = 8
WINDOW = 512
Q_BLOCK = 128
LRU_WIDTH = D_MODEL // 2
LRU_BLOCKS = 8
LRU_BLOCK_DIM = LRU_WIDTH // LRU_BLOCKS
LRU_C = 8.0
SHORT_CONV = 4
SSD_HEADS = 16
SSD_HEAD_DIM = 64
SSD_INNER = SSD_HEADS * SSD_HEAD_DIM
SSD_GROUPS = 2
SSD_STATE = 128
SSD_CHUNK = 64
SSD_CONV_DIM = SSD_INNER + 2 * SSD_GROUPS * SSD_STATE
GLA_HEADS = 4
GLA_DK = 128
GLA_DV = 128
GLA_RANK = 16
GLA_TAU = 16.0
GLA_CHUNK = 32
D_FF = 2816
FFN_CONV = 3
N_EVEN = (DEPTH + 1) // 2
N_ODD = DEPTH // 2
ALPHA = (2.0 * DEPTH) ** 0.25
BETA = (8.0 * DEPTH) ** -0.25
LN_EPS = 1e-5
NEG = -1e30
EV_Q = NSA_HEADS * HEAD_DIM
EV_KV = 6 * NSA_KV_HEADS * HEAD_DIM
EV_GATE = 3 * NSA_HEADS
EVEN_IN = EV_Q + EV_KV + EV_GATE + 2 * LRU_WIDTH
EVEN_OUT = EV_Q + LRU_WIDTH
ODD_SIZES = (SSD_INNER, SSD_CONV_DIM, SSD_HEADS, GLA_HEADS * GLA_DK, GLA_HEADS * GLA_DK, GLA_HEADS * GLA_DV, GLA_HEADS * GLA_DV, GLA_RANK)
ODD_IN = sum(ODD_SIZES)
ODD_OUT = SSD_INNER + GLA_HEADS * GLA_DV
F32 = jnp.float32

kernel_name = 'hybrid_nsa_rglru_ssd_gla_step'


def split_cols(h, sizes):
    cuts = [int(c) for c in np.cumsum(sizes)[:-1]]
    return jnp.split(h, cuts, axis=-1)


def pad_time(a, pad):
    return jnp.pad(a, [(0, 0), (0, pad)] + [(0, 0)] * (a.ndim - 2))


def layer_norm(x, g, b):
    xf = x.astype(F32)
    mu = jnp.mean(xf, -1, keepdims=True)
    var = jnp.mean(jnp.square(xf - mu), -1, keepdims=True)
    return ((xf - mu) * lax.rsqrt(var + LN_EPS) * g + b).astype(x.dtype)


def rms_norm(x, w):
    xf = x.astype(F32)
    return xf * lax.rsqrt(jnp.mean(xf * xf, -1, keepdims=True) + LN_EPS) * w


def masked_softmax(s, mask):
    p = jax.nn.softmax(jnp.where(mask, s, NEG), axis=-1)
    return jnp.where(mask, p, 0.0)


def partial_rope(x, pos):
    half = ROPE_DIM // 2
    inv = jnp.power(ROPE_THETA, -2.0 * jnp.arange(half, dtype=F32) / ROPE_DIM)
    ang = pos.astype(F32)[:, None] * inv[None, :]
    cos = jnp.cos(ang)[None, :, None, :]
    sin = jnp.sin(ang)[None, :, None, :]
    x1 = x[..., :half].astype(F32)
    x2 = x[..., half:ROPE_DIM].astype(F32)
    rot = jnp.concatenate([x1 * cos - x2 * sin, x2 * cos + x1 * sin], -1).astype(x.dtype)
    return jnp.concatenate([rot, x[..., ROPE_DIM:]], -1)


def causal_dwconv(x, buf, w, b):
    K = w.shape[0]
    T = x.shape[1]
    xp = jnp.concatenate([buf.astype(x.dtype), x], axis=1)
    y = b + sum(w[j] * xp[:, j:j + T] for j in range(K))
    return y, xp[:, T:]


def segsum(x):
    L = x.shape[-1]
    xe = jnp.where(jnp.tril(jnp.ones((L, L), bool), -1), jnp.broadcast_to(x[..., None], x.shape + (L,)), 0.0)
    ss = jnp.cumsum(xe, axis=-2)
    return jnp.where(jnp.tril(jnp.ones((L, L), bool)), ss, -jnp.inf)


def linear_scan(a, u, h0):
    u = u.at[:, 0].add(a[:, 0] * h0)
    def combine(l, r):
        return (l[0] * r[0], r[0] * l[1] + r[1])
    _, h = lax.associative_scan(combine, (a, u), axis=1)
    return h


def overlap_matrix(n_cmp, n_sel):
    s1 = np.arange(n_cmp)[:, None] * CMP_STRIDE
    s2 = np.arange(n_sel)[None, :] * SEL_BLOCK
    ov = np.clip(np.minimum(s1 + CMP_BLOCK, s2 + SEL_BLOCK) - np.maximum(s1, s2), 0, None)
    return jnp.asarray(ov / CMP_BLOCK, dtype=F32)


def compress_blocks(kv, cmp_w):
    B, Tk = kv.shape[:2]
    r = CMP_BLOCK // CMP_STRIDE
    nh = Tk // CMP_STRIDE
    n_cmp = nh - r + 1
    halves = kv[:, :nh * CMP_STRIDE].reshape(B, nh, CMP_STRIDE, 2, NSA_KV_HEADS, HEAD_DIM)
    w = cmp_w.astype(F32).reshape(r, CMP_STRIDE, 2, NSA_KV_HEADS)
    parts = jnp.einsum('bnschd,rsch->rbnchd', halves, w)
    return sum(parts[m][:, m:m + n_cmp] for m in range(r))


def nsa_attention(q, kv_full, win_full, gates, cmp_w):
    B, Tq = q.shape[:2]
    dh = HEAD_DIM
    Tk = kv_full.shape[1]
    t = (Tk - Tq) + jnp.arange(Tq)
    qg = (q.astype(F32) * dh ** -0.5).reshape(B, Tq, NSA_KV_HEADS, NSA_GROUP, dh)
    kv = kv_full.astype(F32)
    qb_len = Q_BLOCK if Tq % Q_BLOCK == 0 else Tq
    nb = Tq // qb_len

    ck = compress_blocks(kv[:, :, :2], cmp_w)
    n_cmp = ck.shape[1]
    end_c = jnp.arange(n_cmp) * CMP_STRIDE + CMP_BLOCK - 1
    mask_c = (end_c[None, :] <= t[:, None])[None, :, None, None, :]
    p_c = masked_softmax(jnp.einsum('bqhgd,bnhd->bqhgn', qg, ck[:, :, 0]), mask_c)
    o_cmp = jnp.einsum('bqhgn,bnhd->bqhgd', p_c, ck[:, :, 1])

    n_sel = -(-Tk // SEL_BLOCK)
    k_top = min(N_SEL, n_sel)
    imp = jnp.einsum('bqhgn,nj->bqhj', p_c, overlap_matrix(n_cmp, n_sel))
    blk = jnp.arange(n_sel)[None, :]
    cur = (t // SEL_BLOCK)[:, None]
    future = blk * SEL_BLOCK > t[:, None]
    forced = (blk == 0) | (blk == cur) | (blk == cur - 1)
    imp = jnp.where(future[None, :, None], -1.0, jnp.where(forced[None, :, None], 1e6, imp))
    _, idx = lax.top_k(imp, k_top)
    pad = n_sel * SEL_BLOCK - Tk

    def to_blocks(a):
        a = jnp.pad(a, ((0, 0), (0, pad), (0, 0), (0, 0)))
        a = a.reshape(B, n_sel, SEL_BLOCK, NSA_KV_HEADS, dh).transpose(0, 3, 1, 2, 4)
        return a.reshape(B, NSA_KV_HEADS, n_sel, SEL_BLOCK * dh)

    ks_b = to_blocks(kv[:, :, 2])
    vs_b = to_blocks(kv[:, :, 3])
    bi = jnp.arange(B)[:, None, None]
    hi = jnp.arange(NSA_KV_HEADS)[None, :, None]
    offs = jnp.arange(SEL_BLOCK)

    def sel_block(args):
        qb, ib, tb = args
        flat = ib.transpose(0, 2, 1, 3).reshape(B, NSA_KV_HEADS, qb_len * k_top)
        kg = ks_b[bi, hi, flat].reshape(B, NSA_KV_HEADS, qb_len, k_top, SEL_BLOCK, dh)
        vg = vs_b[bi, hi, flat].reshape(B, NSA_KV_HEADS, qb_len, k_top, SEL_BLOCK, dh)
        s = jnp.einsum('bqhgd,bhqkpd->bqhgkp', qb, kg)
        kpos = ib[..., None] * SEL_BLOCK + offs
        m = (kpos <= tb[None, :, None, None, None])[:, :, :, None]
        shp = s.shape
        p = masked_softmax(s.reshape(shp[:4] + (-1,)), jnp.broadcast_to(m, shp).reshape(shp[:4] + (-1,))).reshape(shp)
        return jnp.einsum('bqhgkp,bhqkpd->bqhgd', p, vg)

    o_slc = lax.map(sel_block, (
        qg.reshape(B, nb, qb_len, NSA_KV_HEADS, NSA_GROUP, dh).swapaxes(0, 1),
        idx.reshape(B, nb, qb_len, NSA_KV_HEADS, k_top).swapaxes(0, 1),
        t.reshape(nb, qb_len)))
    o_slc = o_slc.swapaxes(0, 1).reshape(B, Tq, NSA_KV_HEADS, NSA_GROUP, dh)

    Tw = win_full.shape[1]
    padl = WINDOW - (Tw - Tq)
    wf = jnp.pad(win_full.astype(F32), ((0, 0), (padl, 0), (0, 0), (0, 0), (0, 0)))
    kidx = (jnp.arange(nb) * qb_len)[:, None] + jnp.arange(qb_len + WINDOW)[None, :]
    wb = wf[:, kidx]
    qw = qg.reshape(B, nb, qb_len, NSA_KV_HEADS, NSA_GROUP, dh)
    s_w = jnp.einsum('bnqhgd,bnkhd->bnqhgk', qw, wb[:, :, :, 0])
    qidx = WINDOW + (jnp.arange(nb) * qb_len)[:, None] + jnp.arange(qb_len)[None, :]
    kk = kidx[:, None, :]
    qq = qidx[:, :, None]
    m_w = (kk <= qq) & (kk > qq - WINDOW) & (kk >= padl)
    p_w = masked_softmax(s_w, m_w[None, :, :, None, None, :])
    o_win = jnp.einsum('bnqhgk,bnkhd->bnqhgd', p_w, wb[:, :, :, 1]).reshape(B, Tq, NSA_KV_HEADS, NSA_GROUP, dh)

    g = jax.nn.sigmoid(gates.astype(F32)).reshape(B, Tq, NSA_KV_HEADS, NSA_GROUP, 3)
    o = g[..., 0:1] * o_cmp + g[..., 1:2] * o_slc + g[..., 2:3] * o_win
    return o.reshape(B, Tq, NSA_HEADS * dh).astype(q.dtype)


def even_mixer(x, past_kv, win_buf, win_keep, lru_h0, lru_conv0, w_in, w_out, cmp_w, conv_w, conv_b, w_gates, b_gates, lam):
    B, T, _ = x.shape
    pos = past_kv.shape[1] + jnp.arange(T)
    hq, hkv, hg, hx, hgate = split_cols(x @ w_in, (EV_Q, EV_KV, EV_GATE, LRU_WIDTH, LRU_WIDTH))
    q = partial_rope(hq.reshape(B, T, NSA_HEADS, HEAD_DIM), pos)
    kv = hkv.reshape(B, T, 3, 2, NSA_KV_HEADS, HEAD_DIM)
    k = partial_rope(kv[:, :, :, 0].reshape(B, T, 3 * NSA_KV_HEADS, HEAD_DIM), pos).reshape(B, T, 3, NSA_KV_HEADS, HEAD_DIM)
    kv = jnp.stack([k, kv[:, :, :, 1]], axis=3).reshape(B, T, 6, NSA_KV_HEADS, HEAD_DIM)
    new_rows = kv[:, :, :4]
    kv_full = jnp.concatenate([past_kv.astype(kv.dtype), new_rows], axis=1)
    win_full = jnp.concatenate([win_buf.astype(kv.dtype), kv[:, :, 4:]], axis=1)
    o_nsa = nsa_attention(q, kv_full, win_full, hg.reshape(B, T, NSA_HEADS, 3), cmp_w)
    xc, lru_conv = causal_dwconv(hx, lru_conv0, conv_w, conv_b)
    gt = jnp.einsum('btnd,knde->kbtne', xc.reshape(B, T, LRU_BLOCKS, LRU_BLOCK_DIM), w_gates).reshape(2, B, T, LRU_WIDTH)
    gt = gt.astype(F32) + b_gates[:, None, None, :]
    r_gate = jax.nn.sigmoid(gt[0])
    i_gate = jax.nn.sigmoid(gt[1])
    log_a = -LRU_C * r_gate * jax.nn.softplus(-lam.astype(F32))
    a = jnp.exp(log_a)
    u = jnp.sqrt(-jnp.expm1(2.0 * log_a)) * i_gate * xc.astype(F32)
    h = linear_scan(a, u, lru_h0.astype(F32))
    y_lru = (h * jax.nn.gelu(hgate.astype(F32))).astype(x.dtype)
    out = jnp.concatenate([o_nsa, y_lru], axis=-1) @ w_out
    return out, new_rows, win_full[:, win_full.shape[1] - win_keep:], h[:, -1], lru_conv


def ssd_scan(xdt, log_a, bm, cm, h0):
    B, T, H, P = xdt.shape
    G, N = bm.shape[2], bm.shape[3]
    J = H // G
    l = min(SSD_CHUNK, T)
    c = -(-T // l)
    pad = c * l - T
    X = pad_time(xdt, pad).reshape(B, c, l, G, J, P)
    A = pad_time(log_a, pad).reshape(B, c, l, G, J).transpose(0, 3, 4, 1, 2)
    Bc = pad_time(bm, pad).reshape(B, c, l, G, N)
    Cc = pad_time(cm, pad).reshape(B, c, l, G, N)
    A_cum = jnp.cumsum(A, axis=-1)
    L = jnp.exp(segsum(A))
    CB = jnp.einsum('bclgn,bcsgn->bgcls', Cc, Bc)
    y_diag = jnp.einsum('bgcls,bgjcls,bcsgjp->bclgjp', CB, L, X)
    decay_states = jnp.exp(A_cum[..., -1:] - A_cum)
    states = jnp.einsum('bclgn,bgjcl,bclgjp->bcgjpn', Bc, decay_states, X)
    states = jnp.concatenate([h0.reshape(B, 1, G, J, P, N), states], axis=1)
    decay_chunk = jnp.exp(segsum(jnp.pad(A_cum[..., -1], ((0, 0), (0, 0), (0, 0), (1, 0)))))
    new_states = jnp.einsum('bgjzc,bcgjpn->bzgjpn', decay_chunk, states)
    prev, final = new_states[:, :-1], new_states[:, -1]
    y_off = jnp.einsum('bclgn,bcgjpn,bgjcl->bclgjp', Cc, prev, jnp.exp(A_cum))
    y = (y_diag + y_off).reshape(B, c * l, H, P)[:, :T]
    return y, final.reshape(B, H, P, N)


def gla_scan(q, k, v, log_alpha, s0):
    B, T, H, K = q.shape
    V = v.shape[-1]
    l = min(GLA_CHUNK, T)
    c = -(-T // l)
    pad = c * l - T
    q, k, v, la = [pad_time(a, pad).reshape(B, c, l, H, a.shape[-1]) for a in (q, k, v, log_alpha)]
    b = jnp.cumsum(la, axis=2)
    qe = q * jnp.exp(b)
    ke = k * jnp.exp(-b)
    causal = jnp.tril(jnp.ones((l, l), bool))
    att = jnp.where(causal, jnp.einsum('bclhk,bcshk->bchls', qe, ke), 0.0)
    o_intra = jnp.einsum('bchls,bcshv->bclhv', att, v)
    b_end = b[:, :, -1]
    kv_chunk = jnp.einsum('bclhk,bclhv->bchkv', k * jnp.exp(b_end[:, :, None] - b), v)

    def step(s, inp):
        dec, kvc = inp
        return dec[..., None] * s + kvc, s

    s_final, s_prev = lax.scan(step, s0, (jnp.exp(b_end).swapaxes(0, 1), kv_chunk.swapaxes(0, 1)))
    o_inter = jnp.einsum('bclhk,cbhkv->bclhv', qe, s_prev)
    o = (o_intra + o_inter).reshape(B, c * l, H, V)[:, :T]
    return o, s_final


def odd_mixer(x, ssd_h0, ssd_conv0, gla_s0, w_in, w_out, conv_w, conv_b, dt_bias, a_log, d_skip, norm_w, w_alpha, b_alpha, gla_norm_w):
    B, T, _ = x.shape
    z, xbc, dt, gq, gk, gv, gg, ga = split_cols(x @ w_in, ODD_SIZES)
    xbc, ssd_conv = causal_dwconv(xbc, ssd_conv0, conv_w, conv_b)
    xbc = jax.nn.silu(xbc.astype(F32))
    xs, bm, cm = split_cols(xbc, (SSD_INNER, SSD_GROUPS * SSD_STATE, SSD_GROUPS * SSD_STATE))
    xs = xs.reshape(B, T, SSD_HEADS, SSD_HEAD_DIM)
    dt = jax.nn.softplus(dt.astype(F32) + dt_bias)
    log_decay = dt * -jnp.exp(a_log.astype(F32))
    y, ssd_h = ssd_scan(xs * dt[..., None], log_decay, bm.reshape(B, T, SSD_GROUPS, SSD_STATE),
                        cm.reshape(B, T, SSD_GROUPS, SSD_STATE), ssd_h0.astype(F32))
    y = y + d_skip[:, None] * xs
    y = rms_norm(y.reshape(B, T, SSD_INNER) * jax.nn.silu(z.astype(F32)), norm_w)
    q = gq.reshape(B, T, GLA_HEADS, GLA_DK).astype(F32) * GLA_DK ** -0.5
    k = gk.reshape(B, T, GLA_HEADS, GLA_DK).astype(F32)
    v = gv.reshape(B, T, GLA_HEADS, GLA_DV).astype(F32)
    log_alpha = (jax.nn.log_sigmoid(ga.astype(F32) @ w_alpha + b_alpha) / GLA_TAU).reshape(B, T, GLA_HEADS, GLA_DK)
    o, gla_s = gla_scan(q, k, v, log_alpha, gla_s0.astype(F32))
    o = rms_norm(o, gla_norm_w).reshape(B, T, GLA_HEADS * GLA_DV) * jax.nn.silu(gg.astype(F32))
    out = jnp.concatenate([y, o], axis=-1).astype(x.dtype) @ w_out
    return out, ssd_h, ssd_conv, gla_s


def conv_ffn(x, buf, w_up, conv_w, conv_b, w_down):
    u, new_buf = causal_dwconv(x @ w_up, buf, conv_w, conv_b)
    gate, val = jnp.split(u, 2, axis=-1)
    return (jax.nn.gelu(gate) * val) @ w_down, new_buf


def trunk(x, get_past, win_bufs, win_keep, lru_h, lru_conv, ssd_h, ssd_conv, gla_s, ffn_conv, p):
    kv_rows, wins, lh, lc, sh, sc, gs, fc = [], [], [], [], [], [], [], []
    for layer in range(DEPTH):
        j = layer // 2
        if layer % 2 == 0:
            mix, rows, win, h, c = even_mixer(
                x, get_past(j), win_bufs[j], win_keep, lru_h[j], lru_conv[j],
                p['w_in_even'][j], p['w_out_even'][j], p['nsa_cmp_w'][j], p['lru_conv_w'][j],
                p['lru_conv_b'][j], p['lru_w_gates'][j], p['lru_b_gates'][j], p['lru_lambda'][j])
            kv_rows.append(rows)
            wins.append(win)
            lh.append(h)
            lc.append(c)
        else:
            mix, s, c, g = odd_mixer(
                x, ssd_h[j], ssd_conv[j], gla_s[j], p['w_in_odd'][j], p['w_out_odd'][j],
                p['ssd_conv_w'][j], p['ssd_conv_b'][j], p['ssd_dt_bias'][j], p['ssd_a_log'][j],
                p['ssd_d'][j], p['ssd_norm_w'][j], p['gla_w_alpha'][j], p['gla_b_alpha'][j], p['gla_norm_w'][j])
            sh.append(s)
            sc.append(c)
            gs.append(g)
        x = layer_norm(ALPHA * x + mix, p['ln_mix_g'][layer], p['ln_mix_b'][layer])
        f, fb = conv_ffn(x, ffn_conv[layer], p['ffn_w_up'][layer], p['ffn_conv_w'][layer],
                         p['ffn_conv_b'][layer], p['ffn_w_down'][layer])
        fc.append(fb)
        x = layer_norm(ALPHA * x + f, p['ln_ffn_g'][layer], p['ln_ffn_b'][layer])
    stk = jnp.stack
    return x, (stk(kv_rows), stk(wins), stk(lh), stk(lc), stk(sh), stk(sc), stk(gs), stk(fc))


def setup_inputs(seed: int = 0) -> dict:
    key = jax.random.key(seed)
    ks = iter(jax.random.split(key, 48))

    def nrm(shape, scale):
        return jax.random.normal(next(ks), shape, F32) * scale

    def uni(shape, lo, hi):
        return jax.random.uniform(next(ks), shape, F32, lo, hi)

    n_pages = PAST_LEN // PAGE_SIZE
    n_used = DEC_BATCH * n_pages
    n_phys = n_used + n_used // 4
    wbuf = min(WINDOW, PAST_LEN)
    inp = {}
    inp['x_prompt'] = nrm((BATCH, SEQ, D_MODEL), 1.0)
    inp['x_sample'] = nrm((DEC_BATCH, DEC_SEQ, D_MODEL), 1.0)
    inp['cache_nsa_kv'] = nrm((N_EVEN, n_phys, PAGE_SIZE, 4, NSA_KV_HEADS, HEAD_DIM), 1.0)
    inp['cache_nsa_win'] = nrm((N_EVEN, DEC_BATCH, wbuf, 2, NSA_KV_HEADS, HEAD_DIM), 1.0)
    inp['state_lru_h'] = nrm((N_EVEN, DEC_BATCH, LRU_WIDTH), 0.5)
    inp['state_lru_conv'] = nrm((N_EVEN, DEC_BATCH, SHORT_CONV - 1, LRU_WIDTH), 1.0)
    inp['state_ssd'] = nrm((N_ODD, DEC_BATCH, SSD_HEADS, SSD_HEAD_DIM, SSD_STATE), 0.2)
    inp['state_ssd_conv'] = nrm((N_ODD, DEC_BATCH, SHORT_CONV - 1, SSD_CONV_DIM), 1.0)
    inp['state_gla'] = nrm((N_ODD, DEC_BATCH, GLA_HEADS, GLA_DK, GLA_DV), 1.0)
    inp['state_ffn_conv'] = nrm((DEPTH, DEC_BATCH, FFN_CONV - 1, 2 * D_FF), 1.0)
    perm = jax.random.permutation(next(ks), n_phys)
    inp['page_table'] = perm[:n_used].reshape(DEC_BATCH, n_pages).astype(jnp.int32)
    inp['w_in_even'] = nrm((N_EVEN, D_MODEL, EVEN_IN), D_MODEL ** -0.5)
    inp['w_out_even'] = nrm((N_EVEN, EVEN_OUT, D_MODEL), EVEN_OUT ** -0.5 * BETA)
    inp['nsa_cmp_w'] = (1.0 + nrm((N_EVEN, CMP_BLOCK, 2, NSA_KV_HEADS), 0.2)) * CMP_BLOCK ** -0.5
    inp['lru_conv_w'] = nrm((N_EVEN, SHORT_CONV, LRU_WIDTH), SHORT_CONV ** -0.5)
    inp['lru_conv_b'] = nrm((N_EVEN, LRU_WIDTH), 0.01)
    inp['lru_w_gates'] = nrm((N_EVEN, 2, LRU_BLOCKS, LRU_BLOCK_DIM, LRU_BLOCK_DIM), LRU_BLOCK_DIM ** -0.5)
    inp['lru_b_gates'] = nrm((N_EVEN, 2, LRU_WIDTH), 0.1)
    a_c = uni((N_EVEN, LRU_WIDTH), 0.9, 0.999) ** (1.0 / LRU_C)
    inp['lru_lambda'] = jnp.log(a_c) - jnp.log1p(-a_c)
    inp['w_in_odd'] = nrm((N_ODD, D_MODEL, ODD_IN), D_MODEL ** -0.5)
    inp['w_out_odd'] = nrm((N_ODD, ODD_OUT, D_MODEL), ODD_OUT ** -0.5 * BETA)
    inp['ssd_conv_w'] = nrm((N_ODD, SHORT_CONV, SSD_CONV_DIM), SHORT_CONV ** -0.5)
    inp['ssd_conv_b'] = nrm((N_ODD, SSD_CONV_DIM), 0.01)
    dt0 = jnp.exp(uni((N_ODD, SSD_HEADS), float(np.log(1e-3)), float(np.log(1e-1))))
    inp['ssd_dt_bias'] = dt0 + jnp.log(-jnp.expm1(-dt0))
    inp['ssd_a_log'] = jnp.log(uni((N_ODD, SSD_HEADS), 1.0, 16.0))
    inp['ssd_d'] = 1.0 + nrm((N_ODD, SSD_HEADS), 0.1)
    inp['ssd_norm_w'] = 1.0 + nrm((N_ODD, SSD_INNER), 0.1)
    inp['gla_w_alpha'] = nrm((N_ODD, GLA_RANK, GLA_HEADS * GLA_DK), GLA_RANK ** -0.5)
    inp['gla_b_alpha'] = nrm((N_ODD, GLA_HEADS * GLA_DK), 0.1)
    inp['gla_norm_w'] = 1.0 + nrm((N_ODD, GLA_DV), 0.1)
    inp['ffn_w_up'] = nrm((DEPTH, D_MODEL, 2 * D_FF), D_MODEL ** -0.5)
    inp['ffn_conv_w'] = nrm((DEPTH, FFN_CONV, 2 * D_FF), FFN_CONV ** -0.5)
    inp['ffn_conv_b'] = nrm((DEPTH, 2 * D_FF), 0.01)
    inp['ffn_w_down'] = nrm((DEPTH, D_FF, D_MODEL), D_FF ** -0.5 * BETA)
    inp['ln_mix_g'] = 1.0 + nrm((DEPTH, D_MODEL), 0.1)
    inp['ln_mix_b'] = nrm((DEPTH, D_MODEL), 0.02)
    inp['ln_ffn_g'] = 1.0 + nrm((DEPTH, D_MODEL), 0.1)
    inp['ln_ffn_b'] = nrm((DEPTH, D_MODEL), 0.02)
    return inp


def reference(x_prompt, x_sample, cache_nsa_kv, cache_nsa_win, state_lru_h, state_lru_conv, state_ssd,
              state_ssd_conv, state_gla, state_ffn_conv, page_table, w_in_even, w_out_even, nsa_cmp_w,
              lru_conv_w, lru_conv_b, lru_w_gates, lru_b_gates, lru_lambda, w_in_odd, w_out_odd, ssd_conv_w,
              ssd_conv_b, ssd_dt_bias, ssd_a_log, ssd_d, ssd_norm_w, gla_w_alpha, gla_b_alpha, gla_norm_w,
              ffn_w_up, ffn_conv_w, ffn_conv_b, ffn_w_down, ln_mix_g, ln_mix_b, ln_ffn_g, ln_ffn_b):
    p = dict(w_in_even=w_in_even, w_out_even=w_out_even, nsa_cmp_w=nsa_cmp_w, lru_conv_w=lru_conv_w,
             lru_conv_b=lru_conv_b, lru_w_gates=lru_w_gates, lru_b_gates=lru_b_gates, lru_lambda=lru_lambda,
             w_in_odd=w_in_odd, w_out_odd=w_out_odd, ssd_conv_w=ssd_conv_w, ssd_conv_b=ssd_conv_b,
             ssd_dt_bias=ssd_dt_bias, ssd_a_log=ssd_a_log, ssd_d=ssd_d, ssd_norm_w=ssd_norm_w,
             gla_w_alpha=gla_w_alpha, gla_b_alpha=gla_b_alpha, gla_norm_w=gla_norm_w, ffn_w_up=ffn_w_up,
             ffn_conv_w=ffn_conv_w, ffn_conv_b=ffn_conv_b, ffn_w_down=ffn_w_down, ln_mix_g=ln_mix_g,
             ln_mix_b=ln_mix_b, ln_ffn_g=ln_ffn_g, ln_ffn_b=ln_ffn_b)
    bp, tp = x_prompt.shape[0], x_prompt.shape[1]
    dtype = x_prompt.dtype

    def zeros(*shape):
        return jnp.zeros(shape, dtype)

    y_prompt, pr = trunk(
        x_prompt, lambda e: zeros(bp, 0, 4, NSA_KV_HEADS, HEAD_DIM),
        zeros(N_EVEN, bp, 0, 2, NSA_KV_HEADS, HEAD_DIM), min(WINDOW, tp),
        zeros(N_EVEN, bp, LRU_WIDTH), zeros(N_EVEN, bp, SHORT_CONV - 1, LRU_WIDTH),
        zeros(N_ODD, bp, SSD_HEADS, SSD_HEAD_DIM, SSD_STATE), zeros(N_ODD, bp, SHORT_CONV - 1, SSD_CONV_DIM),
        zeros(N_ODD, bp, GLA_HEADS, GLA_DK, GLA_DV), zeros(DEPTH, bp, FFN_CONV - 1, 2 * D_FF), p)
    bd = x_sample.shape[0]

    def past(e):
        return cache_nsa_kv[e][page_table].reshape(bd, -1, 4, NSA_KV_HEADS, HEAD_DIM)

    y_sample, sr = trunk(
        x_sample, past, cache_nsa_win, cache_nsa_win.shape[2], state_lru_h, state_lru_conv,
        state_ssd, state_ssd_conv, state_gla, state_ffn_conv, p)
    kv_p, win_p, lh_p, lc_p, ss_p, sc_p, g_p, f_p = pr
    kv_s, win_s, lh_s, lc_s, ss_s, sc_s, g_s, f_s = sr
    return (y_prompt, y_sample, kv_p, kv_s, win_p, win_s, lh_p, lh_s, lc_p, lc_s, ss_p, ss_s, sc_p, sc_s, g_p, g_s, f_p, f_s)
```

```python
import functools

import jax
import jax.numpy as jnp
import numpy as np
from jax import lax
from jax.experimental import pallas as pl
from jax.experimental.pallas import tpu as pltpu

F32 = jnp.float32
BF16 = jnp.bfloat16
HIGHEST = lax.Precision.HIGHEST

D_MODEL = 1024
DEPTH = 4
PAGE_SIZE = 128
HEAD_DIM = 64
ROPE_DIM = HEAD_DIM // 4
ROPE_THETA = 500000.0
NSA_HEADS = 8
NSA_KV_HEADS = 2
NSA_GROUP = NSA_HEADS // NSA_KV_HEADS
CMP_BLOCK = 32
CMP_STRIDE = 16
SEL_BLOCK = 64
N_SEL = 8
WINDOW = 512
LRU_WIDTH = D_MODEL // 2
LRU_BLOCKS = 8
LRU_BLOCK_DIM = LRU_WIDTH // LRU_BLOCKS
LRU_C = 8.0
SHORT_CONV = 4
SSD_HEADS = 16
SSD_HEAD_DIM = 64
SSD_INNER = SSD_HEADS * SSD_HEAD_DIM
SSD_GROUPS = 2
SSD_STATE = 128
SSD_CONV_DIM = SSD_INNER + 2 * SSD_GROUPS * SSD_STATE
GLA_HEADS = 4
GLA_DK = 128
GLA_DV = 128
GLA_RANK = 16
GLA_TAU = 16.0
GLA_CHUNK = 32
D_FF = 2816
FFN_CONV = 3
ALPHA = (2.0 * DEPTH) ** 0.25
LN_EPS = 1e-5
NEG = -1e30
EV_Q = NSA_HEADS * HEAD_DIM
EV_KV = 6 * NSA_KV_HEADS * HEAD_DIM
EV_GATE = 3 * NSA_HEADS
GLA_W = GLA_HEADS * GLA_DK

LANES = 128
SUBLANES = 8
V7X_VMEM_BYTES = 64 * 1024 * 1024
VMEM_LIMIT = V7X_VMEM_BYTES - 8 * 1024 * 1024
HALO = SUBLANES


def _cparams(*sem):
    return pltpu.CompilerParams(dimension_semantics=sem, vmem_limit_bytes=VMEM_LIMIT)


def _bdot(a, b):
    return jnp.dot(a.astype(BF16), b.astype(BF16), preferred_element_type=F32)


def _bdot_nt(a, b):
    return lax.dot_general(a.astype(BF16), b.astype(BF16), (((1,), (1,)), ((), ())), preferred_element_type=F32)


def _bdot_tn(a, b):
    return lax.dot_general(a.astype(BF16), b.astype(BF16), (((0,), (0,)), ((), ())), preferred_element_type=F32)


def _iota(shape, axis):
    return lax.broadcasted_iota(jnp.int32, shape, axis)


def _layer_norm(y, g, b):
    mu = jnp.mean(y, -1, keepdims=True)
    d = y - mu
    var = jnp.mean(d * d, -1, keepdims=True)
    return d * lax.rsqrt(var + LN_EPS) * g + b


def _masked_softmax(s, mask):
    s = jnp.where(mask, s, NEG)
    m = jnp.max(s, -1, keepdims=True)
    e = jnp.exp(s - m)
    p = e / jnp.sum(e, -1, keepdims=True)
    return jnp.where(mask, p, 0.0)


def _proj_body(*refs, segs, has_rope):
    if has_rope:
        x_ref, w_ref, c_ref, s1_ref, s2_ref = refs[:5]
        out_refs = refs[5:]
    else:
        x_ref, w_ref = refs[:2]
        out_refs = refs[2:]
    xb = x_ref[...].astype(BF16)
    for o_ref, (start, width, rope) in zip(out_refs, segs):
        acc = jnp.dot(xb, w_ref[:, start:start + width], preferred_element_type=F32)
        if rope is None:
            o_ref[...] = acc
            continue
        for c, flag in enumerate(rope):
            chunk = acc[:, c * LANES:(c + 1) * LANES]
            if flag:
                chunk = (chunk * c_ref[...] + pltpu.roll(chunk, LANES - ROPE_DIM // 2, 1) * s2_ref[...]
                         + pltpu.roll(chunk, ROPE_DIM // 2, 1) * s1_ref[...])
            o_ref[:, c * LANES:(c + 1) * LANES] = chunk


def _proj(x2d, w, segs, tm, tabs=None, tab_blocks=1):
    M, K = x2d.shape
    N = w.shape[1]
    has_rope = tabs is not None
    in_specs = [pl.BlockSpec((tm, K), lambda i: (i, 0)), pl.BlockSpec((K, N), lambda i: (0, 0))]
    args = [x2d, w]
    if has_rope:
        in_specs += [pl.BlockSpec((tm, LANES), lambda i: (i % tab_blocks, 0))] * 3
        args += list(tabs)
    return pl.pallas_call(
        functools.partial(_proj_body, segs=segs, has_rope=has_rope),
        grid=(M // tm,),
        in_specs=in_specs,
        out_specs=[pl.BlockSpec((tm, wd), lambda i: (i, 0)) for _, wd, _ in segs],
        out_shape=[jax.ShapeDtypeStruct((M, wd), F32) for _, wd, _ in segs],
        compiler_params=_cparams("parallel"),
        name="proj",
    )(*args)


def _rope_tables(pos):
    half = ROPE_DIM // 2
    inv = jnp.power(ROPE_THETA, -2.0 * jnp.arange(half, dtype=F32) / ROPE_DIM)
    ang = pos.astype(F32)[:, None] * inv[None, :]
    cos, sin = jnp.cos(ang), jnp.sin(ang)
    R = pos.shape[0]
    rest = HEAD_DIM - ROPE_DIM
    c64 = jnp.concatenate([cos, cos, jnp.ones((R, rest), F32)], 1)
    s1 = jnp.concatenate([jnp.zeros((R, half), F32), sin, jnp.zeros((R, rest), F32)], 1)
    s2 = jnp.concatenate([-sin, jnp.zeros((R, half + rest), F32)], 1)
    return tuple(jnp.concatenate([t, t], 1) for t in (c64, s1, s2))


def _compress_body(*refs):
    kv_ref, w_ref, o_ref = refs[-3:]
    R = kv_ref.shape[1]
    x = kv_ref[0].reshape(R // CMP_STRIDE, CMP_STRIDE, 2 * LANES)
    o_ref[0, :, 0:2 * LANES] = jnp.sum(x * w_ref[0:CMP_STRIDE, :][None], axis=1)
    o_ref[0, :, 2 * LANES:4 * LANES] = jnp.sum(x * w_ref[CMP_STRIDE:CMP_BLOCK, :][None], axis=1)


def _compress_prompt(rows3, wtab):
    B, T, _ = rows3.shape
    return pl.pallas_call(
        _compress_body,
        grid=(B,),
        in_specs=[pl.BlockSpec((1, T, 2 * LANES), lambda b: (b, 0, 0)),
                  pl.BlockSpec((CMP_BLOCK, 2 * LANES), lambda b: (0, 0))],
        out_specs=pl.BlockSpec((1, T // CMP_STRIDE, 4 * LANES), lambda b: (b, 0, 0)),
        out_shape=jax.ShapeDtypeStruct((B, T // CMP_STRIDE, 4 * LANES), F32),
        compiler_params=_cparams("parallel"),
        name="compress_prompt",
    )(rows3, wtab)


def _compress_sample(cache3, page_table, page_base, wtab):
    B, n_pages = page_table.shape
    per_page = PAGE_SIZE // CMP_STRIDE
    grid_spec = pltpu.PrefetchScalarGridSpec(
        num_scalar_prefetch=1,
        grid=(B, n_pages),
        in_specs=[pl.BlockSpec((1, PAGE_SIZE, 2 * LANES), lambda b, p, pt: (page_base + pt[b, p], 0, 0)),
                  pl.BlockSpec((CMP_BLOCK, 2 * LANES), lambda b, p, pt: (0, 0))],
        out_specs=pl.BlockSpec((1, per_page, 4 * LANES), lambda b, p, pt: (b, p, 0)),
    )
    return pl.pallas_call(
        _compress_body,
        grid_spec=grid_spec,
        out_shape=jax.ShapeDtypeStruct((B, n_pages * per_page, 4 * LANES), F32),
        compiler_params=_cparams("parallel", "arbitrary"),
        name="compress_sample",
    )(page_table, cache3, wtab)


def _overlap_matrix(n_rows, n_cmp, n_sel):
    s1 = np.arange(n_rows)[:, None] * CMP_STRIDE
    s2 = np.arange(LANES)[None, :] * SEL_BLOCK
    ov = np.clip(np.minimum(s1 + CMP_BLOCK, s2 + SEL_BLOCK) - np.maximum(s1, s2), 0, None) / CMP_BLOCK
    ov = ov * (np.arange(n_rows)[:, None] < n_cmp) * (np.arange(LANES)[None, :] < n_sel)
    return jnp.asarray(ov, dtype=F32)


def _select_blocks(psum, ov, t_col, n_sel):
    imp = jnp.dot(psum, ov, precision=HIGHEST, preferred_element_type=F32)
    blk = _iota(imp.shape, 1)
    cur = jnp.right_shift(t_col, int(np.log2(SEL_BLOCK)))
    future = blk * SEL_BLOCK > t_col
    forced = (blk == 0) | (blk == cur) | (blk == cur - 1)
    w = jnp.where(future, -1.0, jnp.where(forced, 1e6, imp))
    w = jnp.where(blk < n_sel, w, -jnp.inf)
    sel = jnp.zeros(imp.shape, F32)
    for _ in range(min(N_SEL, n_sel)):
        m = jnp.max(w, axis=-1, keepdims=True)
        idx = jnp.min(jnp.where(w == m, blk, LANES), axis=-1, keepdims=True)
        hit = blk == idx
        sel = jnp.where(hit, 1.0, sel)
        w = jnp.where(hit, -jnp.inf, w)
    return sel


def _nsa_core(q, sg, ck, mask_c, ov, t_col, n_sel, mask_pos_s, get_slc, mask_w, get_win, o_ref):
    tq = q.shape[0]
    G = NSA_GROUP
    Ks = mask_pos_s.shape[1]
    expand = (jnp.right_shift(_iota((LANES, Ks), 1), int(np.log2(SEL_BLOCK))) == _iota((LANES, Ks), 0)).astype(BF16)
    for h in range(NSA_KV_HEADS):
        qh = jnp.concatenate([q[:, (h * G + g) * HEAD_DIM:(h * G + g + 1) * HEAD_DIM] for g in range(G)], axis=0)
        qb = (qh * HEAD_DIM ** -0.5).astype(BF16)
        ckk = ck[:, h * HEAD_DIM:(h + 1) * HEAD_DIM]
        ckv = ck[:, LANES + h * HEAD_DIM:LANES + (h + 1) * HEAD_DIM]
        n_rows = ck.shape[0]
        p_c = _masked_softmax(_bdot_nt(qb, ckk).reshape(G, tq, n_rows), mask_c[None])
        o_c = _bdot(p_c.reshape(G * tq, n_rows), ckv)
        psum = p_c[0]
        for g in range(1, G):
            psum = psum + p_c[g]
        sel = _select_blocks(psum, ov, t_col, n_sel)
        mask_s = (jnp.dot(sel.astype(BF16), expand, preferred_element_type=F32) > 0.5) & mask_pos_s
        ks, vs = get_slc(h)
        p_s = _masked_softmax(_bdot_nt(qb, ks).reshape(G, tq, Ks), mask_s[None])
        o_s = _bdot(p_s.reshape(G * tq, Ks), vs)
        kw, vw = get_win(h)
        Kw = mask_w.shape[1]
        p_w = _masked_softmax(_bdot_nt(qb, kw).reshape(G, tq, Kw), mask_w[None])
        o_w = _bdot(p_w.reshape(G * tq, Kw), vw)
        for g in range(G):
            hh = h * G + g
            r = slice(g * tq, (g + 1) * tq)
            o = (sg[:, 3 * hh:3 * hh + 1] * o_c[r] + sg[:, 3 * hh + 1:3 * hh + 2] * o_s[r]
                 + sg[:, 3 * hh + 2:3 * hh + 3] * o_w[r])
            o_ref[:, hh * HEAD_DIM:(hh + 1) * HEAD_DIM] = o


def _combine_compressed(H):
    n_rows = H.shape[0]
    return H[:, 0:2 * LANES] + pltpu.roll(H[:, 2 * LANES:4 * LANES], n_rows - 1, 0)


def _nsa_prompt_body(q_ref, hg_ref, H_ref, rows_ref, win_ref, ov_ref, o_ref, *, T, tq):
    qi = pl.program_id(1)
    n_rows = T // CMP_STRIDE
    n_cmp = n_rows - CMP_BLOCK // CMP_STRIDE + 1
    n_sel = T // SEL_BLOCK
    t_col = qi * tq + _iota((tq, 1), 0)
    ck = _combine_compressed(H_ref[0])
    ncol = _iota((1, n_rows), 1)
    mask_c = (ncol * CMP_STRIDE + CMP_BLOCK - 1 <= t_col) & (ncol < n_cmp)
    causal = _iota((1, T), 1) <= t_col
    band = WINDOW + tq
    wstart = pl.multiple_of(jnp.clip(qi * tq - WINDOW, 0, T - band), tq)
    wpos = wstart + _iota((1, band), 1)
    mask_w = (wpos <= t_col) & (wpos > t_col - WINDOW)

    def get_slc(h):
        return (rows_ref[0, :, 2 * LANES + h * HEAD_DIM:2 * LANES + (h + 1) * HEAD_DIM],
                rows_ref[0, :, 3 * LANES + h * HEAD_DIM:3 * LANES + (h + 1) * HEAD_DIM])

    def get_win(h):
        return (win_ref[0, pl.ds(wstart, band), h * HEAD_DIM:(h + 1) * HEAD_DIM],
                win_ref[0, pl.ds(wstart, band), LANES + h * HEAD_DIM:LANES + (h + 1) * HEAD_DIM])

    _nsa_core(q_ref[0], jax.nn.sigmoid(hg_ref[0]), ck, mask_c, ov_ref[...], t_col, n_sel,
              causal, get_slc, mask_w, get_win, o_ref.at[0])


def _nsa_prompt(q3, hg3, H3, rows3, win3, tq):
    B, T, _ = q3.shape
    n_rows = T // CMP_STRIDE
    ov = _overlap_matrix(n_rows, n_rows - 1, T // SEL_BLOCK)
    return pl.pallas_call(
        functools.partial(_nsa_prompt_body, T=T, tq=tq),
        grid=(B, T // tq),
        in_specs=[pl.BlockSpec((1, tq, EV_Q), lambda b, i: (b, i, 0)),
                  pl.BlockSpec((1, tq, LANES), lambda b, i: (b, i, 0)),
                  pl.BlockSpec((1, n_rows, 4 * LANES), lambda b, i: (b, 0, 0)),
                  pl.BlockSpec((1, T, 4 * LANES), lambda b, i: (b, 0, 0)),
                  pl.BlockSpec((1, T, 2 * LANES), lambda b, i: (b, 0, 0)),
                  pl.BlockSpec((n_rows, LANES), lambda b, i: (0, 0))],
        out_specs=pl.BlockSpec((1, tq, EV_Q), lambda b, i: (b, i, 0)),
        out_shape=jax.ShapeDtypeStruct((B, T, EV_Q), F32),
        compiler_params=_cparams("parallel", "arbitrary"),
        name="nsa_prompt",
    )(q3, hg3, H3, rows3, win3, ov)


def _nsa_sample_body(*refs, n_pages, Tq, Wb):
    pt_ref, q_ref, hg_ref, H_ref, rnew_ref = refs[:5]
    page_refs = refs[5:5 + n_pages]
    cwin_ref, wnew_ref, ov_ref, o_ref, wout_ref, ks_scr, ws_scr = refs[5 + n_pages:]
    P = n_pages * PAGE_SIZE
    Ks = ks_scr.shape[0]
    Kw = ws_scr.shape[0]
    for p in range(n_pages):
        ks_scr[p * PAGE_SIZE:(p + 1) * PAGE_SIZE, :] = page_refs[p][0, :, 2 * LANES:4 * LANES]
    ks_scr[P:P + Tq, :] = rnew_ref[0, :, 2 * LANES:4 * LANES]
    ks_scr[P + Tq:Ks, :] = jnp.zeros((Ks - P - Tq, 2 * LANES), F32)
    ws_scr[0:Wb, :] = cwin_ref[0]
    ws_scr[Wb:Wb + Tq, :] = wnew_ref[0]
    ws_scr[Wb + Tq:Kw, :] = jnp.zeros((Kw - Wb - Tq, 2 * LANES), F32)
    wout_ref[0] = ws_scr[Tq:Tq + Wb, :]
    n_rows = P // CMP_STRIDE
    n_cmp = n_rows - CMP_BLOCK // CMP_STRIDE + 1
    n_sel = -(-(P + Tq) // SEL_BLOCK)
    tt = _iota((Tq, 1), 0)
    t_col = P + tt
    ck = _combine_compressed(H_ref[0])
    mask_c = (_iota((1, n_rows), 1) < n_cmp) & (tt >= 0)
    causal = _iota((1, Ks), 1) <= t_col
    wpos = _iota((1, Kw), 1)
    mask_w = (wpos <= Wb + tt) & (wpos > Wb + tt - WINDOW)

    def get_slc(h):
        return (ks_scr[:, h * HEAD_DIM:(h + 1) * HEAD_DIM], ks_scr[:, LANES + h * HEAD_DIM:LANES + (h + 1) * HEAD_DIM])

    def get_win(h):
        return (ws_scr[:, h * HEAD_DIM:(h + 1) * HEAD_DIM], ws_scr[:, LANES + h * HEAD_DIM:LANES + (h + 1) * HEAD_DIM])

    _nsa_core(q_ref[0], jax.nn.sigmoid(hg_ref[0]), ck, mask_c, ov_ref[...], t_col, n_sel,
              causal, get_slc, mask_w, get_win, o_ref.at[0])


def _nsa_sample(q3, hg3, H3, rows3, win3, cache3, page_table, page_base, cwin3, cwin_base):
    B, Tq, _ = q3.shape
    n_pages = page_table.shape[1]
    P = n_pages * PAGE_SIZE
    Wb = cwin3.shape[1]
    n_rows = P // CMP_STRIDE
    n_sel = -(-(P + Tq) // SEL_BLOCK)
    ov = _overlap_matrix(n_rows, n_rows - 1, n_sel)
    Ks = -(-(P + Tq) // LANES) * LANES
    Kw = -(-(Wb + Tq) // LANES) * LANES

    def page_spec(p):
        return pl.BlockSpec((1, PAGE_SIZE, 4 * LANES), lambda b, pt: (page_base + pt[b, p], 0, 0))

    grid_spec = pltpu.PrefetchScalarGridSpec(
        num_scalar_prefetch=1,
        grid=(B,),
        in_specs=[pl.BlockSpec((1, Tq, EV_Q), lambda b, pt: (b, 0, 0)),
                  pl.BlockSpec((1, Tq, LANES), lambda b, pt: (b, 0, 0)),
                  pl.BlockSpec((1, n_rows, 4 * LANES), lambda b, pt: (b, 0, 0)),
                  pl.BlockSpec((1, Tq, 4 * LANES), lambda b, pt: (b, 0, 0))]
        + [page_spec(p) for p in range(n_pages)]
        + [pl.BlockSpec((1, Wb, 2 * LANES), lambda b, pt: (cwin_base + b, 0, 0)),
           pl.BlockSpec((1, Tq, 2 * LANES), lambda b, pt: (b, 0, 0)),
           pl.BlockSpec((n_rows, LANES), lambda b, pt: (0, 0))],
        out_specs=[pl.BlockSpec((1, Tq, EV_Q), lambda b, pt: (b, 0, 0)),
                   pl.BlockSpec((1, Wb, 2 * LANES), lambda b, pt: (b, 0, 0))],
        scratch_shapes=[pltpu.VMEM((Ks, 2 * LANES), F32), pltpu.VMEM((Kw, 2 * LANES), F32)],
    )
    return pl.pallas_call(
        functools.partial(_nsa_sample_body, n_pages=n_pages, Tq=Tq, Wb=Wb),
        grid_spec=grid_spec,
        out_shape=[jax.ShapeDtypeStruct((B, Tq, EV_Q), F32), jax.ShapeDtypeStruct((B, Wb, 2 * LANES), F32)],
        compiler_params=_cparams("arbitrary"),
        name="nsa_sample",
    )(page_table, q3, hg3, H3, rows3, *([cache3] * n_pages), cwin3, win3, ov)


def _causal_conv(xp, w_ref, b_ref, tc, cols=None):
    K = w_ref.shape[0]
    cs = slice(None) if cols is None else cols
    acc = None
    for j in range(K):
        term = w_ref[j:j + 1, cs][None] * xp[:, HALO - (K - 1) + j:HALO - (K - 1) + j + tc, :]
        acc = term if acc is None else acc + term
    return b_ref[:, cs][None] + acc


def _lru_body(hx_ref, hgate_ref, conv0_ref, h0_ref, cw_ref, cb_ref, wg_ref, bg_ref, lam_ref,
              y_ref, hlast_ref, convn_ref, xp, *, nb, tc):
    K1 = SHORT_CONV - 1

    @pl.when(pl.program_id(1) == 0)
    def _():
        xp[:, HALO - K1:HALO, :] = conv0_ref[...]
        hlast_ref[...] = h0_ref[...]

    xp[:, HALO:HALO + tc, :] = hx_ref[...]
    xc = _causal_conv(xp, cw_ref, cb_ref, tc)
    tail = xp[:, HALO + tc - K1:HALO + tc, :]
    convn_ref[...] = tail
    xp[:, HALO - K1:HALO, :] = tail
    R = nb * tc
    xc2 = xc.reshape(R, LRU_WIDTH)
    gt = _bdot(xc2, wg_ref[...]) + bg_ref[...]
    r_gate = jax.nn.sigmoid(gt[:, :LRU_WIDTH])
    i_gate = jax.nn.sigmoid(gt[:, LRU_WIDTH:])
    log_a = -LRU_C * r_gate * jax.nn.softplus(-lam_ref[...])
    a = jnp.exp(log_a)
    th = jnp.tanh(log_a)
    u = jnp.sqrt(-2.0 * th / (1.0 - th)) * i_gate * xc2
    tpos = lax.rem(_iota((R, 1), 0), tc)
    d = 1
    while d < tc:
        valid = tpos >= d
        u = jnp.where(valid, a * pltpu.roll(u, d, 0) + u, u)
        a = jnp.where(valid, a * pltpu.roll(a, d, 0), a)
        d *= 2
    hprev = jnp.broadcast_to(hlast_ref[...], (nb, tc, LRU_WIDTH)).reshape(R, LRU_WIDTH)
    h = a * hprev + u
    y_ref[...] = (h * jax.nn.gelu(hgate_ref[...].reshape(R, LRU_WIDTH))).reshape(nb, tc, LRU_WIDTH)
    last = _iota((1, tc, 1), 1) == tc - 1
    hlast_ref[...] = jnp.sum(jnp.where(last, h.reshape(nb, tc, LRU_WIDTH), 0.0), axis=1, keepdims=True)


def _lru(hx3, hgate3, conv0, conv_base, h0, h_base, cw, cb, wg, bg, lam, nb, tc):
    B, T, C = hx3.shape
    K1 = SHORT_CONV - 1
    cb0 = conv_base // nb
    hb0 = h_base // nb
    tok = pl.BlockSpec((nb, tc, C), lambda b, t: (b, t, 0))
    full = lambda a: pl.BlockSpec(a.shape, lambda b, t: (0,) * a.ndim)
    return pl.pallas_call(
        functools.partial(_lru_body, nb=nb, tc=tc),
        grid=(B // nb, T // tc),
        in_specs=[tok, tok,
                  pl.BlockSpec((nb, K1, C), lambda b, t: (cb0 + b, 0, 0)),
                  pl.BlockSpec((nb, 1, C), lambda b, t: (hb0 + b, 0, 0)),
                  full(cw), full(cb), full(wg), full(bg), full(lam)],
        out_specs=[tok, pl.BlockSpec((nb, 1, C), lambda b, t: (b, 0, 0)),
                   pl.BlockSpec((nb, K1, C), lambda b, t: (b, 0, 0))],
        out_shape=[jax.ShapeDtypeStruct((B, T, C), F32), jax.ShapeDtypeStruct((B, 1, C), F32),
                   jax.ShapeDtypeStruct((B, K1, C), F32)],
        scratch_shapes=[pltpu.VMEM((nb, HALO + tc, C), F32)],
        compiler_params=_cparams("parallel", "arbitrary"),
        name="lru",
    )(hx3, hgate3, conv0, h0, cw, cb, wg, bg, lam)


def _outproj_body(a1_ref, a2_ref, w1_ref, w2_ref, x_ref, g_ref, b_ref, o_ref):
    mix = _bdot(a1_ref[...], w1_ref[...]) + _bdot(a2_ref[...], w2_ref[...])
    o_ref[...] = _layer_norm(ALPHA * x_ref[...] + mix, g_ref[...], b_ref[...])


def _outproj(a1, a2, w1, w2, x2d, g, b, tm):
    M = x2d.shape[0]
    row = lambda a: pl.BlockSpec((tm, a.shape[1]), lambda i: (i, 0))
    full = lambda a: pl.BlockSpec(a.shape, lambda i: (0, 0))
    return pl.pallas_call(
        _outproj_body,
        grid=(M // tm,),
        in_specs=[row(a1), row(a2), full(w1), full(w2), row(x2d), full(g), full(b)],
        out_specs=row(x2d),
        out_shape=jax.ShapeDtypeStruct(x2d.shape, F32),
        compiler_params=_cparams("parallel"),
        name="outproj_ln",
    )(a1, a2, w1, w2, x2d, g, b)


def _ffn_body(x_ref, wup_ref, cw_ref, cb_ref, wdn_ref, buf0_ref, g_ref, b_ref, y_ref, bufn_ref, sg, sv,
              *, nb, tc, cw):
    U = 2 * D_FF

    @pl.when(pl.program_id(1) == 0)
    def _():
        bufn_ref[...] = buf0_ref[...]

    R = nb * tc
    x = x_ref[...].reshape(R, D_MODEL)
    xb = x.astype(BF16)
    acc = jnp.zeros((R, D_MODEL), F32)
    for c in range(D_FF // cw):
        conv = []
        for half, scr in ((0, sg), (1, sv)):
            cols = slice(half * D_FF + c * cw, half * D_FF + (c + 1) * cw)
            cols1 = slice(U + cols.start, U + cols.stop)
            u = jnp.dot(xb, wup_ref[:, cols], preferred_element_type=F32)
            scr[:, HALO:HALO + tc, :] = u.reshape(nb, tc, cw)
            scr[:, HALO - 2:HALO - 1, :] = bufn_ref[:, :, cols]
            scr[:, HALO - 1:HALO, :] = bufn_ref[:, :, cols1]
            conv.append(_causal_conv(scr, cw_ref, cb_ref, tc, cols).reshape(R, cw))
            bufn_ref[:, :, cols] = scr[:, HALO + tc - 2:HALO + tc - 1, :]
            bufn_ref[:, :, cols1] = scr[:, HALO + tc - 1:HALO + tc, :]
        act = jax.nn.gelu(conv[0]) * conv[1]
        acc = acc + jnp.dot(act.astype(BF16), wdn_ref[c * cw:(c + 1) * cw, :], preferred_element_type=F32)
    y = _layer_norm(ALPHA * x + acc, g_ref[...], b_ref[...])
    y_ref[...] = y.reshape(nb, tc, D_MODEL)


def _ffn(x3, wup, cw, cb, wdn, buf0, buf_base, g, b, nb, tc, cwid):
    B, T, _ = x3.shape
    U = 2 * D_FF
    bb0 = buf_base // nb
    tok = pl.BlockSpec((nb, tc, D_MODEL), lambda bi, t: (bi, t, 0))
    full = lambda a: pl.BlockSpec(a.shape, lambda bi, t: (0,) * a.ndim)
    return pl.pallas_call(
        functools.partial(_ffn_body, nb=nb, tc=tc, cw=cwid),
        grid=(B // nb, T // tc),
        in_specs=[tok, full(wup), full(cw), full(cb), full(wdn),
                  pl.BlockSpec((nb, 1, 2 * U), lambda bi, t: (bb0 + bi, 0, 0)), full(g), full(b)],
        out_specs=[tok, pl.BlockSpec((nb, 1, 2 * U), lambda bi, t: (bi, 0, 0))],
        out_shape=[jax.ShapeDtypeStruct(x3.shape, F32), jax.ShapeDtypeStruct((B, 1, 2 * U), F32)],
        scratch_shapes=[pltpu.VMEM((nb, HALO + tc, cwid), F32), pltpu.VMEM((nb, HALO + tc, cwid), F32)],
        compiler_params=_cparams("parallel", "arbitrary"),
        name="ffn",
    )(x3, wup, cw, cb, wdn, buf0, g, b)


def _ssd_body(xbc_ref, z_ref, sm_ref, conv0_ref, h0_ref, cw_ref, cb_ref, dtb_ref, alog_ref, dsk_ref, nw_ref,
              y_ref, hn_ref, convn_ref, xp, ysc, *, L):
    K1 = SHORT_CONV - 1
    P, N = SSD_HEAD_DIM, SSD_STATE

    @pl.when(pl.program_id(1) == 0)
    def _():
        xp[:, HALO - K1:HALO, :] = conv0_ref[...]
        hn_ref[...] = h0_ref[...]

    xp[:, HALO:HALO + L, :] = xbc_ref[...]
    xc = _causal_conv(xp, cw_ref, cb_ref, L)
    tail = xp[:, HALO + L - K1:HALO + L, :]
    convn_ref[...] = tail
    xp[:, HALO - K1:HALO, :] = tail
    xa = jax.nn.silu(xc[0])
    xs = xa[:, :SSD_INNER]
    bm = xa[:, SSD_INNER:SSD_INNER + SSD_GROUPS * N]
    cm = xa[:, SSD_INNER + SSD_GROUPS * N:]
    head_lane = _iota((1, LANES), 1) < SSD_HEADS
    dt = jnp.where(head_lane, jax.nn.softplus(sm_ref[0] + dtb_ref[...]), 0.0)
    la = dt * -jnp.exp(alog_ref[...])
    lower = _iota((L, L), 0) >= _iota((L, L), 1)
    acum = jnp.dot(lower.astype(F32), la, precision=HIGHEST, preferred_element_type=F32)
    acum_t = acum.T
    dt_t = dt.T
    xs_t = xs.T
    a_end = acum[L - 1:L, :]
    J = SSD_HEADS // SSD_GROUPS
    for g in range(SSD_GROUPS):
        Bg = bm[:, g * N:(g + 1) * N].astype(BF16)
        Cg = cm[:, g * N:(g + 1) * N].astype(BF16)
        CB = _bdot_nt(Cg, Bg)
        for j in range(J):
            h = g * J + j
            hs = slice(h * P, (h + 1) * P)
            col = acum[:, h:h + 1]
            row = acum_t[h:h + 1, :]
            decay = jnp.exp(jnp.where(lower, col - row, -jnp.inf))
            xh = xs[:, hs]
            y_diag = _bdot(CB * decay, xh * dt[:, h:h + 1])
            h_prev = hn_ref[0, hs, :]
            y_off = _bdot_nt(Cg, h_prev) * jnp.exp(col)
            e_end = a_end[:, h:h + 1]
            xw_t = xs_t[hs, :] * (dt_t[h:h + 1, :] * jnp.exp(e_end - row))
            hn_ref[0, hs, :] = jnp.exp(e_end) * h_prev + _bdot(xw_t, Bg)
            ysc[:, hs] = y_diag + y_off + dsk_ref[:, h:h + 1] * xh
    y = ysc[...] * jax.nn.silu(z_ref[0])
    y_ref[0] = y * lax.rsqrt(jnp.mean(y * y, -1, keepdims=True) + LN_EPS) * nw_ref[...]


def _ssd(xbc3, z3, sm3, conv0, conv_base, h0, h_base, cw, cb, dtb, alog, dsk, nw, L):
    B, T, _ = xbc3.shape
    K1 = SHORT_CONV - 1
    tokspec = lambda a: pl.BlockSpec((1, L, a.shape[2]), lambda b, t: (b, t, 0))
    full = lambda a: pl.BlockSpec(a.shape, lambda b, t: (0,) * a.ndim)
    return pl.pallas_call(
        functools.partial(_ssd_body, L=L),
        grid=(B, T // L),
        in_specs=[tokspec(xbc3), tokspec(z3), tokspec(sm3),
                  pl.BlockSpec((1, K1, SSD_CONV_DIM), lambda b, t: (conv_base + b, 0, 0)),
                  pl.BlockSpec((1, SSD_INNER, SSD_STATE), lambda b, t: (h_base + b, 0, 0)),
                  full(cw), full(cb), full(dtb), full(alog), full(dsk), full(nw)],
        out_specs=[tokspec(z3), pl.BlockSpec((1, SSD_INNER, SSD_STATE), lambda b, t: (b, 0, 0)),
                   pl.BlockSpec((1, K1, SSD_CONV_DIM), lambda b, t: (b, 0, 0))],
        out_shape=[jax.ShapeDtypeStruct(z3.shape, F32), jax.ShapeDtypeStruct((B, SSD_INNER, SSD_STATE), F32),
                   jax.ShapeDtypeStruct((B, K1, SSD_CONV_DIM), F32)],
        scratch_shapes=[pltpu.VMEM((1, HALO + L, SSD_CONV_DIM), F32), pltpu.VMEM((L, SSD_INNER), F32)],
        compiler_params=_cparams("parallel", "arbitrary"),
        name="ssd",
    )(xbc3, z3, sm3, conv0, h0, cw, cb, dtb, alog, dsk, nw)


def _gla_body(q_ref, k_ref, v_ref, gg_ref, sm_ref, s0_ref, wa_ref, ba_ref, nw_ref, o_ref, sn_ref, st, *, tc, l):
    K, V = GLA_DK, GLA_DV

    @pl.when(pl.program_id(1) == 0)
    def _():
        for h in range(GLA_HEADS):
            st[h * V:(h + 1) * V, :] = s0_ref[0, h * K:(h + 1) * K, :].T

    log_alpha = jax.nn.log_sigmoid(_bdot(sm_ref[0], wa_ref[...]) + ba_ref[...]) / GLA_TAU
    ri, ci = _iota((tc, tc), 0), _iota((tc, tc), 1)
    shift = int(np.log2(l))
    same = jnp.right_shift(ri, shift) == jnp.right_shift(ci, shift)
    lower = (ri >= ci) & same
    bc = jnp.dot(lower.astype(F32), log_alpha, precision=HIGHEST, preferred_element_type=F32)
    q = q_ref[0] * GLA_DK ** -0.5
    k = k_ref[0]
    v = v_ref[0]
    qe = q * jnp.exp(bc)
    ke = k * jnp.exp(-bc)
    for h in range(GLA_HEADS):
        cs = slice(h * K, (h + 1) * K)
        att = jnp.where(lower, _bdot_nt(qe[:, cs], ke[:, cs]), 0.0)
        o_h = _bdot(att, v[:, cs])
        s_t = st[h * V:(h + 1) * V, :]
        inter = []
        for c in range(tc // l):
            rs = slice(c * l, (c + 1) * l)
            inter.append(_bdot_nt(qe[rs, cs], s_t))
            b_end = bc[(c + 1) * l - 1:(c + 1) * l, cs]
            kd = k[rs, cs] * jnp.exp(b_end - bc[rs, cs])
            s_t = s_t * jnp.exp(b_end) + _bdot_tn(v[rs, cs], kd)
        st[h * V:(h + 1) * V, :] = s_t
        sn_ref[0, h * K:(h + 1) * K, :] = s_t.T
        o_h = o_h + (inter[0] if len(inter) == 1 else jnp.concatenate(inter, axis=0))
        o_h = o_h * lax.rsqrt(jnp.mean(o_h * o_h, -1, keepdims=True) + LN_EPS) * nw_ref[...]
        o_ref[0, :, cs] = o_h * jax.nn.silu(gg_ref[0, :, cs])


def _gla(gq3, gk3, gv3, gg3, sm3, s0, s_base, wa, ba, nw, tc):
    B, T, _ = gq3.shape
    l = min(GLA_CHUNK, T)
    tokspec = lambda a: pl.BlockSpec((1, tc, a.shape[2]), lambda b, t: (b, t, 0))
    full = lambda a: pl.BlockSpec(a.shape, lambda b, t: (0,) * a.ndim)
    st_spec = pl.BlockSpec((1, GLA_W, GLA_DV), lambda b, t: (b, 0, 0))
    return pl.pallas_call(
        functools.partial(_gla_body, tc=tc, l=l),
        grid=(B, T // tc),
        in_specs=[tokspec(gq3), tokspec(gk3), tokspec(gv3), tokspec(gg3), tokspec(sm3),
                  pl.BlockSpec((1, GLA_W, GLA_DV), lambda b, t: (s_base + b, 0, 0)),
                  full(wa), full(ba), full(nw)],
        out_specs=[tokspec(gq3), st_spec],
        out_shape=[jax.ShapeDtypeStruct(gq3.shape, F32), jax.ShapeDtypeStruct((B, GLA_W, GLA_DV), F32)],
        scratch_shapes=[pltpu.VMEM((GLA_HEADS * GLA_DV, GLA_DK), F32)],
        compiler_params=_cparams("parallel", "arbitrary"),
        name="gla",
    )(gq3, gk3, gv3, gg3, sm3, s0, wa, ba, nw)


EVEN_SEGS = (
    (0, EV_Q, (True,) * 4),
    (EV_Q, 4 * LANES, (True, False, True, False)),
    (EV_Q + 4 * LANES, 2 * LANES, (True, False)),
    (EV_Q + EV_KV, LRU_WIDTH, None),
    (EV_Q + EV_KV + LRU_WIDTH, LRU_WIDTH, None),
    (EV_Q + EV_KV + 2 * LRU_WIDTH, LANES, None),
)
ODD_SEGS = (
    (0, SSD_INNER, None),
    (SSD_INNER, SSD_CONV_DIM, None),
    (SSD_INNER + SSD_CONV_DIM, GLA_W, None),
    (SSD_INNER + SSD_CONV_DIM + GLA_W, GLA_W, None),
    (SSD_INNER + SSD_CONV_DIM + 2 * GLA_W, GLA_W, None),
    (SSD_INNER + SSD_CONV_DIM + 3 * GLA_W, GLA_W, None),
    (SSD_INNER + SSD_CONV_DIM + 4 * GLA_W, LANES, None),
)


def _even_w_in(w):
    a = EV_Q + EV_KV
    pad = jnp.zeros((D_MODEL, LANES - EV_GATE), w.dtype)
    return jnp.concatenate([w[:, :a], w[:, a + EV_GATE:], w[:, a:a + EV_GATE], pad], axis=1).astype(BF16)


def _odd_w_in(w):
    a = SSD_INNER + SSD_CONV_DIM
    dt = w[:, a:a + SSD_HEADS]
    rest = w[:, a + SSD_HEADS:a + SSD_HEADS + 4 * GLA_W]
    ga = w[:, a + SSD_HEADS + 4 * GLA_W:]
    pad = jnp.zeros((D_MODEL, LANES - SSD_HEADS - GLA_RANK), w.dtype)
    return jnp.concatenate([w[:, :a], rest, dt, ga, pad], axis=1).astype(BF16)


def _lane_pad(v):
    return jnp.pad(v.astype(F32), (0, LANES - v.shape[0]))[None, :]


def _prep_layer(layer, P):
    j = layer // 2
    d = dict(
        ffn_wup=P['ffn_w_up'][layer].astype(BF16), ffn_cw=P['ffn_conv_w'][layer], ffn_cb=P['ffn_conv_b'][layer][None, :],
        ffn_wdn=P['ffn_w_down'][layer].astype(BF16),
        ln_mix_g=P['ln_mix_g'][layer][None, :], ln_mix_b=P['ln_mix_b'][layer][None, :],
        ln_ffn_g=P['ln_ffn_g'][layer][None, :], ln_ffn_b=P['ln_ffn_b'][layer][None, :])
    if layer % 2 == 0:
        w_out = P['w_out_even'][j].astype(BF16)
        wg = P['lru_w_gates'][j]
        eye = jnp.eye(LRU_BLOCKS, dtype=F32)
        wg = jnp.einsum('knde,nm->kndme', wg, eye).reshape(2, LRU_WIDTH, LRU_WIDTH)
        d.update(
            w_in=_even_w_in(P['w_in_even'][j]), w_out1=w_out[:EV_Q], w_out2=w_out[EV_Q:],
            cmp_tab=jnp.repeat(P['nsa_cmp_w'][j].reshape(CMP_BLOCK, 2 * NSA_KV_HEADS), HEAD_DIM, axis=1),
            lru_cw=P['lru_conv_w'][j], lru_cb=P['lru_conv_b'][j][None, :],
            lru_wg=jnp.concatenate([wg[0], wg[1]], axis=1).astype(BF16),
            lru_bg=P['lru_b_gates'][j].reshape(1, 2 * LRU_WIDTH), lru_lam=P['lru_lambda'][j][None, :])
    else:
        w_out = P['w_out_odd'][j].astype(BF16)
        wa = jnp.zeros((LANES, GLA_W), F32).at[SSD_HEADS:SSD_HEADS + GLA_RANK].set(P['gla_w_alpha'][j])
        d.update(
            w_in=_odd_w_in(P['w_in_odd'][j]), w_out1=w_out[:SSD_INNER], w_out2=w_out[SSD_INNER:],
            ssd_cw=P['ssd_conv_w'][j], ssd_cb=P['ssd_conv_b'][j][None, :],
            ssd_dtb=_lane_pad(P['ssd_dt_bias'][j]), ssd_alog=_lane_pad(P['ssd_a_log'][j]),
            ssd_d=_lane_pad(P['ssd_d'][j]), ssd_nw=P['ssd_norm_w'][j][None, :],
            gla_wa=wa.astype(BF16), gla_ba=P['gla_b_alpha'][j][None, :], gla_nw=P['gla_norm_w'][j][None, :])
    return d


def _trunk(x3, prm, st, cfg):
    B, T, _ = x3.shape
    M = B * T
    out = dict(kv=[], win=[], lh=[], lc=[], sh=[], sc=[], gs=[], fc=[])
    for layer in range(DEPTH):
        p = prm[layer]
        j = layer // 2
        x2 = x3.reshape(M, D_MODEL)
        if layer % 2 == 0:
            q, rows, win, hx, hgate, hg = _proj(x2, p['w_in'], EVEN_SEGS, cfg['tm'], cfg['rope'], cfg['rope_blocks'])
            to3 = lambda a: a.reshape(B, T, a.shape[1])
            rows3, win3 = to3(rows), to3(win)
            if cfg['sample']:
                H3 = _compress_sample(st['cache3'], st['page_table'], j * st['n_phys'], p['cmp_tab'])
                o_nsa, win_keep = _nsa_sample(to3(q), to3(hg), H3, rows3, win3, st['cache3'], st['page_table'],
                                              j * st['n_phys'], st['cwin3'], j * B)
            else:
                H3 = _compress_prompt(rows3, p['cmp_tab'])
                o_nsa = _nsa_prompt(to3(q), to3(hg), H3, rows3, win3, cfg['tq'])
                win_keep = win3[:, T - min(WINDOW, T):]
            y_lru, h_last, conv_n = _lru(to3(hx), to3(hgate), st['lru_conv'], j * B, st['lru_h'], j * B,
                                         p['lru_cw'], p['lru_cb'], p['lru_wg'], p['lru_bg'], p['lru_lam'],
                                         cfg['lru_nb'], cfg['lru_tc'])
            a1, a2 = o_nsa.reshape(M, EV_Q), y_lru.reshape(M, LRU_WIDTH)
            out['kv'].append(rows3.reshape(B, T, 4, NSA_KV_HEADS, HEAD_DIM))
            out['win'].append(win_keep.reshape(B, -1, 2, NSA_KV_HEADS, HEAD_DIM))
            out['lh'].append(h_last.reshape(B, LRU_WIDTH))
            out['lc'].append(conv_n)
        else:
            z, xbc, gq, gk, gv, gg, sm = _proj(x2, p['w_in'], ODD_SEGS, cfg['tm'])
            to3 = lambda a: a.reshape(B, T, a.shape[1])
            sm3 = to3(sm)
            y_ssd, h_n, conv_n = _ssd(to3(xbc), to3(z), sm3, st['ssd_conv'], j * B, st['ssd_h'], j * B,
                                      p['ssd_cw'], p['ssd_cb'], p['ssd_dtb'], p['ssd_alog'], p['ssd_d'], p['ssd_nw'],
                                      cfg['ssd_L'])
            o_gla, s_n = _gla(to3(gq), to3(gk), to3(gv), to3(gg), sm3, st['gla_s'], j * B,
                              p['gla_wa'], p['gla_ba'], p['gla_nw'], cfg['gla_tc'])
            a1, a2 = y_ssd.reshape(M, SSD_INNER), o_gla.reshape(M, GLA_W)
            out['sh'].append(h_n.reshape(B, SSD_HEADS, SSD_HEAD_DIM, SSD_STATE))
            out['sc'].append(conv_n)
            out['gs'].append(s_n.reshape(B, GLA_HEADS, GLA_DK, GLA_DV))
        x2 = _outproj(a1, a2, p['w_out1'], p['w_out2'], x2, p['ln_mix_g'], p['ln_mix_b'], cfg['tm'])
        x3, fbuf = _ffn(x2.reshape(B, T, D_MODEL), p['ffn_wup'], p['ffn_cw'], p['ffn_cb'], p['ffn_wdn'],
                        st['ffn_conv'], layer * B, p['ln_ffn_g'], p['ln_ffn_b'],
                        cfg['ffn_nb'], cfg['ffn_tc'], cfg['ffn_cw'])
        out['fc'].append(fbuf.reshape(B, FFN_CONV - 1, 2 * D_FF))
    return x3, {k: jnp.stack(v) for k, v in out.items()}


def _largest_tile(n, cap):
    t = min(n, cap)
    while n % t:
        t //= 2
    return t


def kernel(x_prompt, x_sample, cache_nsa_kv, cache_nsa_win, state_lru_h, state_lru_conv, state_ssd, state_ssd_conv, state_gla, state_ffn_conv, page_table, w_in_even, w_out_even, nsa_cmp_w, lru_conv_w, lru_conv_b, lru_w_gates, lru_b_gates, lru_lambda, w_in_odd, w_out_odd, ssd_conv_w, ssd_conv_b, ssd_dt_bias, ssd_a_log, ssd_d, ssd_norm_w, gla_w_alpha, gla_b_alpha, gla_norm_w, ffn_w_up, ffn_conv_w, ffn_conv_b, ffn_w_down, ln_mix_g, ln_mix_b, ln_ffn_g, ln_ffn_b):
    P = dict(w_in_even=w_in_even, w_out_even=w_out_even, nsa_cmp_w=nsa_cmp_w, lru_conv_w=lru_conv_w,
             lru_conv_b=lru_conv_b, lru_w_gates=lru_w_gates, lru_b_gates=lru_b_gates, lru_lambda=lru_lambda,
             w_in_odd=w_in_odd, w_out_odd=w_out_odd, ssd_conv_w=ssd_conv_w, ssd_conv_b=ssd_conv_b,
             ssd_dt_bias=ssd_dt_bias, ssd_a_log=ssd_a_log, ssd_d=ssd_d, ssd_norm_w=ssd_norm_w,
             gla_w_alpha=gla_w_alpha, gla_b_alpha=gla_b_alpha, gla_norm_w=gla_norm_w, ffn_w_up=ffn_w_up,
             ffn_conv_w=ffn_conv_w, ffn_conv_b=ffn_conv_b, ffn_w_down=ffn_w_down, ln_mix_g=ln_mix_g,
             ln_mix_b=ln_mix_b, ln_ffn_g=ln_ffn_g, ln_ffn_b=ln_ffn_b)
    prm = [_prep_layer(layer, P) for layer in range(DEPTH)]
    n_even, n_odd = w_in_even.shape[0], w_in_odd.shape[0]
    U2 = 2 * (2 * D_FF)

    Bp, Tp, _ = x_prompt.shape
    tm_p = _largest_tile(Tp, 256)
    st_p = dict(
        lru_conv=jnp.zeros((n_even * Bp, SHORT_CONV - 1, LRU_WIDTH), F32), lru_h=jnp.zeros((n_even * Bp, 1, LRU_WIDTH), F32),
        ssd_conv=jnp.zeros((n_odd * Bp, SHORT_CONV - 1, SSD_CONV_DIM), F32),
        ssd_h=jnp.zeros((n_odd * Bp, SSD_INNER, SSD_STATE), F32), gla_s=jnp.zeros((n_odd * Bp, GLA_W, GLA_DV), F32),
        ffn_conv=jnp.zeros((DEPTH * Bp, 1, U2), F32))
    cfg_p = dict(sample=False, tm=tm_p, rope=_rope_tables(jnp.arange(Tp)), rope_blocks=Tp // tm_p,
                 tq=_largest_tile(Tp, 128), lru_nb=1, lru_tc=_largest_tile(Tp, 512),
                 ssd_L=_largest_tile(Tp, 128), gla_tc=_largest_tile(Tp, 256),
                 ffn_nb=1, ffn_tc=_largest_tile(Tp, 256), ffn_cw=D_FF // 2)
    y_p, o_p = _trunk(x_prompt, prm, st_p, cfg_p)

    Bs, Ts, _ = x_sample.shape
    n_phys = cache_nsa_kv.shape[1]
    past_len = page_table.shape[1] * PAGE_SIZE
    tm_s = _largest_tile(Bs * Ts, 256)
    pos_s = past_len + jnp.arange(tm_s) % Ts
    nb_s = _largest_tile(Bs, 32)
    st_s = dict(
        cache3=cache_nsa_kv.reshape(n_even * n_phys, PAGE_SIZE, 4 * LANES), n_phys=n_phys, page_table=page_table,
        cwin3=cache_nsa_win.reshape(n_even * Bs, cache_nsa_win.shape[2], 2 * LANES),
        lru_conv=state_lru_conv.reshape(n_even * Bs, SHORT_CONV - 1, LRU_WIDTH),
        lru_h=state_lru_h.reshape(n_even * Bs, 1, LRU_WIDTH),
        ssd_conv=state_ssd_conv.reshape(n_odd * Bs, SHORT_CONV - 1, SSD_CONV_DIM),
        ssd_h=state_ssd.reshape(n_odd * Bs, SSD_INNER, SSD_STATE),
        gla_s=state_gla.reshape(n_odd * Bs, GLA_W, GLA_DV),
        ffn_conv=state_ffn_conv.reshape(DEPTH * Bs, 1, U2))
    cfg_s = dict(sample=True, tm=tm_s, rope=_rope_tables(pos_s), rope_blocks=1,
                 lru_nb=nb_s, lru_tc=Ts, ssd_L=Ts, gla_tc=Ts, ffn_nb=nb_s, ffn_tc=Ts, ffn_cw=D_FF // 2)
    y_s, o_s = _trunk(x_sample, prm, st_s, cfg_s)

    return (y_p, y_s, o_p['kv'], o_s['kv'], o_p['win'], o_s['win'], o_p['lh'], o_s['lh'], o_p['lc'], o_s['lc'],
            o_p['sh'], o_s['sh'], o_p['sc'], o_s['sc'], o_p['gs'], o_s['gs'], o_p['fc'], o_s['fc'])
```

```python
import functools

import jax
import jax.numpy as jnp
import numpy as np
from jax import lax
from jax.experimental import pallas as pl
from jax.experimental.pallas import tpu as pltpu

F32 = jnp.float32
BF16 = jnp.bfloat16
HIGHEST = lax.Precision.HIGHEST

D_MODEL = 1024
DEPTH = 4
PAGE_SIZE = 128
HEAD_DIM = 64
ROPE_DIM = HEAD_DIM // 4
ROPE_THETA = 500000.0
NSA_HEADS = 8
NSA_KV_HEADS = 2
NSA_GROUP = NSA_HEADS // NSA_KV_HEADS
CMP_BLOCK = 32
CMP_STRIDE = 16
SEL_BLOCK = 64
N_SEL = 8
WINDOW = 512
LRU_WIDTH = D_MODEL // 2
LRU_BLOCKS = 8
LRU_BLOCK_DIM = LRU_WIDTH // LRU_BLOCKS
LRU_C = 8.0
SHORT_CONV = 4
SSD_HEADS = 16
SSD_HEAD_DIM = 64
SSD_INNER = SSD_HEADS * SSD_HEAD_DIM
SSD_GROUPS = 2
SSD_STATE = 128
SSD_CONV_DIM = SSD_INNER + 2 * SSD_GROUPS * SSD_STATE
GLA_HEADS = 4
GLA_DK = 128
GLA_DV = 128
GLA_RANK = 16
GLA_TAU = 16.0
GLA_CHUNK = 32
D_FF = 2816
FFN_CONV = 3
ALPHA = (2.0 * DEPTH) ** 0.25
LN_EPS = 1e-5
NEG = -1e30
EV_Q = NSA_HEADS * HEAD_DIM
EV_KV = 6 * NSA_KV_HEADS * HEAD_DIM
EV_GATE = 3 * NSA_HEADS
GLA_W = GLA_HEADS * GLA_DK

LANES = 128
SUBLANES = 8
V7X_VMEM_BYTES = 64 * 1024 * 1024
VMEM_LIMIT = V7X_VMEM_BYTES - 8 * 1024 * 1024
HALO = SUBLANES


def _cparams(*sem):
    return pltpu.CompilerParams(dimension_semantics=sem, vmem_limit_bytes=VMEM_LIMIT)


def _bdot(a, b):
    return jnp.dot(a.astype(BF16), b.astype(BF16), preferred_element_type=F32)


def _bdot_nt(a, b):
    return lax.dot_general(a.astype(BF16), b.astype(BF16), (((1,), (1,)), ((), ())), preferred_element_type=F32)


def _bdot_tn(a, b):
    return lax.dot_general(a.astype(BF16), b.astype(BF16), (((0,), (0,)), ((), ())), preferred_element_type=F32)


def _iota(shape, axis):
    return lax.broadcasted_iota(jnp.int32, shape, axis)


def _layer_norm(y, g, b):
    mu = jnp.mean(y, -1, keepdims=True)
    d = y - mu
    var = jnp.mean(d * d, -1, keepdims=True)
    return d * lax.rsqrt(var + LN_EPS) * g + b


def _masked_softmax(s, mask):
    s = jnp.where(mask, s, NEG)
    m = jnp.max(s, -1, keepdims=True)
    e = jnp.exp(s - m)
    p = e / jnp.sum(e, -1, keepdims=True)
    return jnp.where(mask, p, 0.0)


def _proj_body(*refs, segs, has_rope):
    if has_rope:
        x_ref, w_ref, c_ref, s1_ref, s2_ref = refs[:5]
        out_refs = refs[5:]
    else:
        x_ref, w_ref = refs[:2]
        out_refs = refs[2:]
    xb = x_ref[...].astype(BF16)
    for o_ref, (start, width, rope) in zip(out_refs, segs):
        acc = jnp.dot(xb, w_ref[:, start:start + width], preferred_element_type=F32)
        if rope is None:
            o_ref[...] = acc
            continue
        for c, flag in enumerate(rope):
            chunk = acc[:, c * LANES:(c + 1) * LANES]
            if flag:
                chunk = (chunk * c_ref[...] + pltpu.roll(chunk, LANES - ROPE_DIM // 2, 1) * s2_ref[...]
                         + pltpu.roll(chunk, ROPE_DIM // 2, 1) * s1_ref[...])
            o_ref[:, c * LANES:(c + 1) * LANES] = chunk


def _proj(x2d, w, segs, tm, tabs=None, tab_blocks=1):
    M, K = x2d.shape
    N = w.shape[1]
    has_rope = tabs is not None
    in_specs = [pl.BlockSpec((tm, K), lambda i: (i, 0)), pl.BlockSpec((K, N), lambda i: (0, 0))]
    args = [x2d, w]
    if has_rope:
        in_specs += [pl.BlockSpec((tm, LANES), lambda i: (i % tab_blocks, 0))] * 3
        args += list(tabs)
    return pl.pallas_call(
        functools.partial(_proj_body, segs=segs, has_rope=has_rope),
        grid=(M // tm,),
        in_specs=in_specs,
        out_specs=[pl.BlockSpec((tm, wd), lambda i: (i, 0)) for _, wd, _ in segs],
        out_shape=[jax.ShapeDtypeStruct((M, wd), F32) for _, wd, _ in segs],
        compiler_params=_cparams("parallel"),
        name="proj",
    )(*args)


def _rope_tables(pos):
    half = ROPE_DIM // 2
    inv = jnp.power(ROPE_THETA, -2.0 * jnp.arange(half, dtype=F32) / ROPE_DIM)
    ang = pos.astype(F32)[:, None] * inv[None, :]
    cos, sin = jnp.cos(ang), jnp.sin(ang)
    R = pos.shape[0]
    rest = HEAD_DIM - ROPE_DIM
    c64 = jnp.concatenate([cos, cos, jnp.ones((R, rest), F32)], 1)
    s1 = jnp.concatenate([jnp.zeros((R, half), F32), sin, jnp.zeros((R, rest), F32)], 1)
    s2 = jnp.concatenate([-sin, jnp.zeros((R, half + rest), F32)], 1)
    return tuple(jnp.concatenate([t, t], 1) for t in (c64, s1, s2))


def _compress_rows(x, w_ref):
    x3 = x.reshape(x.shape[0] // CMP_STRIDE, CMP_STRIDE, 2 * LANES)
    return (jnp.sum(x3 * w_ref[0:CMP_STRIDE, :][None], axis=1),
            jnp.sum(x3 * w_ref[CMP_STRIDE:CMP_BLOCK, :][None], axis=1))


def _compress_body(kv_ref, w_ref, o_ref):
    h0, h1 = _compress_rows(kv_ref[0], w_ref)
    o_ref[0, :, 0:2 * LANES] = h0
    o_ref[0, :, 2 * LANES:4 * LANES] = h1


def _compress_prompt(rows3, wtab):
    B, T, _ = rows3.shape
    return pl.pallas_call(
        _compress_body,
        grid=(B,),
        in_specs=[pl.BlockSpec((1, T, 2 * LANES), lambda b: (b, 0, 0)),
                  pl.BlockSpec((CMP_BLOCK, 2 * LANES), lambda b: (0, 0))],
        out_specs=pl.BlockSpec((1, T // CMP_STRIDE, 4 * LANES), lambda b: (b, 0, 0)),
        out_shape=jax.ShapeDtypeStruct((B, T // CMP_STRIDE, 4 * LANES), F32),
        compiler_params=_cparams("parallel"),
        name="compress_prompt",
    )(rows3, wtab)


def _overlap_matrix(n_rows, n_cmp, n_sel):
    s1 = np.arange(n_rows)[:, None] * CMP_STRIDE
    s2 = np.arange(LANES)[None, :] * SEL_BLOCK
    ov = np.clip(np.minimum(s1 + CMP_BLOCK, s2 + SEL_BLOCK) - np.maximum(s1, s2), 0, None) / CMP_BLOCK
    ov = ov * (np.arange(n_rows)[:, None] < n_cmp) * (np.arange(LANES)[None, :] < n_sel)
    return jnp.asarray(ov, dtype=F32)


def _select_blocks(psum, ov, t_col, n_sel):
    imp = jnp.dot(psum, ov, precision=HIGHEST, preferred_element_type=F32)
    blk = _iota(imp.shape, 1)
    cur = jnp.right_shift(t_col, int(np.log2(SEL_BLOCK)))
    future = blk * SEL_BLOCK > t_col
    forced = (blk == 0) | (blk == cur) | (blk == cur - 1)
    w = jnp.where(future, -1.0, jnp.where(forced, 1e6, imp))
    w = jnp.where(blk < n_sel, w, -jnp.inf)
    sel = jnp.zeros(imp.shape, F32)
    for _ in range(min(N_SEL, n_sel)):
        m = jnp.max(w, axis=-1, keepdims=True)
        idx = jnp.min(jnp.where(w == m, blk, LANES), axis=-1, keepdims=True)
        hit = blk == idx
        sel = jnp.where(hit, 1.0, sel)
        w = jnp.where(hit, -jnp.inf, w)
    return sel


def _stack_heads(q, h):
    G = NSA_GROUP
    qh = jnp.concatenate([q[:, (h * G + g) * HEAD_DIM:(h * G + g + 1) * HEAD_DIM] for g in range(G)], axis=0)
    return (qh * HEAD_DIM ** -0.5).astype(BF16)


def _expand_matrix(K):
    blk = jnp.right_shift(_iota((LANES, K), 1), int(np.log2(SEL_BLOCK)))
    return (blk == _iota((LANES, K), 0)).astype(BF16)


def _attend(qb, chunks, mask, tq):
    G = NSA_GROUP
    s = [_bdot(qb, k) if tr else _bdot_nt(qb, k) for tr, k, _ in chunks]
    s = s[0] if len(s) == 1 else jnp.concatenate(s, axis=-1)
    K = s.shape[-1]
    s = jnp.where(mask[None], s.reshape(G, tq, K), NEG)
    e = jnp.exp(s - jnp.max(s, -1, keepdims=True))
    denom = jnp.sum(e, -1, keepdims=True).reshape(G * tq, 1)
    eb = e.reshape(G * tq, K).astype(BF16)
    o, off = None, 0
    for tr, _, v in chunks:
        kc = v.shape[1] if tr else v.shape[0]
        part = _bdot_nt(eb[:, off:off + kc], v) if tr else _bdot(eb[:, off:off + kc], v)
        o = part if o is None else o + part
        off += kc
    return o / denom


def _gate_store(o_ref, sg, h, tq, o_c, o_s, o_w):
    for g in range(NSA_GROUP):
        hh = h * NSA_GROUP + g
        r = slice(g * tq, (g + 1) * tq)
        o_ref[:, hh * HEAD_DIM:(hh + 1) * HEAD_DIM] = (
            sg[:, 3 * hh:3 * hh + 1] * o_c[r] + sg[:, 3 * hh + 1:3 * hh + 2] * o_s[r]
            + sg[:, 3 * hh + 2:3 * hh + 3] * o_w[r])


def _nsa_core(items, ov, t_col, n_sel):
    tq = t_col.shape[0]
    G = NSA_GROUP
    o_cs, psums = [], []
    for it in items:
        n_rows = it['ckk'].shape[0]
        p_c = _masked_softmax(_bdot_nt(it['qb'], it['ckk']).reshape(G, tq, n_rows), it['mask_c'][None])
        o_cs.append(_bdot(p_c.reshape(G * tq, n_rows), it['ckv']))
        psum = p_c[0]
        for g in range(1, G):
            psum = psum + p_c[g]
        psums.append(psum)
    n = len(items)
    sel = _select_blocks(jnp.concatenate(psums, axis=0), ov, jnp.concatenate([t_col] * n, axis=0), n_sel)
    for i, it in enumerate(items):
        o_s = it['slc'](it['qb'], sel[i * tq:(i + 1) * tq])
        it['finish'](o_cs[i], o_s, it['win'](it['qb']))


def _combine_compressed(h0, h1):
    return h0 + pltpu.roll(h1, h1.shape[0] - 1, 0)


def _nsa_prompt_body(q_ref, hg_ref, H_ref, rows_ref, win_ref, ov_ref, o_ref, *, T, tq):
    qi = pl.program_id(1)
    n_rows = T // CMP_STRIDE
    n_cmp = n_rows - CMP_BLOCK // CMP_STRIDE + 1
    n_sel = T // SEL_BLOCK
    t_col = qi * tq + _iota((tq, 1), 0)
    ck = _combine_compressed(H_ref[0, :, 0:2 * LANES], H_ref[0, :, 2 * LANES:4 * LANES])
    ncol = _iota((1, n_rows), 1)
    mask_c = (ncol * CMP_STRIDE + CMP_BLOCK - 1 <= t_col) & (ncol < n_cmp)
    band = WINDOW + tq
    wstart = pl.multiple_of(jnp.clip(qi * tq - WINDOW, 0, T - band), tq)
    wpos = wstart + _iota((1, band), 1)
    mask_w = (wpos <= t_col) & (wpos > t_col - WINDOW)
    q = q_ref[0]
    sg = jax.nn.sigmoid(hg_ref[0])
    kstep = min(T, 4 * tq)

    def make_item(h):
        def slc_span(K):
            def run(qb, sel):
                allowed = jnp.dot(sel.astype(BF16), _expand_matrix(K), preferred_element_type=F32) > 0.5
                mask = allowed & (_iota((1, K), 1) <= t_col)
                k = rows_ref[0, 0:K, 2 * LANES + h * HEAD_DIM:2 * LANES + (h + 1) * HEAD_DIM]
                v = rows_ref[0, 0:K, 3 * LANES + h * HEAD_DIM:3 * LANES + (h + 1) * HEAD_DIM]
                return _attend(qb, [(False, k, v)], mask, tq)
            return run

        def slc(qb, sel):
            spans = [slc_span((i + 1) * kstep) for i in range(T // kstep)]
            if len(spans) == 1:
                return spans[0](qb, sel)
            return lax.switch(lax.div(qi * tq, kstep), spans, qb, sel)

        def win(qb):
            k = win_ref[0, pl.ds(wstart, band), h * HEAD_DIM:(h + 1) * HEAD_DIM]
            v = win_ref[0, pl.ds(wstart, band), LANES + h * HEAD_DIM:LANES + (h + 1) * HEAD_DIM]
            return _attend(qb, [(False, k, v)], mask_w, tq)

        return dict(qb=_stack_heads(q, h), ckk=ck[:, h * HEAD_DIM:(h + 1) * HEAD_DIM],
                    ckv=ck[:, LANES + h * HEAD_DIM:LANES + (h + 1) * HEAD_DIM], mask_c=mask_c,
                    slc=slc, win=win, finish=functools.partial(_gate_store, o_ref.at[0], sg, h, tq))

    _nsa_core([make_item(h) for h in range(NSA_KV_HEADS)], ov_ref[...], t_col, n_sel)


def _nsa_prompt(q3, hg3, H3, rows3, win3, tq):
    B, T, _ = q3.shape
    n_rows = T // CMP_STRIDE
    ov = _overlap_matrix(n_rows, n_rows - 1, T // SEL_BLOCK)
    return pl.pallas_call(
        functools.partial(_nsa_prompt_body, T=T, tq=tq),
        grid=(B, T // tq),
        in_specs=[pl.BlockSpec((1, tq, EV_Q), lambda b, i: (b, i, 0)),
                  pl.BlockSpec((1, tq, LANES), lambda b, i: (b, i, 0)),
                  pl.BlockSpec((1, n_rows, 4 * LANES), lambda b, i: (b, 0, 0)),
                  pl.BlockSpec((1, T, 4 * LANES), lambda b, i: (b, 0, 0)),
                  pl.BlockSpec((1, T, 2 * LANES), lambda b, i: (b, 0, 0)),
                  pl.BlockSpec((n_rows, LANES), lambda b, i: (0, 0))],
        out_specs=pl.BlockSpec((1, tq, EV_Q), lambda b, i: (b, i, 0)),
        out_shape=jax.ShapeDtypeStruct((B, T, EV_Q), F32),
        compiler_params=_cparams("parallel", "arbitrary"),
        name="nsa_prompt",
    )(q3, hg3, H3, rows3, win3, ov)


def _nsa_sample_body(*refs, nb, n_pages, Tq, Wb):
    q_ref, hg_ref, rnew_ref, wnew_ref, ctab_ref, ov_ref = refs[1:7]
    page_refs = refs[7:7 + nb * n_pages]
    cwin_ref, o_ref, wout_ref, knew, wnew = refs[7 + nb * n_pages:]
    P = n_pages * PAGE_SIZE
    Ks = P + PAGE_SIZE
    Kw = Wb + PAGE_SIZE
    n_rows = P // CMP_STRIDE
    n_cmp = n_rows - CMP_BLOCK // CMP_STRIDE + 1
    n_sel = -(-(P + Tq) // SEL_BLOCK)
    tt = _iota((Tq, 1), 0)
    t_col = P + tt
    mask_c = (_iota((1, n_rows), 1) < n_cmp) & (tt >= 0)
    causal = _iota((1, Ks), 1) <= t_col
    wpos = _iota((1, Kw), 1)
    mask_w = (wpos <= Wb + tt) & (wpos > Wb + tt - WINDOW)
    expand = _expand_matrix(Ks)
    pad_rows = jnp.zeros((PAGE_SIZE - Tq, 2 * LANES), F32)

    n_grp = 2 * NSA_KV_HEADS
    per_page = PAGE_SIZE // CMP_STRIDE

    def make_item(i, h, cks, sg, pages):
        ksl = slice(h * HEAD_DIM, (h + 1) * HEAD_DIM)
        vsl = slice(LANES + h * HEAD_DIM, LANES + (h + 1) * HEAD_DIM)

        def slc(qb, sel):
            mask = (jnp.dot(sel.astype(BF16), expand, preferred_element_type=F32) > 0.5) & causal
            chunks = [(True, pr[0, 2 * LANES + ksl.start:2 * LANES + ksl.stop, :],
                       pr[0, 2 * LANES + vsl.start:2 * LANES + vsl.stop, :]) for pr in pages]
            chunks.append((False, knew[i, :, ksl], knew[i, :, vsl]))
            return _attend(qb, chunks, mask, Tq)

        def win(qb):
            chunks = [(True, cwin_ref[i, ksl, :], cwin_ref[i, vsl, :]), (False, wnew[i, :, ksl], wnew[i, :, vsl])]
            return _attend(qb, chunks, mask_w, Tq)

        return dict(qb=_stack_heads(q_ref[i], h), ckk=cks[h], ckv=cks[NSA_KV_HEADS + h], mask_c=mask_c,
                    slc=slc, win=win, finish=functools.partial(_gate_store, o_ref.at[i], sg, h, Tq))

    items = []
    new_lanes = _iota((1, LANES), 1) >= LANES - Tq
    for i in range(nb):
        knew[i, 0:Tq, :] = rnew_ref[i, :, 2 * LANES:4 * LANES]
        knew[i, Tq:PAGE_SIZE, :] = pad_rows
        wnew[i, 0:Tq, :] = wnew_ref[i]
        wnew[i, Tq:PAGE_SIZE, :] = pad_rows
        shifted = pltpu.roll(cwin_ref[i], Wb - Tq, 1)
        new_t = pltpu.roll(wnew[i].T, LANES - Tq, 1)
        wout_ref[i, :, 0:Wb - LANES] = shifted[:, 0:Wb - LANES]
        wout_ref[i, :, Wb - LANES:Wb] = jnp.where(new_lanes, new_t, shifted[:, Wb - LANES:Wb])
        pages = page_refs[i * n_pages:(i + 1) * n_pages]
        parts = [_bdot_nt(ctab_ref[...], pr[0, 0:2 * LANES, :]) for pr in pages]
        cks = []
        for g in range(n_grp):
            r0, cs = g * 2 * per_page, slice(g * HEAD_DIM, (g + 1) * HEAD_DIM)
            cks.append(_combine_compressed(
                jnp.concatenate([pp[r0:r0 + per_page, cs] for pp in parts], axis=0),
                jnp.concatenate([pp[r0 + per_page:r0 + 2 * per_page, cs] for pp in parts], axis=0)))
        sg = jax.nn.sigmoid(hg_ref[i])
        items += [make_item(i, h, cks, sg, pages) for h in range(NSA_KV_HEADS)]
    _nsa_core(items, ov_ref[...], t_col, n_sel)


def _page_compress_table(cmp_w):
    per_page = PAGE_SIZE // CMP_STRIDE
    w = cmp_w.reshape(CMP_BLOCK, 2 * NSA_KV_HEADS).T
    s = jnp.arange(PAGE_SIZE)[None, :] - CMP_STRIDE * jnp.arange(per_page)[:, None]
    inside = (s >= 0) & (s < CMP_STRIDE)
    sc = jnp.clip(s, 0, CMP_STRIDE - 1)
    first = jnp.where(inside[None], w[:, sc], 0.0)
    second = jnp.where(inside[None], w[:, CMP_STRIDE + sc], 0.0)
    return jnp.concatenate([first, second], axis=1).reshape(-1, PAGE_SIZE).astype(BF16)


def _nsa_sample(q3, hg3, rows3, win3, cache_t, page_table, page_base, cwin_t, cwin_base, ctab, nb):
    B, Tq, _ = q3.shape
    n_pages = page_table.shape[1]
    P = n_pages * PAGE_SIZE
    Wb = cwin_t.shape[2]
    assert Wb % LANES == 0 and Wb > LANES and Tq <= SUBLANES and B % nb == 0
    n_rows = P // CMP_STRIDE
    n_sel = -(-(P + Tq) // SEL_BLOCK)
    ov = _overlap_matrix(n_rows, n_rows - 1, n_sel)
    cb0 = cwin_base // nb

    def page_spec(i, p):
        return pl.BlockSpec((1, 4 * LANES, PAGE_SIZE), lambda b, pt: (page_base + pt[b * nb + i, p], 0, 0))

    tok = lambda a: pl.BlockSpec((nb, Tq, a.shape[2]), lambda b, pt: (b, 0, 0))
    grid_spec = pltpu.PrefetchScalarGridSpec(
        num_scalar_prefetch=1,
        grid=(B // nb,),
        in_specs=[tok(q3), tok(hg3), tok(rows3), tok(win3),
                  pl.BlockSpec(ctab.shape, lambda b, pt: (0, 0)), pl.BlockSpec(ov.shape, lambda b, pt: (0, 0))]
        + [page_spec(i, p) for i in range(nb) for p in range(n_pages)]
        + [pl.BlockSpec((nb, 2 * LANES, Wb), lambda b, pt: (cb0 + b, 0, 0))],
        out_specs=[tok(q3), pl.BlockSpec((nb, 2 * LANES, Wb), lambda b, pt: (b, 0, 0))],
        scratch_shapes=[pltpu.VMEM((nb, PAGE_SIZE, 2 * LANES), F32), pltpu.VMEM((nb, PAGE_SIZE, 2 * LANES), F32)],
    )
    return pl.pallas_call(
        functools.partial(_nsa_sample_body, nb=nb, n_pages=n_pages, Tq=Tq, Wb=Wb),
        grid_spec=grid_spec,
        out_shape=[jax.ShapeDtypeStruct((B, Tq, EV_Q), F32), jax.ShapeDtypeStruct((B, 2 * LANES, Wb), F32)],
        compiler_params=_cparams("arbitrary"),
        name="nsa_sample",
    )(page_table, q3, hg3, rows3, win3, ctab, ov, *([cache_t] * (nb * n_pages)), cwin_t)


def _causal_conv(xp, w_ref, b_ref, tc, cols=None):
    K = w_ref.shape[0]
    cs = slice(None) if cols is None else cols
    acc = None
    for j in range(K):
        term = w_ref[j:j + 1, cs][None] * xp[:, HALO - (K - 1) + j:HALO - (K - 1) + j + tc, :]
        acc = term if acc is None else acc + term
    return b_ref[:, cs][None] + acc


def _lru_body(hx_ref, hgate_ref, conv0_ref, h0_ref, cw_ref, cb_ref, wg_ref, bg_ref, lam_ref,
              y_ref, hlast_ref, convn_ref, xp, *, nb, tc):
    K1 = SHORT_CONV - 1

    @pl.when(pl.program_id(1) == 0)
    def _():
        xp[:, HALO - K1:HALO, :] = conv0_ref[...]
        hlast_ref[...] = h0_ref[...]

    xp[:, HALO:HALO + tc, :] = hx_ref[...]
    xc = _causal_conv(xp, cw_ref, cb_ref, tc)
    tail = xp[:, HALO + tc - K1:HALO + tc, :]
    convn_ref[...] = tail
    xp[:, HALO - K1:HALO, :] = tail
    R = nb * tc
    xc2 = xc.reshape(R, LRU_WIDTH)
    gt = _bdot(xc2, wg_ref[...]) + bg_ref[...]
    r_gate = jax.nn.sigmoid(gt[:, :LRU_WIDTH])
    i_gate = jax.nn.sigmoid(gt[:, LRU_WIDTH:])
    log_a = -LRU_C * r_gate * jax.nn.softplus(-lam_ref[...])
    a = jnp.exp(log_a)
    th = jnp.tanh(log_a)
    u = jnp.sqrt(-2.0 * th / (1.0 - th)) * i_gate * xc2
    tpos = lax.rem(_iota((R, 1), 0), tc)
    d = 1
    while d < tc:
        valid = tpos >= d
        u = jnp.where(valid, a * pltpu.roll(u, d, 0) + u, u)
        a = jnp.where(valid, a * pltpu.roll(a, d, 0), a)
        d *= 2
    hprev = jnp.broadcast_to(hlast_ref[...], (nb, tc, LRU_WIDTH)).reshape(R, LRU_WIDTH)
    h = a * hprev + u
    y_ref[...] = (h * jax.nn.gelu(hgate_ref[...].reshape(R, LRU_WIDTH))).reshape(nb, tc, LRU_WIDTH)
    last = _iota((1, tc, 1), 1) == tc - 1
    hlast_ref[...] = jnp.sum(jnp.where(last, h.reshape(nb, tc, LRU_WIDTH), 0.0), axis=1, keepdims=True)


def _lru(hx3, hgate3, conv0, conv_base, h0, h_base, cw, cb, wg, bg, lam, nb, tc):
    B, T, C = hx3.shape
    K1 = SHORT_CONV - 1
    cb0 = conv_base // nb
    hb0 = h_base // nb
    tok = pl.BlockSpec((nb, tc, C), lambda b, t: (b, t, 0))
    full = lambda a: pl.BlockSpec(a.shape, lambda b, t: (0,) * a.ndim)
    return pl.pallas_call(
        functools.partial(_lru_body, nb=nb, tc=tc),
        grid=(B // nb, T // tc),
        in_specs=[tok, tok,
                  pl.BlockSpec((nb, K1, C), lambda b, t: (cb0 + b, 0, 0)),
                  pl.BlockSpec((nb, 1, C), lambda b, t: (hb0 + b, 0, 0)),
                  full(cw), full(cb), full(wg), full(bg), full(lam)],
        out_specs=[tok, pl.BlockSpec((nb, 1, C), lambda b, t: (b, 0, 0)),
                   pl.BlockSpec((nb, K1, C), lambda b, t: (b, 0, 0))],
        out_shape=[jax.ShapeDtypeStruct((B, T, C), F32), jax.ShapeDtypeStruct((B, 1, C), F32),
                   jax.ShapeDtypeStruct((B, K1, C), F32)],
        scratch_shapes=[pltpu.VMEM((nb, HALO + tc, C), F32)],
        compiler_params=_cparams("parallel", "arbitrary"),
        name="lru",
    )(hx3, hgate3, conv0, h0, cw, cb, wg, bg, lam)


def _outproj_body(a1_ref, a2_ref, w1_ref, w2_ref, x_ref, g_ref, b_ref, o_ref):
    mix = _bdot(a1_ref[...], w1_ref[...]) + _bdot(a2_ref[...], w2_ref[...])
    o_ref[...] = _layer_norm(ALPHA * x_ref[...] + mix, g_ref[...], b_ref[...])


def _outproj(a1, a2, w1, w2, x2d, g, b, tm):
    M = x2d.shape[0]
    row = lambda a: pl.BlockSpec((tm, a.shape[1]), lambda i: (i, 0))
    full = lambda a: pl.BlockSpec(a.shape, lambda i: (0, 0))
    return pl.pallas_call(
        _outproj_body,
        grid=(M // tm,),
        in_specs=[row(a1), row(a2), full(w1), full(w2), row(x2d), full(g), full(b)],
        out_specs=row(x2d),
        out_shape=jax.ShapeDtypeStruct(x2d.shape, F32),
        compiler_params=_cparams("parallel"),
        name="outproj_ln",
    )(a1, a2, w1, w2, x2d, g, b)


def _ffn_body(x_ref, wup_ref, cw_ref, cb_ref, wdn_ref, buf0_ref, g_ref, b_ref, y_ref, bufn_ref, sg, sv,
              *, nb, tc, cw):
    K1 = FFN_CONV - 1

    @pl.when(pl.program_id(1) == 0)
    def _():
        bufn_ref[...] = buf0_ref[...]

    R = nb * tc
    x = x_ref[...].reshape(R, D_MODEL)
    xb = x.astype(BF16)
    acc = jnp.zeros((R, D_MODEL), F32)
    for c in range(D_FF // cw):
        conv = []
        for half, scr in ((0, sg), (1, sv)):
            cols = slice(half * D_FF + c * cw, half * D_FF + (c + 1) * cw)
            u = jnp.dot(xb, wup_ref[:, cols], preferred_element_type=F32)
            scr[:, HALO:HALO + tc, :] = u.reshape(nb, tc, cw)
            scr[:, HALO - K1:HALO, :] = bufn_ref[:, :, cols]
            conv.append(_causal_conv(scr, cw_ref, cb_ref, tc, cols).reshape(R, cw))
            bufn_ref[:, :, cols] = scr[:, HALO + tc - K1:HALO + tc, :]
        act = jax.nn.gelu(conv[0]) * conv[1]
        acc = acc + jnp.dot(act.astype(BF16), wdn_ref[c * cw:(c + 1) * cw, :], preferred_element_type=F32)
    y = _layer_norm(ALPHA * x + acc, g_ref[...], b_ref[...])
    y_ref[...] = y.reshape(nb, tc, D_MODEL)


def _ffn(x3, wup, cw, cb, wdn, buf0, buf_base, g, b, nb, tc, cwid):
    B, T, _ = x3.shape
    U, K1 = 2 * D_FF, FFN_CONV - 1
    bb0 = buf_base // nb
    tok = pl.BlockSpec((nb, tc, D_MODEL), lambda bi, t: (bi, t, 0))
    full = lambda a: pl.BlockSpec(a.shape, lambda bi, t: (0,) * a.ndim)
    return pl.pallas_call(
        functools.partial(_ffn_body, nb=nb, tc=tc, cw=cwid),
        grid=(B // nb, T // tc),
        in_specs=[tok, full(wup), full(cw), full(cb), full(wdn),
                  pl.BlockSpec((nb, K1, U), lambda bi, t: (bb0 + bi, 0, 0)), full(g), full(b)],
        out_specs=[tok, pl.BlockSpec((nb, K1, U), lambda bi, t: (bi, 0, 0))],
        out_shape=[jax.ShapeDtypeStruct(x3.shape, F32), jax.ShapeDtypeStruct((B, K1, U), F32)],
        scratch_shapes=[pltpu.VMEM((nb, HALO + tc, cwid), F32), pltpu.VMEM((nb, HALO + tc, cwid), F32)],
        compiler_params=_cparams("parallel", "arbitrary"),
        name="ffn",
    )(x3, wup, cw, cb, wdn, buf0, g, b)


def _ssd_body(xbc_ref, z_ref, sm_ref, conv0_ref, h0_ref, cw_ref, cb_ref, dtb_ref, alog_ref, dsk_ref, nw_ref,
              y_ref, hn_ref, convn_ref, xp, ysc, *, L):
    K1 = SHORT_CONV - 1
    P, N = SSD_HEAD_DIM, SSD_STATE

    @pl.when(pl.program_id(1) == 0)
    def _():
        xp[:, HALO - K1:HALO, :] = conv0_ref[...]
        hn_ref[...] = h0_ref[...]

    xp[:, HALO:HALO + L, :] = xbc_ref[...]
    xc = _causal_conv(xp, cw_ref, cb_ref, L)
    tail = xp[:, HALO + L - K1:HALO + L, :]
    convn_ref[...] = tail
    xp[:, HALO - K1:HALO, :] = tail
    xa = jax.nn.silu(xc[0])
    xs = xa[:, :SSD_INNER]
    bm = xa[:, SSD_INNER:SSD_INNER + SSD_GROUPS * N]
    cm = xa[:, SSD_INNER + SSD_GROUPS * N:]
    head_lane = _iota((1, LANES), 1) < SSD_HEADS
    dt = jnp.where(head_lane, jax.nn.softplus(sm_ref[0] + dtb_ref[...]), 0.0)
    la = dt * -jnp.exp(alog_ref[...])
    lower = _iota((L, L), 0) >= _iota((L, L), 1)
    acum = jnp.dot(lower.astype(F32), la, precision=HIGHEST, preferred_element_type=F32)
    acum_t = acum.T
    dt_t = dt.T
    xs_t = xs.T
    a_end = acum[L - 1:L, :]
    J = SSD_HEADS // SSD_GROUPS
    for g in range(SSD_GROUPS):
        Bg = bm[:, g * N:(g + 1) * N].astype(BF16)
        Cg = cm[:, g * N:(g + 1) * N].astype(BF16)
        CB = _bdot_nt(Cg, Bg)
        for j in range(J):
            h = g * J + j
            hs = slice(h * P, (h + 1) * P)
            col = acum[:, h:h + 1]
            row = acum_t[h:h + 1, :]
            decay = jnp.exp(jnp.where(lower, col - row, -jnp.inf))
            xh = xs[:, hs]
            y_diag = _bdot(CB * decay, xh * dt[:, h:h + 1])
            h_prev = hn_ref[0, hs, :]
            y_off = _bdot_nt(Cg, h_prev) * jnp.exp(col)
            e_end = a_end[:, h:h + 1]
            xw_t = xs_t[hs, :] * (dt_t[h:h + 1, :] * jnp.exp(e_end - row))
            hn_ref[0, hs, :] = jnp.exp(e_end) * h_prev + _bdot(xw_t, Bg)
            ysc[:, hs] = y_diag + y_off + dsk_ref[:, h:h + 1] * xh
    y = ysc[...] * jax.nn.silu(z_ref[0])
    y_ref[0] = y * lax.rsqrt(jnp.mean(y * y, -1, keepdims=True) + LN_EPS) * nw_ref[...]


def _ssd(xbc3, z3, sm3, conv0, conv_base, h0, h_base, cw, cb, dtb, alog, dsk, nw, L):
    B, T, _ = xbc3.shape
    K1 = SHORT_CONV - 1
    tokspec = lambda a: pl.BlockSpec((1, L, a.shape[2]), lambda b, t: (b, t, 0))
    full = lambda a: pl.BlockSpec(a.shape, lambda b, t: (0,) * a.ndim)
    return pl.pallas_call(
        functools.partial(_ssd_body, L=L),
        grid=(B, T // L),
        in_specs=[tokspec(xbc3), tokspec(z3), tokspec(sm3),
                  pl.BlockSpec((1, K1, SSD_CONV_DIM), lambda b, t: (conv_base + b, 0, 0)),
                  pl.BlockSpec((1, SSD_INNER, SSD_STATE), lambda b, t: (h_base + b, 0, 0)),
                  full(cw), full(cb), full(dtb), full(alog), full(dsk), full(nw)],
        out_specs=[tokspec(z3), pl.BlockSpec((1, SSD_INNER, SSD_STATE), lambda b, t: (b, 0, 0)),
                   pl.BlockSpec((1, K1, SSD_CONV_DIM), lambda b, t: (b, 0, 0))],
        out_shape=[jax.ShapeDtypeStruct(z3.shape, F32), jax.ShapeDtypeStruct((B, SSD_INNER, SSD_STATE), F32),
                   jax.ShapeDtypeStruct((B, K1, SSD_CONV_DIM), F32)],
        scratch_shapes=[pltpu.VMEM((1, HALO + L, SSD_CONV_DIM), F32), pltpu.VMEM((L, SSD_INNER), F32)],
        compiler_params=_cparams("parallel", "arbitrary"),
        name="ssd",
    )(xbc3, z3, sm3, conv0, h0, cw, cb, dtb, alog, dsk, nw)


def _gla_body(q_ref, k_ref, v_ref, gg_ref, sm_ref, s0_ref, wa_ref, ba_ref, nw_ref, o_ref, sn_ref, st, *, tc, l):
    K, V = GLA_DK, GLA_DV

    @pl.when(pl.program_id(1) == 0)
    def _():
        for h in range(GLA_HEADS):
            st[h * V:(h + 1) * V, :] = s0_ref[0, h * K:(h + 1) * K, :].T

    log_alpha = jax.nn.log_sigmoid(_bdot(sm_ref[0], wa_ref[...]) + ba_ref[...]) / GLA_TAU
    ri, ci = _iota((tc, tc), 0), _iota((tc, tc), 1)
    shift = int(np.log2(l))
    same = jnp.right_shift(ri, shift) == jnp.right_shift(ci, shift)
    lower = (ri >= ci) & same
    bc = jnp.dot(lower.astype(F32), log_alpha, precision=HIGHEST, preferred_element_type=F32)
    q = q_ref[0] * GLA_DK ** -0.5
    k = k_ref[0]
    v = v_ref[0]
    qe = q * jnp.exp(bc)
    ke = k * jnp.exp(-bc)
    for h in range(GLA_HEADS):
        cs = slice(h * K, (h + 1) * K)
        att = jnp.where(lower, _bdot_nt(qe[:, cs], ke[:, cs]), 0.0)
        o_h = _bdot(att, v[:, cs])
        s_t = st[h * V:(h + 1) * V, :]
        inter = []
        for c in range(tc // l):
            rs = slice(c * l, (c + 1) * l)
            inter.append(_bdot_nt(qe[rs, cs], s_t))
            b_end = bc[(c + 1) * l - 1:(c + 1) * l, cs]
            kd = k[rs, cs] * jnp.exp(b_end - bc[rs, cs])
            s_t = s_t * jnp.exp(b_end) + _bdot_tn(v[rs, cs], kd)
        st[h * V:(h + 1) * V, :] = s_t
        sn_ref[0, h * K:(h + 1) * K, :] = s_t.T
        o_h = o_h + (inter[0] if len(inter) == 1 else jnp.concatenate(inter, axis=0))
        o_h = o_h * lax.rsqrt(jnp.mean(o_h * o_h, -1, keepdims=True) + LN_EPS) * nw_ref[...]
        o_ref[0, :, cs] = o_h * jax.nn.silu(gg_ref[0, :, cs])


def _gla(gq3, gk3, gv3, gg3, sm3, s0, s_base, wa, ba, nw, tc):
    B, T, _ = gq3.shape
    l = min(GLA_CHUNK, T)
    tokspec = lambda a: pl.BlockSpec((1, tc, a.shape[2]), lambda b, t: (b, t, 0))
    full = lambda a: pl.BlockSpec(a.shape, lambda b, t: (0,) * a.ndim)
    st_spec = pl.BlockSpec((1, GLA_W, GLA_DV), lambda b, t: (b, 0, 0))
    return pl.pallas_call(
        functools.partial(_gla_body, tc=tc, l=l),
        grid=(B, T // tc),
        in_specs=[tokspec(gq3), tokspec(gk3), tokspec(gv3), tokspec(gg3), tokspec(sm3),
                  pl.BlockSpec((1, GLA_W, GLA_DV), lambda b, t: (s_base + b, 0, 0)),
                  full(wa), full(ba), full(nw)],
        out_specs=[tokspec(gq3), st_spec],
        out_shape=[jax.ShapeDtypeStruct(gq3.shape, F32), jax.ShapeDtypeStruct((B, GLA_W, GLA_DV), F32)],
        scratch_shapes=[pltpu.VMEM((GLA_HEADS * GLA_DV, GLA_DK), F32)],
        compiler_params=_cparams("parallel", "arbitrary"),
        name="gla",
    )(gq3, gk3, gv3, gg3, sm3, s0, wa, ba, nw)


EVEN_SEGS = (
    (0, EV_Q, (True,) * 4),
    (EV_Q, 4 * LANES, (True, False, True, False)),
    (EV_Q + 4 * LANES, 2 * LANES, (True, False)),
    (EV_Q + EV_KV, LRU_WIDTH, None),
    (EV_Q + EV_KV + LRU_WIDTH, LRU_WIDTH, None),
    (EV_Q + EV_KV + 2 * LRU_WIDTH, LANES, None),
)
ODD_SEGS = (
    (0, SSD_INNER, None),
    (SSD_INNER, SSD_CONV_DIM, None),
    (SSD_INNER + SSD_CONV_DIM, GLA_W, None),
    (SSD_INNER + SSD_CONV_DIM + GLA_W, GLA_W, None),
    (SSD_INNER + SSD_CONV_DIM + 2 * GLA_W, GLA_W, None),
    (SSD_INNER + SSD_CONV_DIM + 3 * GLA_W, GLA_W, None),
    (SSD_INNER + SSD_CONV_DIM + 4 * GLA_W, LANES, None),
)


def _even_w_in(w):
    a = EV_Q + EV_KV
    pad = jnp.zeros((D_MODEL, LANES - EV_GATE), w.dtype)
    return jnp.concatenate([w[:, :a], w[:, a + EV_GATE:], w[:, a:a + EV_GATE], pad], axis=1).astype(BF16)


def _odd_w_in(w):
    a = SSD_INNER + SSD_CONV_DIM
    dt = w[:, a:a + SSD_HEADS]
    rest = w[:, a + SSD_HEADS:a + SSD_HEADS + 4 * GLA_W]
    ga = w[:, a + SSD_HEADS + 4 * GLA_W:]
    pad = jnp.zeros((D_MODEL, LANES - SSD_HEADS - GLA_RANK), w.dtype)
    return jnp.concatenate([w[:, :a], rest, dt, ga, pad], axis=1).astype(BF16)


def _lane_pad(v):
    return jnp.pad(v.astype(F32), (0, LANES - v.shape[0]))[None, :]


def _prep_layer(layer, P):
    j = layer // 2
    d = dict(
        ffn_wup=P['ffn_w_up'][layer].astype(BF16), ffn_cw=P['ffn_conv_w'][layer], ffn_cb=P['ffn_conv_b'][layer][None, :],
        ffn_wdn=P['ffn_w_down'][layer].astype(BF16),
        ln_mix_g=P['ln_mix_g'][layer][None, :], ln_mix_b=P['ln_mix_b'][layer][None, :],
        ln_ffn_g=P['ln_ffn_g'][layer][None, :], ln_ffn_b=P['ln_ffn_b'][layer][None, :])
    if layer % 2 == 0:
        w_out = P['w_out_even'][j].astype(BF16)
        wg = P['lru_w_gates'][j]
        eye = jnp.eye(LRU_BLOCKS, dtype=F32)
        wg = jnp.einsum('knde,nm->kndme', wg, eye).reshape(2, LRU_WIDTH, LRU_WIDTH)
        d.update(
            w_in=_even_w_in(P['w_in_even'][j]), w_out1=w_out[:EV_Q], w_out2=w_out[EV_Q:],
            cmp_tab=jnp.repeat(P['nsa_cmp_w'][j].reshape(CMP_BLOCK, 2 * NSA_KV_HEADS), HEAD_DIM, axis=1),
            cmp_page_tab=_page_compress_table(P['nsa_cmp_w'][j]),
            lru_cw=P['lru_conv_w'][j], lru_cb=P['lru_conv_b'][j][None, :],
            lru_wg=jnp.concatenate([wg[0], wg[1]], axis=1).astype(BF16),
            lru_bg=P['lru_b_gates'][j].reshape(1, 2 * LRU_WIDTH), lru_lam=P['lru_lambda'][j][None, :])
    else:
        w_out = P['w_out_odd'][j].astype(BF16)
        wa = jnp.zeros((LANES, GLA_W), F32).at[SSD_HEADS:SSD_HEADS + GLA_RANK].set(P['gla_w_alpha'][j])
        d.update(
            w_in=_odd_w_in(P['w_in_odd'][j]), w_out1=w_out[:SSD_INNER], w_out2=w_out[SSD_INNER:],
            ssd_cw=P['ssd_conv_w'][j], ssd_cb=P['ssd_conv_b'][j][None, :],
            ssd_dtb=_lane_pad(P['ssd_dt_bias'][j]), ssd_alog=_lane_pad(P['ssd_a_log'][j]),
            ssd_d=_lane_pad(P['ssd_d'][j]), ssd_nw=P['ssd_norm_w'][j][None, :],
            gla_wa=wa.astype(BF16), gla_ba=P['gla_b_alpha'][j][None, :], gla_nw=P['gla_norm_w'][j][None, :])
    return d


def _trunk(x3, prm, st, cfg):
    B, T, _ = x3.shape
    M = B * T
    out = dict(kv=[], win=[], lh=[], lc=[], sh=[], sc=[], gs=[], fc=[])
    for layer in range(DEPTH):
        p = prm[layer]
        j = layer // 2
        x2 = x3.reshape(M, D_MODEL)
        if layer % 2 == 0:
            q, rows, win, hx, hgate, hg = _proj(x2, p['w_in'], EVEN_SEGS, cfg['tm'], cfg['rope'], cfg['rope_blocks'])
            to3 = lambda a: a.reshape(B, T, a.shape[1])
            rows3, win3 = to3(rows), to3(win)
            if cfg['sample']:
                o_nsa, win_t = _nsa_sample(to3(q), to3(hg), rows3, win3, st['cache_t'], st['page_table'],
                                           j * st['n_phys'], st['cwin_t'], j * B, p['cmp_page_tab'], cfg['nsa_nb'])
                win_keep = win_t.transpose(0, 2, 1)
            else:
                H3 = _compress_prompt(rows3, p['cmp_tab'])
                o_nsa = _nsa_prompt(to3(q), to3(hg), H3, rows3, win3, cfg['tq'])
                win_keep = win3[:, T - min(WINDOW, T):]
            y_lru, h_last, conv_n = _lru(to3(hx), to3(hgate), st['lru_conv'], j * B, st['lru_h'], j * B,
                                         p['lru_cw'], p['lru_cb'], p['lru_wg'], p['lru_bg'], p['lru_lam'],
                                         cfg['lru_nb'], cfg['lru_tc'])
            a1, a2 = o_nsa.reshape(M, EV_Q), y_lru.reshape(M, LRU_WIDTH)
            out['kv'].append(rows3.reshape(B, T, 4, NSA_KV_HEADS, HEAD_DIM))
            out['win'].append(win_keep.reshape(B, -1, 2, NSA_KV_HEADS, HEAD_DIM))
            out['lh'].append(h_last.reshape(B, LRU_WIDTH))
            out['lc'].append(conv_n)
        else:
            z, xbc, gq, gk, gv, gg, sm = _proj(x2, p['w_in'], ODD_SEGS, cfg['tm'])
            to3 = lambda a: a.reshape(B, T, a.shape[1])
            sm3 = to3(sm)
            y_ssd, h_n, conv_n = _ssd(to3(xbc), to3(z), sm3, st['ssd_conv'], j * B, st['ssd_h'], j * B,
                                      p['ssd_cw'], p['ssd_cb'], p['ssd_dtb'], p['ssd_alog'], p['ssd_d'], p['ssd_nw'],
                                      cfg['ssd_L'])
            o_gla, s_n = _gla(to3(gq), to3(gk), to3(gv), to3(gg), sm3, st['gla_s'], j * B,
                              p['gla_wa'], p['gla_ba'], p['gla_nw'], cfg['gla_tc'])
            a1, a2 = y_ssd.reshape(M, SSD_INNER), o_gla.reshape(M, GLA_W)
            out['sh'].append(h_n.reshape(B, SSD_HEADS, SSD_HEAD_DIM, SSD_STATE))
            out['sc'].append(conv_n)
            out['gs'].append(s_n.reshape(B, GLA_HEADS, GLA_DK, GLA_DV))
        x2 = _outproj(a1, a2, p['w_out1'], p['w_out2'], x2, p['ln_mix_g'], p['ln_mix_b'], cfg['tm'])
        x3, fbuf = _ffn(x2.reshape(B, T, D_MODEL), p['ffn_wup'], p['ffn_cw'], p['ffn_cb'], p['ffn_wdn'],
                        st['ffn_conv'], layer * B, p['ln_ffn_g'], p['ln_ffn_b'],
                        cfg['ffn_nb'], cfg['ffn_tc'], cfg['ffn_cw'])
        out['fc'].append(fbuf)
    return x3, {k: jnp.stack(v) for k, v in out.items()}


def _largest_tile(n, cap):
    t = min(n, cap)
    while n % t:
        t //= 2
    return t


def kernel(x_prompt, x_sample, cache_nsa_kv, cache_nsa_win, state_lru_h, state_lru_conv, state_ssd, state_ssd_conv, state_gla, state_ffn_conv, page_table, w_in_even, w_out_even, nsa_cmp_w, lru_conv_w, lru_conv_b, lru_w_gates, lru_b_gates, lru_lambda, w_in_odd, w_out_odd, ssd_conv_w, ssd_conv_b, ssd_dt_bias, ssd_a_log, ssd_d, ssd_norm_w, gla_w_alpha, gla_b_alpha, gla_norm_w, ffn_w_up, ffn_conv_w, ffn_conv_b, ffn_w_down, ln_mix_g, ln_mix_b, ln_ffn_g, ln_ffn_b):
    P = dict(w_in_even=w_in_even, w_out_even=w_out_even, nsa_cmp_w=nsa_cmp_w, lru_conv_w=lru_conv_w,
             lru_conv_b=lru_conv_b, lru_w_gates=lru_w_gates, lru_b_gates=lru_b_gates, lru_lambda=lru_lambda,
             w_in_odd=w_in_odd, w_out_odd=w_out_odd, ssd_conv_w=ssd_conv_w, ssd_conv_b=ssd_conv_b,
             ssd_dt_bias=ssd_dt_bias, ssd_a_log=ssd_a_log, ssd_d=ssd_d, ssd_norm_w=ssd_norm_w,
             gla_w_alpha=gla_w_alpha, gla_b_alpha=gla_b_alpha, gla_norm_w=gla_norm_w, ffn_w_up=ffn_w_up,
             ffn_conv_w=ffn_conv_w, ffn_conv_b=ffn_conv_b, ffn_w_down=ffn_w_down, ln_mix_g=ln_mix_g,
             ln_mix_b=ln_mix_b, ln_ffn_g=ln_ffn_g, ln_ffn_b=ln_ffn_b)
    prm = [_prep_layer(layer, P) for layer in range(DEPTH)]
    n_even, n_odd = w_in_even.shape[0], w_in_odd.shape[0]
    ffn_buf = (FFN_CONV - 1, 2 * D_FF)

    Bp, Tp, _ = x_prompt.shape
    tm_p = _largest_tile(Tp, 256)
    st_p = dict(
        lru_conv=jnp.zeros((n_even * Bp, SHORT_CONV - 1, LRU_WIDTH), F32), lru_h=jnp.zeros((n_even * Bp, 1, LRU_WIDTH), F32),
        ssd_conv=jnp.zeros((n_odd * Bp, SHORT_CONV - 1, SSD_CONV_DIM), F32),
        ssd_h=jnp.zeros((n_odd * Bp, SSD_INNER, SSD_STATE), F32), gla_s=jnp.zeros((n_odd * Bp, GLA_W, GLA_DV), F32),
        ffn_conv=jnp.zeros((DEPTH * Bp,) + ffn_buf, F32))
    cfg_p = dict(sample=False, tm=tm_p, rope=_rope_tables(jnp.arange(Tp)), rope_blocks=Tp // tm_p,
                 tq=_largest_tile(Tp, 128), lru_nb=1, lru_tc=_largest_tile(Tp, 512),
                 ssd_L=_largest_tile(Tp, 128), gla_tc=_largest_tile(Tp, 256),
                 ffn_nb=1, ffn_tc=_largest_tile(Tp, 256), ffn_cw=D_FF // 2)
    y_p, o_p = _trunk(x_prompt, prm, st_p, cfg_p)

    Bs, Ts, _ = x_sample.shape
    n_phys = cache_nsa_kv.shape[1]
    past_len = page_table.shape[1] * PAGE_SIZE
    tm_s = _largest_tile(Bs * Ts, 256)
    pos_s = past_len + jnp.arange(tm_s) % Ts
    nb_s = _largest_tile(Bs, 32)
    st_s = dict(
        cache_t=cache_nsa_kv.transpose(0, 1, 3, 4, 5, 2).reshape(n_even * n_phys, 4 * LANES, PAGE_SIZE),
        n_phys=n_phys, page_table=page_table,
        cwin_t=cache_nsa_win.transpose(0, 1, 3, 4, 5, 2).reshape(n_even * Bs, 2 * LANES, cache_nsa_win.shape[2]),
        lru_conv=state_lru_conv.reshape(n_even * Bs, SHORT_CONV - 1, LRU_WIDTH),
        lru_h=state_lru_h.reshape(n_even * Bs, 1, LRU_WIDTH),
        ssd_conv=state_ssd_conv.reshape(n_odd * Bs, SHORT_CONV - 1, SSD_CONV_DIM),
        ssd_h=state_ssd.reshape(n_odd * Bs, SSD_INNER, SSD_STATE),
        gla_s=state_gla.reshape(n_odd * Bs, GLA_W, GLA_DV),
        ffn_conv=state_ffn_conv.reshape((DEPTH * Bs,) + ffn_buf))
    cfg_s = dict(sample=True, tm=tm_s, rope=_rope_tables(pos_s), rope_blocks=1, nsa_nb=_largest_tile(Bs, 4),
                 lru_nb=nb_s, lru_tc=Ts, ssd_L=Ts, gla_tc=Ts, ffn_nb=nb_s, ffn_tc=Ts, ffn_cw=D_FF // 2)
    y_s, o_s = _trunk(x_sample, prm, st_s, cfg_s)

    return (y_p, y_s, o_p['kv'], o_s['kv'], o_p['win'], o_s['win'], o_p['lh'], o_s['lh'], o_p['lc'], o_s['lc'],
            o_p['sh'], o_s['sh'], o_p['sc'], o_s['sc'], o_p['gs'], o_s['gs'], o_p['fc'], o_s['fc'])
```

```python
import functools

import jax
import jax.numpy as jnp
import numpy as np
from jax import lax
from jax.experimental import pallas as pl
from jax.experimental.pallas import tpu as pltpu

F32 = jnp.float32
BF16 = jnp.bfloat16
HIGHEST = lax.Precision.HIGHEST

D_MODEL = 1024
DEPTH = 4
PAGE_SIZE = 128
HEAD_DIM = 64
ROPE_DIM = HEAD_DIM // 4
ROPE_THETA = 500000.0
NSA_HEADS = 8
NSA_KV_HEADS = 2
NSA_GROUP = NSA_HEADS // NSA_KV_HEADS
CMP_BLOCK = 32
CMP_STRIDE = 16
SEL_BLOCK = 64
N_SEL = 8
WINDOW = 512
LRU_WIDTH = D_MODEL // 2
LRU_BLOCKS = 8
LRU_BLOCK_DIM = LRU_WIDTH // LRU_BLOCKS
LRU_C = 8.0
SHORT_CONV = 4
SSD_HEADS = 16
SSD_HEAD_DIM = 64
SSD_INNER = SSD_HEADS * SSD_HEAD_DIM
SSD_GROUPS = 2
SSD_STATE = 128
SSD_CONV_DIM = SSD_INNER + 2 * SSD_GROUPS * SSD_STATE
GLA_HEADS = 4
GLA_DK = 128
GLA_DV = 128
GLA_RANK = 16
GLA_TAU = 16.0
GLA_CHUNK = 32
D_FF = 2816
FFN_CONV = 3
ALPHA = (2.0 * DEPTH) ** 0.25
LN_EPS = 1e-5
NEG = -1e30
EV_Q = NSA_HEADS * HEAD_DIM
EV_KV = 6 * NSA_KV_HEADS * HEAD_DIM
EV_GATE = 3 * NSA_HEADS
GLA_W = GLA_HEADS * GLA_DK

LANES = 128
SUBLANES = 8
V7X_VMEM_BYTES = 64 * 1024 * 1024
VMEM_LIMIT = V7X_VMEM_BYTES - 8 * 1024 * 1024
HALO = SUBLANES


def _cparams(*sem):
    return pltpu.CompilerParams(dimension_semantics=sem, vmem_limit_bytes=VMEM_LIMIT)


def _resident(a):
    return pl.BlockSpec(a.shape, lambda *_: (0,) * a.ndim, pipeline_mode=pl.Buffered(1))


def _bdot(a, b):
    return jnp.dot(a.astype(BF16), b.astype(BF16), preferred_element_type=F32)


def _bdot_nt(a, b):
    return lax.dot_general(a.astype(BF16), b.astype(BF16), (((1,), (1,)), ((), ())), preferred_element_type=F32)


def _bdot_tn(a, b):
    return lax.dot_general(a.astype(BF16), b.astype(BF16), (((0,), (0,)), ((), ())), preferred_element_type=F32)


def _iota(shape, axis):
    return lax.broadcasted_iota(jnp.int32, shape, axis)


def _layer_norm(y, g, b):
    mu = jnp.mean(y, -1, keepdims=True)
    d = y - mu
    var = jnp.mean(d * d, -1, keepdims=True)
    return d * lax.rsqrt(var + LN_EPS) * g + b


def _masked_softmax(s, mask):
    s = jnp.where(mask, s, NEG)
    m = jnp.max(s, -1, keepdims=True)
    e = jnp.exp(s - m)
    p = e / jnp.sum(e, -1, keepdims=True)
    return jnp.where(mask, p, 0.0)


def _proj_body(*refs, segs, has_rope):
    if has_rope:
        x_ref, w_ref, c_ref, s1_ref, s2_ref = refs[:5]
        out_refs = refs[5:]
    else:
        x_ref, w_ref = refs[:2]
        out_refs = refs[2:]
    xb = x_ref[...].astype(BF16)
    for o_ref, (start, width, rope) in zip(out_refs, segs):
        acc = jnp.dot(xb, w_ref[:, start:start + width], preferred_element_type=F32)
        if rope is None:
            o_ref[...] = acc
            continue
        for c, flag in enumerate(rope):
            chunk = acc[:, c * LANES:(c + 1) * LANES]
            if flag:
                chunk = (chunk * c_ref[...] + pltpu.roll(chunk, LANES - ROPE_DIM // 2, 1) * s2_ref[...]
                         + pltpu.roll(chunk, ROPE_DIM // 2, 1) * s1_ref[...])
            o_ref[:, c * LANES:(c + 1) * LANES] = chunk


def _proj(x2d, w, segs, tm, tabs=None, tab_blocks=1):
    M, K = x2d.shape
    N = w.shape[1]
    has_rope = tabs is not None
    in_specs = [pl.BlockSpec((tm, K), lambda i: (i, 0)), _resident(w)]
    args = [x2d, w]
    if has_rope:
        in_specs += [pl.BlockSpec((tm, LANES), lambda i: (i % tab_blocks, 0))] * 3
        args += list(tabs)
    return pl.pallas_call(
        functools.partial(_proj_body, segs=segs, has_rope=has_rope),
        grid=(M // tm,),
        in_specs=in_specs,
        out_specs=[pl.BlockSpec((tm, wd), lambda i: (i, 0)) for _, wd, _ in segs],
        out_shape=[jax.ShapeDtypeStruct((M, wd), F32) for _, wd, _ in segs],
        compiler_params=_cparams("parallel"),
        name="proj",
    )(*args)


def _rope_tables(pos):
    half = ROPE_DIM // 2
    inv = jnp.power(ROPE_THETA, -2.0 * jnp.arange(half, dtype=F32) / ROPE_DIM)
    ang = pos.astype(F32)[:, None] * inv[None, :]
    cos, sin = jnp.cos(ang), jnp.sin(ang)
    R = pos.shape[0]
    rest = HEAD_DIM - ROPE_DIM
    c64 = jnp.concatenate([cos, cos, jnp.ones((R, rest), F32)], 1)
    s1 = jnp.concatenate([jnp.zeros((R, half), F32), sin, jnp.zeros((R, rest), F32)], 1)
    s2 = jnp.concatenate([-sin, jnp.zeros((R, half + rest), F32)], 1)
    return tuple(jnp.concatenate([t, t], 1) for t in (c64, s1, s2))


def _compress_rows(x, w_ref):
    x3 = x.reshape(x.shape[0] // CMP_STRIDE, CMP_STRIDE, 2 * LANES)
    return (jnp.sum(x3 * w_ref[0:CMP_STRIDE, :][None], axis=1),
            jnp.sum(x3 * w_ref[CMP_STRIDE:CMP_BLOCK, :][None], axis=1))


def _compress_body(kv_ref, w_ref, o_ref):
    h0, h1 = _compress_rows(kv_ref[0], w_ref)
    o_ref[0, :, 0:2 * LANES] = h0
    o_ref[0, :, 2 * LANES:4 * LANES] = h1


def _compress_prompt(rows3, wtab):
    B, T, _ = rows3.shape
    return pl.pallas_call(
        _compress_body,
        grid=(B,),
        in_specs=[pl.BlockSpec((1, T, 2 * LANES), lambda b: (b, 0, 0)),
                  pl.BlockSpec((CMP_BLOCK, 2 * LANES), lambda b: (0, 0))],
        out_specs=pl.BlockSpec((1, T // CMP_STRIDE, 4 * LANES), lambda b: (b, 0, 0)),
        out_shape=jax.ShapeDtypeStruct((B, T // CMP_STRIDE, 4 * LANES), F32),
        compiler_params=_cparams("parallel"),
        name="compress_prompt",
    )(rows3, wtab)


def _overlap_matrix(n_rows, n_cmp, n_sel):
    s1 = np.arange(n_rows)[:, None] * CMP_STRIDE
    s2 = np.arange(LANES)[None, :] * SEL_BLOCK
    ov = np.clip(np.minimum(s1 + CMP_BLOCK, s2 + SEL_BLOCK) - np.maximum(s1, s2), 0, None) / CMP_BLOCK
    ov = ov * (np.arange(n_rows)[:, None] < n_cmp) * (np.arange(LANES)[None, :] < n_sel)
    return jnp.asarray(ov, dtype=F32)


def _select_blocks(psum, ov, t_col, n_sel):
    imp = jnp.dot(psum, ov, precision=HIGHEST, preferred_element_type=F32)
    blk = _iota(imp.shape, 1)
    cur = jnp.right_shift(t_col, int(np.log2(SEL_BLOCK)))
    future = blk * SEL_BLOCK > t_col
    forced = (blk == 0) | (blk == cur) | (blk == cur - 1)
    w = jnp.where(future, -1.0, jnp.where(forced, 1e6, imp))
    w = jnp.where(blk < n_sel, w, -jnp.inf)
    n_pad = -(-n_sel // SUBLANES) * SUBLANES
    wt = w.T[0:n_pad, :]
    sub = _iota((n_pad, 1), 0)
    rank = jnp.zeros(wt.shape, F32)
    for i in range(n_sel):
        wi = wt[i:i + 1, :]
        rank = rank + jnp.where((wi > wt) | ((wi == wt) & (sub > i)), 1.0, 0.0)
    sel_t = jnp.where((rank < min(N_SEL, n_sel)) & (sub < n_sel), 1.0, 0.0)
    if n_pad < LANES:
        sel_t = jnp.concatenate([sel_t, jnp.zeros((LANES - n_pad, sel_t.shape[1]), F32)], axis=0)
    return sel_t.T


def _stack_heads(q, h):
    G = NSA_GROUP
    qh = jnp.concatenate([q[:, (h * G + g) * HEAD_DIM:(h * G + g + 1) * HEAD_DIM] for g in range(G)], axis=0)
    return (qh * HEAD_DIM ** -0.5).astype(BF16)


def _expand_matrix(K):
    blk = jnp.right_shift(_iota((LANES, K), 1), int(np.log2(SEL_BLOCK)))
    return (blk == _iota((LANES, K), 0)).astype(BF16)


def _attend(qb, chunks, mask, tq):
    G = NSA_GROUP
    dh = HEAD_DIM
    bias = jnp.where(mask, 0.0, NEG)
    ks = [k.astype(BF16) for _, k, _ in chunks]
    vs = [jnp.concatenate([v.astype(BF16), jnp.ones(v.shape, BF16)], axis=0 if tr else 1) for tr, _, v in chunks]
    reps = G if tq >= LANES else 1
    rows = G * tq // reps
    outs = []
    for r in range(reps):
        q_r = qb[r * rows:(r + 1) * rows]
        s = [jnp.dot(q_r, k, preferred_element_type=F32) if tr else
             lax.dot_general(q_r, k, (((1,), (1,)), ((), ())), preferred_element_type=F32)
             for (tr, _, _), k in zip(chunks, ks)]
        s = s[0] if len(s) == 1 else jnp.concatenate(s, axis=-1)
        K = s.shape[-1]
        s = (s.reshape(rows // tq, tq, K) + bias[None]).reshape(rows, K)
        e = jnp.exp(s - jnp.max(s, -1, keepdims=True)).astype(BF16)
        ox, off = None, 0
        for (tr, _, _), v in zip(chunks, vs):
            kc = v.shape[1] if tr else v.shape[0]
            part = (lax.dot_general(e[:, off:off + kc], v, (((1,), (1,)), ((), ())), preferred_element_type=F32)
                    if tr else jnp.dot(e[:, off:off + kc], v, preferred_element_type=F32))
            ox = part if ox is None else ox + part
            off += kc
        outs.append(ox[:, :dh] / ox[:, dh:dh + 1])
    return outs[0] if reps == 1 else jnp.concatenate(outs, axis=0)


def _gate_store(o_ref, sg, h, tq, o_c, o_s, o_w):
    for g in range(NSA_GROUP):
        hh = h * NSA_GROUP + g
        r = slice(g * tq, (g + 1) * tq)
        o_ref[:, hh * HEAD_DIM:(hh + 1) * HEAD_DIM] = (
            sg[:, 3 * hh:3 * hh + 1] * o_c[r] + sg[:, 3 * hh + 1:3 * hh + 2] * o_s[r]
            + sg[:, 3 * hh + 2:3 * hh + 3] * o_w[r])


def _nsa_core(items, ov, t_col, n_sel):
    tq = t_col.shape[0]
    G = NSA_GROUP
    o_cs, psums = [], []
    for it in items:
        n_rows = it['ckk'].shape[0]
        p_c = _masked_softmax(_bdot_nt(it['qb'], it['ckk']).reshape(G, tq, n_rows), it['mask_c'][None])
        o_cs.append(_bdot(p_c.reshape(G * tq, n_rows), it['ckv']))
        psum = p_c[0]
        for g in range(1, G):
            psum = psum + p_c[g]
        psums.append(psum)
    o_ws = [it['win'](it['qb']) for it in items]
    n = len(items)
    sel = _select_blocks(jnp.concatenate(psums, axis=0), ov, jnp.concatenate([t_col] * n, axis=0), n_sel)
    for i, it in enumerate(items):
        o_s = it['slc'](it['qb'], sel[i * tq:(i + 1) * tq])
        it['finish'](o_cs[i], o_s, o_ws[i])


def _combine_compressed(h0, h1):
    return h0 + pltpu.roll(h1, h1.shape[0] - 1, 0)


def _nsa_prompt_body(q_ref, hg_ref, H_ref, rows_ref, win_ref, ov_ref, o_ref, *, T, tq):
    qi = pl.program_id(1)
    n_rows = T // CMP_STRIDE
    n_cmp = n_rows - CMP_BLOCK // CMP_STRIDE + 1
    n_sel = T // SEL_BLOCK
    t_col = qi * tq + _iota((tq, 1), 0)
    ck = _combine_compressed(H_ref[0, :, 0:2 * LANES], H_ref[0, :, 2 * LANES:4 * LANES])
    ncol = _iota((1, n_rows), 1)
    mask_c = (ncol * CMP_STRIDE + CMP_BLOCK - 1 <= t_col) & (ncol < n_cmp)
    band = WINDOW + tq
    wstart = pl.multiple_of(jnp.clip(qi * tq - WINDOW, 0, T - band), tq)
    wpos = wstart + _iota((1, band), 1)
    mask_w = (wpos <= t_col) & (wpos > t_col - WINDOW)
    q = q_ref[0]
    sg = jax.nn.sigmoid(hg_ref[0])
    kstep = min(T, 4 * LANES)

    def make_item(h):
        def slc_span(K):
            def run(qb, sel):
                allowed = jnp.dot(sel.astype(BF16), _expand_matrix(K), preferred_element_type=F32) > 0.5
                mask = allowed & (_iota((1, K), 1) <= t_col)
                k = rows_ref[0, 0:K, 2 * LANES + h * HEAD_DIM:2 * LANES + (h + 1) * HEAD_DIM]
                v = rows_ref[0, 0:K, 3 * LANES + h * HEAD_DIM:3 * LANES + (h + 1) * HEAD_DIM]
                return _attend(qb, [(False, k, v)], mask, tq)
            return run

        def slc(qb, sel):
            spans = [slc_span((i + 1) * kstep) for i in range(T // kstep)]
            if len(spans) == 1:
                return spans[0](qb, sel)
            return lax.switch(lax.div(qi * tq + (tq - 1), kstep), spans, qb, sel)

        def win(qb):
            k = win_ref[0, pl.ds(wstart, band), h * HEAD_DIM:(h + 1) * HEAD_DIM]
            v = win_ref[0, pl.ds(wstart, band), LANES + h * HEAD_DIM:LANES + (h + 1) * HEAD_DIM]
            return _attend(qb, [(False, k, v)], mask_w, tq)

        return dict(qb=_stack_heads(q, h), ckk=ck[:, h * HEAD_DIM:(h + 1) * HEAD_DIM],
                    ckv=ck[:, LANES + h * HEAD_DIM:LANES + (h + 1) * HEAD_DIM], mask_c=mask_c,
                    slc=slc, win=win, finish=functools.partial(_gate_store, o_ref.at[0], sg, h, tq))

    _nsa_core([make_item(h) for h in range(NSA_KV_HEADS)], ov_ref[...], t_col, n_sel)


def _nsa_prompt(q3, hg3, H3, rows3, win3, tq):
    B, T, _ = q3.shape
    n_rows = T // CMP_STRIDE
    ov = _overlap_matrix(n_rows, n_rows - 1, T // SEL_BLOCK)
    return pl.pallas_call(
        functools.partial(_nsa_prompt_body, T=T, tq=tq),
        grid=(B, T // tq),
        in_specs=[pl.BlockSpec((1, tq, EV_Q), lambda b, i: (b, i, 0)),
                  pl.BlockSpec((1, tq, LANES), lambda b, i: (b, i, 0)),
                  pl.BlockSpec((1, n_rows, 4 * LANES), lambda b, i: (b, 0, 0)),
                  pl.BlockSpec((1, T, 4 * LANES), lambda b, i: (b, 0, 0)),
                  pl.BlockSpec((1, T, 2 * LANES), lambda b, i: (b, 0, 0)),
                  pl.BlockSpec((n_rows, LANES), lambda b, i: (0, 0))],
        out_specs=pl.BlockSpec((1, tq, EV_Q), lambda b, i: (b, i, 0)),
        out_shape=jax.ShapeDtypeStruct((B, T, EV_Q), F32),
        compiler_params=_cparams("parallel", "arbitrary"),
        name="nsa_prompt",
    )(q3, hg3, H3, rows3, win3, ov)


def _nsa_sample_body(*refs, nb, n_pages, Tq, Wb):
    q_ref, hg_ref, rnew_ref, wnew_ref, ctab_ref, ov_ref = refs[1:7]
    page_refs = refs[7:7 + nb * n_pages]
    cwin_ref, o_ref, wout_ref, knew, wnew = refs[7 + nb * n_pages:]
    P = n_pages * PAGE_SIZE
    Ks = P + PAGE_SIZE
    Kw = Wb + PAGE_SIZE
    n_rows = P // CMP_STRIDE
    n_cmp = n_rows - CMP_BLOCK // CMP_STRIDE + 1
    n_sel = -(-(P + Tq) // SEL_BLOCK)
    tt = _iota((Tq, 1), 0)
    t_col = P + tt
    mask_c = (_iota((1, n_rows), 1) < n_cmp) & (tt >= 0)
    causal = _iota((1, Ks), 1) <= t_col
    wpos = _iota((1, Kw), 1)
    mask_w = (wpos <= Wb + tt) & (wpos > Wb + tt - WINDOW)
    expand = _expand_matrix(Ks)
    pad_rows = jnp.zeros((PAGE_SIZE - Tq, 2 * LANES), F32)

    n_grp = 2 * NSA_KV_HEADS
    per_page = PAGE_SIZE // CMP_STRIDE

    def make_item(i, h, cks, sg, pages):
        ksl = slice(h * HEAD_DIM, (h + 1) * HEAD_DIM)
        vsl = slice(LANES + h * HEAD_DIM, LANES + (h + 1) * HEAD_DIM)

        def slc(qb, sel):
            mask = (jnp.dot(sel.astype(BF16), expand, preferred_element_type=F32) > 0.5) & causal
            chunks = [(True, pr[0, 2 * LANES + ksl.start:2 * LANES + ksl.stop, :],
                       pr[0, 2 * LANES + vsl.start:2 * LANES + vsl.stop, :]) for pr in pages]
            chunks.append((False, knew[i, :, ksl], knew[i, :, vsl]))
            return _attend(qb, chunks, mask, Tq)

        def win(qb):
            chunks = [(True, cwin_ref[i, ksl, :], cwin_ref[i, vsl, :]), (False, wnew[i, :, ksl], wnew[i, :, vsl])]
            return _attend(qb, chunks, mask_w, Tq)

        return dict(qb=_stack_heads(q_ref[i], h), ckk=cks[h], ckv=cks[NSA_KV_HEADS + h], mask_c=mask_c,
                    slc=slc, win=win, finish=functools.partial(_gate_store, o_ref.at[i], sg, h, Tq))

    items = []
    new_lanes = _iota((1, LANES), 1) >= LANES - Tq
    for i in range(nb):
        knew[i, 0:Tq, :] = rnew_ref[i, :, 2 * LANES:4 * LANES]
        knew[i, Tq:PAGE_SIZE, :] = pad_rows
        wnew[i, 0:Tq, :] = wnew_ref[i]
        wnew[i, Tq:PAGE_SIZE, :] = pad_rows
        shifted = pltpu.roll(cwin_ref[i], Wb - Tq, 1)
        new_t = pltpu.roll(wnew[i].T, LANES - Tq, 1)
        wout_ref[i, :, 0:Wb - LANES] = shifted[:, 0:Wb - LANES]
        wout_ref[i, :, Wb - LANES:Wb] = jnp.where(new_lanes, new_t, shifted[:, Wb - LANES:Wb])
        pages = page_refs[i * n_pages:(i + 1) * n_pages]
        parts = [_bdot_nt(ctab_ref[...], pr[0, 0:2 * LANES, :]) for pr in pages]
        cks = []
        for g in range(n_grp):
            r0, cs = g * 2 * per_page, slice(g * HEAD_DIM, (g + 1) * HEAD_DIM)
            cks.append(_combine_compressed(
                jnp.concatenate([pp[r0:r0 + per_page, cs] for pp in parts], axis=0),
                jnp.concatenate([pp[r0 + per_page:r0 + 2 * per_page, cs] for pp in parts], axis=0)))
        sg = jax.nn.sigmoid(hg_ref[i])
        items += [make_item(i, h, cks, sg, pages) for h in range(NSA_KV_HEADS)]
    _nsa_core(items, ov_ref[...], t_col, n_sel)


def _page_compress_table(cmp_w):
    per_page = PAGE_SIZE // CMP_STRIDE
    w = cmp_w.reshape(CMP_BLOCK, 2 * NSA_KV_HEADS).T
    s = jnp.arange(PAGE_SIZE)[None, :] - CMP_STRIDE * jnp.arange(per_page)[:, None]
    inside = (s >= 0) & (s < CMP_STRIDE)
    sc = jnp.clip(s, 0, CMP_STRIDE - 1)
    first = jnp.where(inside[None], w[:, sc], 0.0)
    second = jnp.where(inside[None], w[:, CMP_STRIDE + sc], 0.0)
    return jnp.concatenate([first, second], axis=1).reshape(-1, PAGE_SIZE).astype(BF16)


def _nsa_sample(q3, hg3, rows3, win3, cache_t, page_table, page_base, cwin_t, cwin_base, ctab, nb):
    B, Tq, _ = q3.shape
    n_pages = page_table.shape[1]
    P = n_pages * PAGE_SIZE
    Wb = cwin_t.shape[2]
    assert Wb % LANES == 0 and Wb > LANES and Tq <= SUBLANES and B % nb == 0
    n_rows = P // CMP_STRIDE
    n_sel = -(-(P + Tq) // SEL_BLOCK)
    ov = _overlap_matrix(n_rows, n_rows - 1, n_sel)
    cb0 = cwin_base // nb

    def page_spec(i, p):
        return pl.BlockSpec((1, 4 * LANES, PAGE_SIZE), lambda b, pt: (page_base + pt[b * nb + i, p], 0, 0))

    tok = lambda a: pl.BlockSpec((nb, Tq, a.shape[2]), lambda b, pt: (b, 0, 0))
    grid_spec = pltpu.PrefetchScalarGridSpec(
        num_scalar_prefetch=1,
        grid=(B // nb,),
        in_specs=[tok(q3), tok(hg3), tok(rows3), tok(win3),
                  pl.BlockSpec(ctab.shape, lambda b, pt: (0, 0)), pl.BlockSpec(ov.shape, lambda b, pt: (0, 0))]
        + [page_spec(i, p) for i in range(nb) for p in range(n_pages)]
        + [pl.BlockSpec((nb, 2 * LANES, Wb), lambda b, pt: (cb0 + b, 0, 0))],
        out_specs=[tok(q3), pl.BlockSpec((nb, 2 * LANES, Wb), lambda b, pt: (b, 0, 0))],
        scratch_shapes=[pltpu.VMEM((nb, PAGE_SIZE, 2 * LANES), F32), pltpu.VMEM((nb, PAGE_SIZE, 2 * LANES), F32)],
    )
    return pl.pallas_call(
        functools.partial(_nsa_sample_body, nb=nb, n_pages=n_pages, Tq=Tq, Wb=Wb),
        grid_spec=grid_spec,
        out_shape=[jax.ShapeDtypeStruct((B, Tq, EV_Q), F32), jax.ShapeDtypeStruct((B, 2 * LANES, Wb), F32)],
        compiler_params=_cparams("arbitrary"),
        name="nsa_sample",
    )(page_table, q3, hg3, rows3, win3, ctab, ov, *([cache_t] * (nb * n_pages)), cwin_t)


def _causal_conv(xp, w_ref, b_ref, tc, cols=None):
    K = w_ref.shape[0]
    cs = slice(None) if cols is None else cols
    acc = None
    for j in range(K):
        term = w_ref[j:j + 1, cs][None] * xp[:, HALO - (K - 1) + j:HALO - (K - 1) + j + tc, :]
        acc = term if acc is None else acc + term
    return b_ref[:, cs][None] + acc


def _lru_body(hx_ref, hgate_ref, conv0_ref, h0_ref, cw_ref, cb_ref, wg_ref, bg_ref, lam_ref,
              y_ref, hlast_ref, convn_ref, xp, *, nb, tc):
    K1 = SHORT_CONV - 1

    @pl.when(pl.program_id(1) == 0)
    def _():
        xp[:, HALO - K1:HALO, :] = conv0_ref[...]
        hlast_ref[...] = h0_ref[...]

    xp[:, HALO:HALO + tc, :] = hx_ref[...]
    xc = _causal_conv(xp, cw_ref, cb_ref, tc)
    tail = xp[:, HALO + tc - K1:HALO + tc, :]
    convn_ref[...] = tail
    xp[:, HALO - K1:HALO, :] = tail
    R = nb * tc
    xc2 = xc.reshape(R, LRU_WIDTH)
    gt = _bdot(xc2, wg_ref[...]) + bg_ref[...]
    r_gate = jax.nn.sigmoid(gt[:, :LRU_WIDTH])
    i_gate = jax.nn.sigmoid(gt[:, LRU_WIDTH:])
    log_a = -LRU_C * r_gate * jax.nn.softplus(-lam_ref[...])
    a = jnp.exp(log_a)
    th = jnp.tanh(log_a)
    u = jnp.sqrt(-2.0 * th / (1.0 - th)) * i_gate * xc2
    tpos = lax.rem(_iota((R, 1), 0), tc)
    d = 1
    while d < tc:
        valid = tpos >= d
        u = jnp.where(valid, a * pltpu.roll(u, d, 0) + u, u)
        a = jnp.where(valid, a * pltpu.roll(a, d, 0), a)
        d *= 2
    hprev = jnp.broadcast_to(hlast_ref[...], (nb, tc, LRU_WIDTH)).reshape(R, LRU_WIDTH)
    h = a * hprev + u
    y_ref[...] = (h * jax.nn.gelu(hgate_ref[...].reshape(R, LRU_WIDTH))).reshape(nb, tc, LRU_WIDTH)
    last = _iota((1, tc, 1), 1) == tc - 1
    hlast_ref[...] = jnp.sum(jnp.where(last, h.reshape(nb, tc, LRU_WIDTH), 0.0), axis=1, keepdims=True)


def _lru(hx3, hgate3, conv0, conv_base, h0, h_base, cw, cb, wg, bg, lam, nb, tc):
    B, T, C = hx3.shape
    K1 = SHORT_CONV - 1
    cb0 = conv_base // nb
    hb0 = h_base // nb
    tok = pl.BlockSpec((nb, tc, C), lambda b, t: (b, t, 0))
    full = lambda a: pl.BlockSpec(a.shape, lambda b, t: (0,) * a.ndim)
    return pl.pallas_call(
        functools.partial(_lru_body, nb=nb, tc=tc),
        grid=(B // nb, T // tc),
        in_specs=[tok, tok,
                  pl.BlockSpec((nb, K1, C), lambda b, t: (cb0 + b, 0, 0)),
                  pl.BlockSpec((nb, 1, C), lambda b, t: (hb0 + b, 0, 0)),
                  full(cw), full(cb), full(wg), full(bg), full(lam)],
        out_specs=[tok, pl.BlockSpec((nb, 1, C), lambda b, t: (b, 0, 0)),
                   pl.BlockSpec((nb, K1, C), lambda b, t: (b, 0, 0))],
        out_shape=[jax.ShapeDtypeStruct((B, T, C), F32), jax.ShapeDtypeStruct((B, 1, C), F32),
                   jax.ShapeDtypeStruct((B, K1, C), F32)],
        scratch_shapes=[pltpu.VMEM((nb, HALO + tc, C), F32)],
        compiler_params=_cparams("parallel", "arbitrary"),
        name="lru",
    )(hx3, hgate3, conv0, h0, cw, cb, wg, bg, lam)


def _outproj_body(a1_ref, a2_ref, w1_ref, w2_ref, x_ref, g_ref, b_ref, o_ref):
    mix = _bdot(a1_ref[...], w1_ref[...]) + _bdot(a2_ref[...], w2_ref[...])
    o_ref[...] = _layer_norm(ALPHA * x_ref[...] + mix, g_ref[...], b_ref[...])


def _outproj(a1, a2, w1, w2, x2d, g, b, tm):
    M = x2d.shape[0]
    row = lambda a: pl.BlockSpec((tm, a.shape[1]), lambda i: (i, 0))
    full = _resident
    return pl.pallas_call(
        _outproj_body,
        grid=(M // tm,),
        in_specs=[row(a1), row(a2), full(w1), full(w2), row(x2d), full(g), full(b)],
        out_specs=row(x2d),
        out_shape=jax.ShapeDtypeStruct(x2d.shape, F32),
        compiler_params=_cparams("parallel"),
        name="outproj_ln",
    )(a1, a2, w1, w2, x2d, g, b)


def _ffn_body(x_ref, wup_ref, cw_ref, cb_ref, wdn_ref, buf0_ref, g_ref, b_ref, y_ref, bufn_ref, sg, sv,
              *, nb, tc, cw):
    K1 = FFN_CONV - 1

    @pl.when(pl.program_id(1) == 0)
    def _():
        bufn_ref[...] = buf0_ref[...]

    R = nb * tc
    x = x_ref[...].reshape(R, D_MODEL)
    xb = x.astype(BF16)
    acc = jnp.zeros((R, D_MODEL), F32)
    for c in range(D_FF // cw):
        conv = []
        for half, scr in ((0, sg), (1, sv)):
            cols = slice(half * D_FF + c * cw, half * D_FF + (c + 1) * cw)
            u = jnp.dot(xb, wup_ref[:, cols], preferred_element_type=F32)
            scr[:, HALO:HALO + tc, :] = u.reshape(nb, tc, cw)
            scr[:, HALO - K1:HALO, :] = bufn_ref[:, :, cols]
            conv.append(_causal_conv(scr, cw_ref, cb_ref, tc, cols).reshape(R, cw))
            bufn_ref[:, :, cols] = scr[:, HALO + tc - K1:HALO + tc, :]
        act = jax.nn.gelu(conv[0]) * conv[1]
        acc = acc + jnp.dot(act.astype(BF16), wdn_ref[c * cw:(c + 1) * cw, :], preferred_element_type=F32)
    y = _layer_norm(ALPHA * x + acc, g_ref[...], b_ref[...])
    y_ref[...] = y.reshape(nb, tc, D_MODEL)


def _ffn(x3, wup, cw, cb, wdn, buf0, buf_base, g, b, nb, tc, cwid):
    B, T, _ = x3.shape
    U, K1 = 2 * D_FF, FFN_CONV - 1
    bb0 = buf_base // nb
    tok = pl.BlockSpec((nb, tc, D_MODEL), lambda bi, t: (bi, t, 0))
    full = _resident
    return pl.pallas_call(
        functools.partial(_ffn_body, nb=nb, tc=tc, cw=cwid),
        grid=(B // nb, T // tc),
        in_specs=[tok, full(wup), full(cw), full(cb), full(wdn),
                  pl.BlockSpec((nb, K1, U), lambda bi, t: (bb0 + bi, 0, 0)), full(g), full(b)],
        out_specs=[tok, pl.BlockSpec((nb, K1, U), lambda bi, t: (bi, 0, 0))],
        out_shape=[jax.ShapeDtypeStruct(x3.shape, F32), jax.ShapeDtypeStruct((B, K1, U), F32)],
        scratch_shapes=[pltpu.VMEM((nb, HALO + tc, cwid), F32), pltpu.VMEM((nb, HALO + tc, cwid), F32)],
        compiler_params=_cparams("parallel", "arbitrary"),
        name="ffn",
    )(x3, wup, cw, cb, wdn, buf0, g, b)


def _ssd_body(xbc_ref, z_ref, sm_ref, conv0_ref, h0_ref, cw_ref, cb_ref, dtb_ref, alog_ref, dsk_ref, nw_ref,
              y_ref, hn_ref, convn_ref, xp, ysc, *, L):
    K1 = SHORT_CONV - 1
    P, N = SSD_HEAD_DIM, SSD_STATE

    @pl.when(pl.program_id(1) == 0)
    def _():
        xp[:, HALO - K1:HALO, :] = conv0_ref[...]
        hn_ref[...] = h0_ref[...]

    xp[:, HALO:HALO + L, :] = xbc_ref[...]
    xc = _causal_conv(xp, cw_ref, cb_ref, L)
    tail = xp[:, HALO + L - K1:HALO + L, :]
    convn_ref[...] = tail
    xp[:, HALO - K1:HALO, :] = tail
    xa = jax.nn.silu(xc[0])
    xs = xa[:, :SSD_INNER]
    bm = xa[:, SSD_INNER:SSD_INNER + SSD_GROUPS * N]
    cm = xa[:, SSD_INNER + SSD_GROUPS * N:]
    head_lane = _iota((1, LANES), 1) < SSD_HEADS
    dt = jnp.where(head_lane, jax.nn.softplus(sm_ref[0] + dtb_ref[...]), 0.0)
    la = dt * -jnp.exp(alog_ref[...])
    lower = _iota((L, L), 0) >= _iota((L, L), 1)
    acum = jnp.dot(lower.astype(F32), la, precision=HIGHEST, preferred_element_type=F32)
    acum_t = acum.T
    dt_t = dt.T
    xs_t = xs.T
    a_end = acum[L - 1:L, :]
    J = SSD_HEADS // SSD_GROUPS
    for g in range(SSD_GROUPS):
        Bg = bm[:, g * N:(g + 1) * N].astype(BF16)
        Cg = cm[:, g * N:(g + 1) * N].astype(BF16)
        CB = _bdot_nt(Cg, Bg)
        for j in range(J):
            h = g * J + j
            hs = slice(h * P, (h + 1) * P)
            col = acum[:, h:h + 1]
            row = acum_t[h:h + 1, :]
            decay = jnp.exp(jnp.where(lower, col - row, -jnp.inf))
            xh = xs[:, hs]
            y_diag = _bdot(CB * decay, xh * dt[:, h:h + 1])
            h_prev = hn_ref[0, hs, :]
            y_off = _bdot_nt(Cg, h_prev) * jnp.exp(col)
            e_end = a_end[:, h:h + 1]
            xw_t = xs_t[hs, :] * (dt_t[h:h + 1, :] * jnp.exp(e_end - row))
            hn_ref[0, hs, :] = jnp.exp(e_end) * h_prev + _bdot(xw_t, Bg)
            ysc[:, hs] = y_diag + y_off + dsk_ref[:, h:h + 1] * xh
    y = ysc[...] * jax.nn.silu(z_ref[0])
    y_ref[0] = y * lax.rsqrt(jnp.mean(y * y, -1, keepdims=True) + LN_EPS) * nw_ref[...]


def _ssd(xbc3, z3, sm3, conv0, conv_base, h0, h_base, cw, cb, dtb, alog, dsk, nw, L):
    B, T, _ = xbc3.shape
    K1 = SHORT_CONV - 1
    tokspec = lambda a: pl.BlockSpec((1, L, a.shape[2]), lambda b, t: (b, t, 0))
    full = lambda a: pl.BlockSpec(a.shape, lambda b, t: (0,) * a.ndim)
    return pl.pallas_call(
        functools.partial(_ssd_body, L=L),
        grid=(B, T // L),
        in_specs=[tokspec(xbc3), tokspec(z3), tokspec(sm3),
                  pl.BlockSpec((1, K1, SSD_CONV_DIM), lambda b, t: (conv_base + b, 0, 0)),
                  pl.BlockSpec((1, SSD_INNER, SSD_STATE), lambda b, t: (h_base + b, 0, 0)),
                  full(cw), full(cb), full(dtb), full(alog), full(dsk), full(nw)],
        out_specs=[tokspec(z3), pl.BlockSpec((1, SSD_INNER, SSD_STATE), lambda b, t: (b, 0, 0)),
                   pl.BlockSpec((1, K1, SSD_CONV_DIM), lambda b, t: (b, 0, 0))],
        out_shape=[jax.ShapeDtypeStruct(z3.shape, F32), jax.ShapeDtypeStruct((B, SSD_INNER, SSD_STATE), F32),
                   jax.ShapeDtypeStruct((B, K1, SSD_CONV_DIM), F32)],
        scratch_shapes=[pltpu.VMEM((1, HALO + L, SSD_CONV_DIM), F32), pltpu.VMEM((L, SSD_INNER), F32)],
        compiler_params=_cparams("parallel", "arbitrary"),
        name="ssd",
    )(xbc3, z3, sm3, conv0, h0, cw, cb, dtb, alog, dsk, nw)


def _gla_body(q_ref, k_ref, v_ref, gg_ref, sm_ref, s0_ref, wa_ref, ba_ref, nw_ref, o_ref, sn_ref, st, *, tc, l):
    K, V = GLA_DK, GLA_DV

    @pl.when(pl.program_id(1) == 0)
    def _():
        for h in range(GLA_HEADS):
            st[h * V:(h + 1) * V, :] = s0_ref[0, h * K:(h + 1) * K, :].T

    log_alpha = jax.nn.log_sigmoid(_bdot(sm_ref[0], wa_ref[...]) + ba_ref[...]) / GLA_TAU
    ri, ci = _iota((tc, tc), 0), _iota((tc, tc), 1)
    shift = int(np.log2(l))
    same = jnp.right_shift(ri, shift) == jnp.right_shift(ci, shift)
    lower = (ri >= ci) & same
    bc = jnp.dot(lower.astype(F32), log_alpha, precision=HIGHEST, preferred_element_type=F32)
    q = q_ref[0] * GLA_DK ** -0.5
    k = k_ref[0]
    v = v_ref[0]
    qe = q * jnp.exp(bc)
    ke = k * jnp.exp(-bc)
    for h in range(GLA_HEADS):
        cs = slice(h * K, (h + 1) * K)
        att = jnp.where(lower, _bdot_nt(qe[:, cs], ke[:, cs]), 0.0)
        o_h = _bdot(att, v[:, cs])
        s_t = st[h * V:(h + 1) * V, :]
        inter = []
        for c in range(tc // l):
            rs = slice(c * l, (c + 1) * l)
            inter.append(_bdot_nt(qe[rs, cs], s_t))
            b_end = bc[(c + 1) * l - 1:(c + 1) * l, cs]
            kd = k[rs, cs] * jnp.exp(b_end - bc[rs, cs])
            s_t = s_t * jnp.exp(b_end) + _bdot_tn(v[rs, cs], kd)
        st[h * V:(h + 1) * V, :] = s_t
        sn_ref[0, h * K:(h + 1) * K, :] = s_t.T
        o_h = o_h + (inter[0] if len(inter) == 1 else jnp.concatenate(inter, axis=0))
        o_h = o_h * lax.rsqrt(jnp.mean(o_h * o_h, -1, keepdims=True) + LN_EPS) * nw_ref[...]
        o_ref[0, :, cs] = o_h * jax.nn.silu(gg_ref[0, :, cs])


def _gla(gq3, gk3, gv3, gg3, sm3, s0, s_base, wa, ba, nw, tc):
    B, T, _ = gq3.shape
    l = min(GLA_CHUNK, T)
    tokspec = lambda a: pl.BlockSpec((1, tc, a.shape[2]), lambda b, t: (b, t, 0))
    full = lambda a: pl.BlockSpec(a.shape, lambda b, t: (0,) * a.ndim)
    st_spec = pl.BlockSpec((1, GLA_W, GLA_DV), lambda b, t: (b, 0, 0))
    return pl.pallas_call(
        functools.partial(_gla_body, tc=tc, l=l),
        grid=(B, T // tc),
        in_specs=[tokspec(gq3), tokspec(gk3), tokspec(gv3), tokspec(gg3), tokspec(sm3),
                  pl.BlockSpec((1, GLA_W, GLA_DV), lambda b, t: (s_base + b, 0, 0)),
                  full(wa), full(ba), full(nw)],
        out_specs=[tokspec(gq3), st_spec],
        out_shape=[jax.ShapeDtypeStruct(gq3.shape, F32), jax.ShapeDtypeStruct((B, GLA_W, GLA_DV), F32)],
        scratch_shapes=[pltpu.VMEM((GLA_HEADS * GLA_DV, GLA_DK), F32)],
        compiler_params=_cparams("parallel", "arbitrary"),
        name="gla",
    )(gq3, gk3, gv3, gg3, sm3, s0, wa, ba, nw)


EVEN_SEGS = (
    (0, EV_Q, (True,) * 4),
    (EV_Q, 4 * LANES, (True, False, True, False)),
    (EV_Q + 4 * LANES, 2 * LANES, (True, False)),
    (EV_Q + EV_KV, LRU_WIDTH, None),
    (EV_Q + EV_KV + LRU_WIDTH, LRU_WIDTH, None),
    (EV_Q + EV_KV + 2 * LRU_WIDTH, LANES, None),
)
ODD_SEGS = (
    (0, SSD_INNER, None),
    (SSD_INNER, SSD_CONV_DIM, None),
    (SSD_INNER + SSD_CONV_DIM, GLA_W, None),
    (SSD_INNER + SSD_CONV_DIM + GLA_W, GLA_W, None),
    (SSD_INNER + SSD_CONV_DIM + 2 * GLA_W, GLA_W, None),
    (SSD_INNER + SSD_CONV_DIM + 3 * GLA_W, GLA_W, None),
    (SSD_INNER + SSD_CONV_DIM + 4 * GLA_W, LANES, None),
)


def _even_w_in(w):
    a = EV_Q + EV_KV
    pad = jnp.zeros((D_MODEL, LANES - EV_GATE), w.dtype)
    return jnp.concatenate([w[:, :a], w[:, a + EV_GATE:], w[:, a:a + EV_GATE], pad], axis=1).astype(BF16)


def _odd_w_in(w):
    a = SSD_INNER + SSD_CONV_DIM
    dt = w[:, a:a + SSD_HEADS]
    rest = w[:, a + SSD_HEADS:a + SSD_HEADS + 4 * GLA_W]
    ga = w[:, a + SSD_HEADS + 4 * GLA_W:]
    pad = jnp.zeros((D_MODEL, LANES - SSD_HEADS - GLA_RANK), w.dtype)
    return jnp.concatenate([w[:, :a], rest, dt, ga, pad], axis=1).astype(BF16)


def _lane_pad(v):
    return jnp.pad(v.astype(F32), (0, LANES - v.shape[0]))[None, :]


def _prep_layer(layer, P):
    j = layer // 2
    d = dict(
        ffn_wup=P['ffn_w_up'][layer].astype(BF16), ffn_cw=P['ffn_conv_w'][layer], ffn_cb=P['ffn_conv_b'][layer][None, :],
        ffn_wdn=P['ffn_w_down'][layer].astype(BF16),
        ln_mix_g=P['ln_mix_g'][layer][None, :], ln_mix_b=P['ln_mix_b'][layer][None, :],
        ln_ffn_g=P['ln_ffn_g'][layer][None, :], ln_ffn_b=P['ln_ffn_b'][layer][None, :])
    if layer % 2 == 0:
        w_out = P['w_out_even'][j].astype(BF16)
        wg = P['lru_w_gates'][j]
        eye = jnp.eye(LRU_BLOCKS, dtype=F32)
        wg = jnp.einsum('knde,nm->kndme', wg, eye).reshape(2, LRU_WIDTH, LRU_WIDTH)
        d.update(
            w_in=_even_w_in(P['w_in_even'][j]), w_out1=w_out[:EV_Q], w_out2=w_out[EV_Q:],
            cmp_tab=jnp.repeat(P['nsa_cmp_w'][j].reshape(CMP_BLOCK, 2 * NSA_KV_HEADS), HEAD_DIM, axis=1),
            cmp_page_tab=_page_compress_table(P['nsa_cmp_w'][j]),
            lru_cw=P['lru_conv_w'][j], lru_cb=P['lru_conv_b'][j][None, :],
            lru_wg=jnp.concatenate([wg[0], wg[1]], axis=1).astype(BF16),
            lru_bg=P['lru_b_gates'][j].reshape(1, 2 * LRU_WIDTH), lru_lam=P['lru_lambda'][j][None, :])
    else:
        w_out = P['w_out_odd'][j].astype(BF16)
        wa = jnp.zeros((LANES, GLA_W), F32).at[SSD_HEADS:SSD_HEADS + GLA_RANK].set(P['gla_w_alpha'][j])
        d.update(
            w_in=_odd_w_in(P['w_in_odd'][j]), w_out1=w_out[:SSD_INNER], w_out2=w_out[SSD_INNER:],
            ssd_cw=P['ssd_conv_w'][j], ssd_cb=P['ssd_conv_b'][j][None, :],
            ssd_dtb=_lane_pad(P['ssd_dt_bias'][j]), ssd_alog=_lane_pad(P['ssd_a_log'][j]),
            ssd_d=_lane_pad(P['ssd_d'][j]), ssd_nw=P['ssd_norm_w'][j][None, :],
            gla_wa=wa.astype(BF16), gla_ba=P['gla_b_alpha'][j][None, :], gla_nw=P['gla_norm_w'][j][None, :])
    return d


def _trunk(x3, prm, st, cfg):
    B, T, _ = x3.shape
    M = B * T
    out = dict(kv=[], win=[], lh=[], lc=[], sh=[], sc=[], gs=[], fc=[])
    for layer in range(DEPTH):
        p = prm[layer]
        j = layer // 2
        x2 = x3.reshape(M, D_MODEL)
        if layer % 2 == 0:
            q, rows, win, hx, hgate, hg = _proj(x2, p['w_in'], EVEN_SEGS, cfg['tm'], cfg['rope'], cfg['rope_blocks'])
            to3 = lambda a: a.reshape(B, T, a.shape[1])
            rows3, win3 = to3(rows), to3(win)
            if cfg['sample']:
                o_nsa, win_t = _nsa_sample(to3(q), to3(hg), rows3, win3, st['cache_t'], st['page_table'],
                                           j * st['n_phys'], st['cwin_t'], j * B, p['cmp_page_tab'], cfg['nsa_nb'])
                win_keep = win_t.transpose(0, 2, 1)
            else:
                H3 = _compress_prompt(rows3, p['cmp_tab'])
                o_nsa = _nsa_prompt(to3(q), to3(hg), H3, rows3, win3, cfg['tq'])
                win_keep = win3[:, T - min(WINDOW, T):]
            y_lru, h_last, conv_n = _lru(to3(hx), to3(hgate), st['lru_conv'], j * B, st['lru_h'], j * B,
                                         p['lru_cw'], p['lru_cb'], p['lru_wg'], p['lru_bg'], p['lru_lam'],
                                         cfg['lru_nb'], cfg['lru_tc'])
            a1, a2 = o_nsa.reshape(M, EV_Q), y_lru.reshape(M, LRU_WIDTH)
            out['kv'].append(rows3.reshape(B, T, 4, NSA_KV_HEADS, HEAD_DIM))
            out['win'].append(win_keep.reshape(B, -1, 2, NSA_KV_HEADS, HEAD_DIM))
            out['lh'].append(h_last.reshape(B, LRU_WIDTH))
            out['lc'].append(conv_n)
        else:
            z, xbc, gq, gk, gv, gg, sm = _proj(x2, p['w_in'], ODD_SEGS, cfg['tm'])
            to3 = lambda a: a.reshape(B, T, a.shape[1])
            sm3 = to3(sm)
            y_ssd, h_n, conv_n = _ssd(to3(xbc), to3(z), sm3, st['ssd_conv'], j * B, st['ssd_h'], j * B,
                                      p['ssd_cw'], p['ssd_cb'], p['ssd_dtb'], p['ssd_alog'], p['ssd_d'], p['ssd_nw'],
                                      cfg['ssd_L'])
            o_gla, s_n = _gla(to3(gq), to3(gk), to3(gv), to3(gg), sm3, st['gla_s'], j * B,
                              p['gla_wa'], p['gla_ba'], p['gla_nw'], cfg['gla_tc'])
            a1, a2 = y_ssd.reshape(M, SSD_INNER), o_gla.reshape(M, GLA_W)
            out['sh'].append(h_n.reshape(B, SSD_HEADS, SSD_HEAD_DIM, SSD_STATE))
            out['sc'].append(conv_n)
            out['gs'].append(s_n.reshape(B, GLA_HEADS, GLA_DK, GLA_DV))
        x2 = _outproj(a1, a2, p['w_out1'], p['w_out2'], x2, p['ln_mix_g'], p['ln_mix_b'], cfg['tm'])
        x3, fbuf = _ffn(x2.reshape(B, T, D_MODEL), p['ffn_wup'], p['ffn_cw'], p['ffn_cb'], p['ffn_wdn'],
                        st['ffn_conv'], layer * B, p['ln_ffn_g'], p['ln_ffn_b'],
                        cfg['ffn_nb'], cfg['ffn_tc'], cfg['ffn_cw'])
        out['fc'].append(fbuf)
    return x3, {k: jnp.stack(v) for k, v in out.items()}


def _largest_tile(n, cap):
    t = min(n, cap)
    while n % t:
        t //= 2
    return t


def kernel(x_prompt, x_sample, cache_nsa_kv, cache_nsa_win, state_lru_h, state_lru_conv, state_ssd, state_ssd_conv, state_gla, state_ffn_conv, page_table, w_in_even, w_out_even, nsa_cmp_w, lru_conv_w, lru_conv_b, lru_w_gates, lru_b_gates, lru_lambda, w_in_odd, w_out_odd, ssd_conv_w, ssd_conv_b, ssd_dt_bias, ssd_a_log, ssd_d, ssd_norm_w, gla_w_alpha, gla_b_alpha, gla_norm_w, ffn_w_up, ffn_conv_w, ffn_conv_b, ffn_w_down, ln_mix_g, ln_mix_b, ln_ffn_g, ln_ffn_b):
    P = dict(w_in_even=w_in_even, w_out_even=w_out_even, nsa_cmp_w=nsa_cmp_w, lru_conv_w=lru_conv_w,
             lru_conv_b=lru_conv_b, lru_w_gates=lru_w_gates, lru_b_gates=lru_b_gates, lru_lambda=lru_lambda,
             w_in_odd=w_in_odd, w_out_odd=w_out_odd, ssd_conv_w=ssd_conv_w, ssd_conv_b=ssd_conv_b,
             ssd_dt_bias=ssd_dt_bias, ssd_a_log=ssd_a_log, ssd_d=ssd_d, ssd_norm_w=ssd_norm_w,
             gla_w_alpha=gla_w_alpha, gla_b_alpha=gla_b_alpha, gla_norm_w=gla_norm_w, ffn_w_up=ffn_w_up,
             ffn_conv_w=ffn_conv_w, ffn_conv_b=ffn_conv_b, ffn_w_down=ffn_w_down, ln_mix_g=ln_mix_g,
             ln_mix_b=ln_mix_b, ln_ffn_g=ln_ffn_g, ln_ffn_b=ln_ffn_b)
    prm = [_prep_layer(layer, P) for layer in range(DEPTH)]
    n_even, n_odd = w_in_even.shape[0], w_in_odd.shape[0]
    ffn_buf = (FFN_CONV - 1, 2 * D_FF)

    Bp, Tp, _ = x_prompt.shape
    tm_p = _largest_tile(Tp, 512)
    st_p = dict(
        lru_conv=jnp.zeros((n_even * Bp, SHORT_CONV - 1, LRU_WIDTH), F32), lru_h=jnp.zeros((n_even * Bp, 1, LRU_WIDTH), F32),
        ssd_conv=jnp.zeros((n_odd * Bp, SHORT_CONV - 1, SSD_CONV_DIM), F32),
        ssd_h=jnp.zeros((n_odd * Bp, SSD_INNER, SSD_STATE), F32), gla_s=jnp.zeros((n_odd * Bp, GLA_W, GLA_DV), F32),
        ffn_conv=jnp.zeros((DEPTH * Bp,) + ffn_buf, F32))
    cfg_p = dict(sample=False, tm=tm_p, rope=_rope_tables(jnp.arange(Tp)), rope_blocks=Tp // tm_p,
                 tq=_largest_tile(Tp, 256), lru_nb=1, lru_tc=_largest_tile(Tp, 512),
                 ssd_L=_largest_tile(Tp, 128), gla_tc=_largest_tile(Tp, 256),
                 ffn_nb=1, ffn_tc=_largest_tile(Tp, 512), ffn_cw=D_FF // 2)
    y_p, o_p = _trunk(x_prompt, prm, st_p, cfg_p)

    Bs, Ts, _ = x_sample.shape
    n_phys = cache_nsa_kv.shape[1]
    past_len = page_table.shape[1] * PAGE_SIZE
    tm_s = _largest_tile(Bs * Ts, 256)
    pos_s = past_len + jnp.arange(tm_s) % Ts
    nb_s = _largest_tile(Bs, 32)
    st_s = dict(
        cache_t=cache_nsa_kv.transpose(0, 1, 3, 4, 5, 2).reshape(n_even * n_phys, 4 * LANES, PAGE_SIZE),
        n_phys=n_phys, page_table=page_table,
        cwin_t=cache_nsa_win.transpose(0, 1, 3, 4, 5, 2).reshape(n_even * Bs, 2 * LANES, cache_nsa_win.shape[2]),
        lru_conv=state_lru_conv.reshape(n_even * Bs, SHORT_CONV - 1, LRU_WIDTH),
        lru_h=state_lru_h.reshape(n_even * Bs, 1, LRU_WIDTH),
        ssd_conv=state_ssd_conv.reshape(n_odd * Bs, SHORT_CONV - 1, SSD_CONV_DIM),
        ssd_h=state_ssd.reshape(n_odd * Bs, SSD_INNER, SSD_STATE),
        gla_s=state_gla.reshape(n_odd * Bs, GLA_W, GLA_DV),
        ffn_conv=state_ffn_conv.reshape((DEPTH * Bs,) + ffn_buf))
    cfg_s = dict(sample=True, tm=tm_s, rope=_rope_tables(pos_s), rope_blocks=1, nsa_nb=_largest_tile(Bs, 4),
                 lru_nb=nb_s, lru_tc=Ts, ssd_L=Ts, gla_tc=Ts, ffn_nb=nb_s, ffn_tc=Ts, ffn_cw=D_FF // 2)
    y_s, o_s = _trunk(x_sample, prm, st_s, cfg_s)

    return (y_p, y_s, o_p['kv'], o_s['kv'], o_p['win'], o_s['win'], o_p['lh'], o_s['lh'], o_p['lc'], o_s['lc'],
            o_p['sh'], o_s['sh'], o_p['sc'], o_s['sc'], o_p['gs'], o_s['gs'], o_p['fc'], o_s['fc'])
```

```python
import functools

import jax
import jax.numpy as jnp
import numpy as np
from jax import lax
from jax.experimental import pallas as pl
from jax.experimental.pallas import tpu as pltpu

F32 = jnp.float32
BF16 = jnp.bfloat16
HIGHEST = lax.Precision.HIGHEST

D_MODEL = 1024
DEPTH = 4
PAGE_SIZE = 128
HEAD_DIM = 64
ROPE_DIM = HEAD_DIM // 4
ROPE_THETA = 500000.0
NSA_HEADS = 8
NSA_KV_HEADS = 2
NSA_GROUP = NSA_HEADS // NSA_KV_HEADS
CMP_BLOCK = 32
CMP_STRIDE = 16
SEL_BLOCK = 64
N_SEL = 8
WINDOW = 512
LRU_WIDTH = D_MODEL // 2
LRU_BLOCKS = 8
LRU_BLOCK_DIM = LRU_WIDTH // LRU_BLOCKS
LRU_C = 8.0
SHORT_CONV = 4
SSD_HEADS = 16
SSD_HEAD_DIM = 64
SSD_INNER = SSD_HEADS * SSD_HEAD_DIM
SSD_GROUPS = 2
SSD_STATE = 128
SSD_CONV_DIM = SSD_INNER + 2 * SSD_GROUPS * SSD_STATE
GLA_HEADS = 4
GLA_DK = 128
GLA_DV = 128
GLA_RANK = 16
GLA_TAU = 16.0
GLA_CHUNK = 32
D_FF = 2816
FFN_CONV = 3
ALPHA = (2.0 * DEPTH) ** 0.25
LN_EPS = 1e-5
NEG = -1e30
EV_Q = NSA_HEADS * HEAD_DIM
EV_KV = 6 * NSA_KV_HEADS * HEAD_DIM
EV_GATE = 3 * NSA_HEADS
GLA_W = GLA_HEADS * GLA_DK

LANES = 128
SUBLANES = 8
V7X_VMEM_BYTES = 64 * 1024 * 1024
VMEM_LIMIT = V7X_VMEM_BYTES - 8 * 1024 * 1024
HALO = SUBLANES


def _cparams(*sem):
    return pltpu.CompilerParams(dimension_semantics=sem, vmem_limit_bytes=VMEM_LIMIT)


def _resident(a):
    return pl.BlockSpec(a.shape, lambda *_: (0,) * a.ndim, pipeline_mode=pl.Buffered(1))


def _bdot(a, b):
    return jnp.dot(a.astype(BF16), b.astype(BF16), preferred_element_type=F32)


def _bdot_nt(a, b):
    return lax.dot_general(a.astype(BF16), b.astype(BF16), (((1,), (1,)), ((), ())), preferred_element_type=F32)


def _bdot_tn(a, b):
    return lax.dot_general(a.astype(BF16), b.astype(BF16), (((0,), (0,)), ((), ())), preferred_element_type=F32)


def _iota(shape, axis):
    return lax.broadcasted_iota(jnp.int32, shape, axis)


def _layer_norm(y, g, b):
    mu = jnp.mean(y, -1, keepdims=True)
    d = y - mu
    var = jnp.mean(d * d, -1, keepdims=True)
    return d * lax.rsqrt(var + LN_EPS) * g + b


def _masked_softmax(s, mask):
    s = jnp.where(mask, s, NEG)
    m = jnp.max(s, -1, keepdims=True)
    e = jnp.exp(s - m)
    p = e / jnp.sum(e, -1, keepdims=True)
    return jnp.where(mask, p, 0.0)


def _proj_body(*refs, segs, has_rope):
    if has_rope:
        x_ref, w_ref, c_ref, s1_ref, s2_ref = refs[:5]
        out_refs = refs[5:]
    else:
        x_ref, w_ref = refs[:2]
        out_refs = refs[2:]
    xb = x_ref[...].astype(BF16)
    for o_ref, (start, width, rope) in zip(out_refs, segs):
        acc = jnp.dot(xb, w_ref[:, start:start + width], preferred_element_type=F32)
        if rope is None:
            o_ref[...] = acc
            continue
        for c, flag in enumerate(rope):
            chunk = acc[:, c * LANES:(c + 1) * LANES]
            if flag:
                chunk = (chunk * c_ref[...] + pltpu.roll(chunk, LANES - ROPE_DIM // 2, 1) * s2_ref[...]
                         + pltpu.roll(chunk, ROPE_DIM // 2, 1) * s1_ref[...])
            o_ref[:, c * LANES:(c + 1) * LANES] = chunk


def _proj(x2d, w, segs, tm, tabs=None, tab_blocks=1):
    M, K = x2d.shape
    N = w.shape[1]
    has_rope = tabs is not None
    in_specs = [pl.BlockSpec((tm, K), lambda i: (i, 0)), _resident(w)]
    args = [x2d, w]
    if has_rope:
        in_specs += [pl.BlockSpec((tm, LANES), lambda i: (i % tab_blocks, 0))] * 3
        args += list(tabs)
    return pl.pallas_call(
        functools.partial(_proj_body, segs=segs, has_rope=has_rope),
        grid=(M // tm,),
        in_specs=in_specs,
        out_specs=[pl.BlockSpec((tm, wd), lambda i: (i, 0)) for _, wd, _ in segs],
        out_shape=[jax.ShapeDtypeStruct((M, wd), F32) for _, wd, _ in segs],
        compiler_params=_cparams("parallel"),
        name="proj",
    )(*args)


def _rope_tables(pos):
    half = ROPE_DIM // 2
    inv = jnp.power(ROPE_THETA, -2.0 * jnp.arange(half, dtype=F32) / ROPE_DIM)
    ang = pos.astype(F32)[:, None] * inv[None, :]
    cos, sin = jnp.cos(ang), jnp.sin(ang)
    R = pos.shape[0]
    rest = HEAD_DIM - ROPE_DIM
    c64 = jnp.concatenate([cos, cos, jnp.ones((R, rest), F32)], 1)
    s1 = jnp.concatenate([jnp.zeros((R, half), F32), sin, jnp.zeros((R, rest), F32)], 1)
    s2 = jnp.concatenate([-sin, jnp.zeros((R, half + rest), F32)], 1)
    return tuple(jnp.concatenate([t, t], 1) for t in (c64, s1, s2))


def _compress_rows(x, w_ref):
    x3 = x.reshape(x.shape[0] // CMP_STRIDE, CMP_STRIDE, 2 * LANES)
    return (jnp.sum(x3 * w_ref[0:CMP_STRIDE, :][None], axis=1),
            jnp.sum(x3 * w_ref[CMP_STRIDE:CMP_BLOCK, :][None], axis=1))


def _compress_body(kv_ref, w_ref, o_ref):
    h0, h1 = _compress_rows(kv_ref[0], w_ref)
    o_ref[0, :, 0:2 * LANES] = h0
    o_ref[0, :, 2 * LANES:4 * LANES] = h1


def _compress_prompt(rows3, wtab):
    B, T, _ = rows3.shape
    return pl.pallas_call(
        _compress_body,
        grid=(B,),
        in_specs=[pl.BlockSpec((1, T, 2 * LANES), lambda b: (b, 0, 0)),
                  pl.BlockSpec((CMP_BLOCK, 2 * LANES), lambda b: (0, 0))],
        out_specs=pl.BlockSpec((1, T // CMP_STRIDE, 4 * LANES), lambda b: (b, 0, 0)),
        out_shape=jax.ShapeDtypeStruct((B, T // CMP_STRIDE, 4 * LANES), F32),
        compiler_params=_cparams("parallel"),
        name="compress_prompt",
    )(rows3, wtab)


def _overlap_matrix(n_rows, n_cmp, n_sel):
    s1 = np.arange(n_rows)[:, None] * CMP_STRIDE
    s2 = np.arange(LANES)[None, :] * SEL_BLOCK
    ov = np.clip(np.minimum(s1 + CMP_BLOCK, s2 + SEL_BLOCK) - np.maximum(s1, s2), 0, None) / CMP_BLOCK
    ov = ov * (np.arange(n_rows)[:, None] < n_cmp) * (np.arange(LANES)[None, :] < n_sel)
    return jnp.asarray(ov, dtype=F32)


def _select_blocks(psum, ov, t_col, n_sel):
    imp = jnp.dot(psum, ov, precision=HIGHEST, preferred_element_type=F32)
    blk = _iota(imp.shape, 1)
    cur = jnp.right_shift(t_col, int(np.log2(SEL_BLOCK)))
    future = blk * SEL_BLOCK > t_col
    forced = (blk == 0) | (blk == cur) | (blk == cur - 1)
    w = jnp.where(future, -1.0, jnp.where(forced, 1e6, imp))
    w = jnp.where(blk < n_sel, w, -jnp.inf)
    n_pad = -(-n_sel // SUBLANES) * SUBLANES
    wt = w.T[0:n_pad, :]
    sub = _iota((n_pad, 1), 0)
    rank = jnp.zeros(wt.shape, F32)
    for i in range(n_sel):
        wi = wt[i:i + 1, :]
        rank = rank + jnp.where((wi > wt) | ((wi == wt) & (sub > i)), 1.0, 0.0)
    sel_t = jnp.where((rank < min(N_SEL, n_sel)) & (sub < n_sel), 1.0, 0.0)
    if n_pad < LANES:
        sel_t = jnp.concatenate([sel_t, jnp.zeros((LANES - n_pad, sel_t.shape[1]), F32)], axis=0)
    return sel_t.T


def _stack_heads(q, h):
    G = NSA_GROUP
    qh = jnp.concatenate([q[:, (h * G + g) * HEAD_DIM:(h * G + g + 1) * HEAD_DIM] for g in range(G)], axis=0)
    return (qh * HEAD_DIM ** -0.5).astype(BF16)


def _expand_matrix(K):
    blk = jnp.right_shift(_iota((LANES, K), 1), int(np.log2(SEL_BLOCK)))
    return (blk == _iota((LANES, K), 0)).astype(BF16)


def _attend(qb, chunks, mask, tq):
    G = NSA_GROUP
    dh = HEAD_DIM
    bias = jnp.where(mask, 0.0, NEG)
    ks = [k.astype(BF16) for _, k, _ in chunks]
    vs = [jnp.concatenate([v.astype(BF16), jnp.ones(v.shape, BF16)], axis=0 if tr else 1) for tr, _, v in chunks]
    reps = G if tq >= LANES else 1
    rows = G * tq // reps
    outs = []
    for r in range(reps):
        q_r = qb[r * rows:(r + 1) * rows]
        s = [jnp.dot(q_r, k, preferred_element_type=F32) if tr else
             lax.dot_general(q_r, k, (((1,), (1,)), ((), ())), preferred_element_type=F32)
             for (tr, _, _), k in zip(chunks, ks)]
        s = s[0] if len(s) == 1 else jnp.concatenate(s, axis=-1)
        K = s.shape[-1]
        s = (s.reshape(rows // tq, tq, K) + bias[None]).reshape(rows, K)
        e = jnp.exp(s - jnp.max(s, -1, keepdims=True)).astype(BF16)
        ox, off = None, 0
        for (tr, _, _), v in zip(chunks, vs):
            kc = v.shape[1] if tr else v.shape[0]
            part = (lax.dot_general(e[:, off:off + kc], v, (((1,), (1,)), ((), ())), preferred_element_type=F32)
                    if tr else jnp.dot(e[:, off:off + kc], v, preferred_element_type=F32))
            ox = part if ox is None else ox + part
            off += kc
        outs.append(ox[:, :dh] / ox[:, dh:dh + 1])
    return outs[0] if reps == 1 else jnp.concatenate(outs, axis=0)


def _gate_store(o_ref, sg, h, tq, o_c, o_s, o_w):
    for g in range(NSA_GROUP):
        hh = h * NSA_GROUP + g
        r = slice(g * tq, (g + 1) * tq)
        o_ref[:, hh * HEAD_DIM:(hh + 1) * HEAD_DIM] = (
            sg[:, 3 * hh:3 * hh + 1] * o_c[r] + sg[:, 3 * hh + 1:3 * hh + 2] * o_s[r]
            + sg[:, 3 * hh + 2:3 * hh + 3] * o_w[r])


def _nsa_core(items, ov, t_col, n_sel):
    tq = t_col.shape[0]
    G = NSA_GROUP
    o_cs, psums = [], []
    for it in items:
        n_rows = it['ckk'].shape[0]
        p_c = _masked_softmax(_bdot_nt(it['qb'], it['ckk']).reshape(G, tq, n_rows), it['mask_c'][None])
        o_cs.append(_bdot(p_c.reshape(G * tq, n_rows), it['ckv']))
        psum = p_c[0]
        for g in range(1, G):
            psum = psum + p_c[g]
        psums.append(psum)
    o_ws = [it['win'](it['qb']) for it in items]
    n = len(items)
    sel = _select_blocks(jnp.concatenate(psums, axis=0), ov, jnp.concatenate([t_col] * n, axis=0), n_sel)
    for i, it in enumerate(items):
        o_s = it['slc'](it['qb'], sel[i * tq:(i + 1) * tq])
        it['finish'](o_cs[i], o_s, o_ws[i])


def _combine_compressed(h0, h1):
    return h0 + pltpu.roll(h1, h1.shape[0] - 1, 0)


def _nsa_prompt_body(q_ref, hg_ref, H_ref, rows_ref, win_ref, ov_ref, o_ref, *, T, tq):
    qi = pl.program_id(1)
    n_rows = T // CMP_STRIDE
    n_cmp = n_rows - CMP_BLOCK // CMP_STRIDE + 1
    n_sel = T // SEL_BLOCK
    t_col = qi * tq + _iota((tq, 1), 0)
    ck = _combine_compressed(H_ref[0, :, 0:2 * LANES], H_ref[0, :, 2 * LANES:4 * LANES])
    ncol = _iota((1, n_rows), 1)
    mask_c = (ncol * CMP_STRIDE + CMP_BLOCK - 1 <= t_col) & (ncol < n_cmp)
    band = WINDOW + tq
    wstart = pl.multiple_of(jnp.clip(qi * tq - WINDOW, 0, T - band), tq)
    wpos = wstart + _iota((1, band), 1)
    mask_w = (wpos <= t_col) & (wpos > t_col - WINDOW)
    q = q_ref[0]
    sg = jax.nn.sigmoid(hg_ref[0])
    kstep = min(T, 4 * LANES)

    def make_item(h):
        def slc_span(K):
            def run(qb, sel):
                allowed = jnp.dot(sel.astype(BF16), _expand_matrix(K), preferred_element_type=F32) > 0.5
                mask = allowed & (_iota((1, K), 1) <= t_col)
                k = rows_ref[0, 0:K, 2 * LANES + h * HEAD_DIM:2 * LANES + (h + 1) * HEAD_DIM]
                v = rows_ref[0, 0:K, 3 * LANES + h * HEAD_DIM:3 * LANES + (h + 1) * HEAD_DIM]
                return _attend(qb, [(False, k, v)], mask, tq)
            return run

        def slc(qb, sel):
            spans = [slc_span((i + 1) * kstep) for i in range(T // kstep)]
            if len(spans) == 1:
                return spans[0](qb, sel)
            return lax.switch(lax.div(qi * tq + (tq - 1), kstep), spans, qb, sel)

        def win(qb):
            k = win_ref[0, pl.ds(wstart, band), h * HEAD_DIM:(h + 1) * HEAD_DIM]
            v = win_ref[0, pl.ds(wstart, band), LANES + h * HEAD_DIM:LANES + (h + 1) * HEAD_DIM]
            return _attend(qb, [(False, k, v)], mask_w, tq)

        return dict(qb=_stack_heads(q, h), ckk=ck[:, h * HEAD_DIM:(h + 1) * HEAD_DIM],
                    ckv=ck[:, LANES + h * HEAD_DIM:LANES + (h + 1) * HEAD_DIM], mask_c=mask_c,
                    slc=slc, win=win, finish=functools.partial(_gate_store, o_ref.at[0], sg, h, tq))

    _nsa_core([make_item(h) for h in range(NSA_KV_HEADS)], ov_ref[...], t_col, n_sel)


def _nsa_prompt(q3, hg3, H3, rows3, win3, tq):
    B, T, _ = q3.shape
    n_rows = T // CMP_STRIDE
    ov = _overlap_matrix(n_rows, n_rows - 1, T // SEL_BLOCK)
    return pl.pallas_call(
        functools.partial(_nsa_prompt_body, T=T, tq=tq),
        grid=(B, T // tq),
        in_specs=[pl.BlockSpec((1, tq, EV_Q), lambda b, i: (b, i, 0)),
                  pl.BlockSpec((1, tq, LANES), lambda b, i: (b, i, 0)),
                  pl.BlockSpec((1, n_rows, 4 * LANES), lambda b, i: (b, 0, 0)),
                  pl.BlockSpec((1, T, 4 * LANES), lambda b, i: (b, 0, 0)),
                  pl.BlockSpec((1, T, 2 * LANES), lambda b, i: (b, 0, 0)),
                  pl.BlockSpec((n_rows, LANES), lambda b, i: (0, 0))],
        out_specs=pl.BlockSpec((1, tq, EV_Q), lambda b, i: (b, i, 0)),
        out_shape=jax.ShapeDtypeStruct((B, T, EV_Q), F32),
        compiler_params=_cparams("parallel", "arbitrary"),
        name="nsa_prompt",
    )(q3, hg3, H3, rows3, win3, ov)


def _nsa_sample_body(*refs, nb, n_pages, Tq, Wb):
    q_ref, hg_ref, rnew_ref, wnew_ref, ctab_ref, ov_ref = refs[1:7]
    page_refs = refs[7:7 + nb * n_pages]
    cwin_ref, o_ref, knew, wnew = refs[7 + nb * n_pages:]
    P = n_pages * PAGE_SIZE
    Ks = P + PAGE_SIZE
    Kw = Wb + PAGE_SIZE
    n_rows = P // CMP_STRIDE
    n_cmp = n_rows - CMP_BLOCK // CMP_STRIDE + 1
    n_sel = -(-(P + Tq) // SEL_BLOCK)
    tt = _iota((Tq, 1), 0)
    t_col = P + tt
    mask_c = (_iota((1, n_rows), 1) < n_cmp) & (tt >= 0)
    causal = _iota((1, Ks), 1) <= t_col
    wpos = _iota((1, Kw), 1)
    mask_w = (wpos <= Wb + tt) & (wpos > Wb + tt - WINDOW)
    expand = _expand_matrix(Ks)
    pad_rows = jnp.zeros((PAGE_SIZE - Tq, 2 * LANES), F32)

    n_grp = 2 * NSA_KV_HEADS
    per_page = PAGE_SIZE // CMP_STRIDE

    def make_item(i, h, cks, sg, pages):
        ksl = slice(h * HEAD_DIM, (h + 1) * HEAD_DIM)
        vsl = slice(LANES + h * HEAD_DIM, LANES + (h + 1) * HEAD_DIM)

        def slc(qb, sel):
            mask = (jnp.dot(sel.astype(BF16), expand, preferred_element_type=F32) > 0.5) & causal
            chunks = [(True, pr[0, 2 * LANES + ksl.start:2 * LANES + ksl.stop, :],
                       pr[0, 2 * LANES + vsl.start:2 * LANES + vsl.stop, :]) for pr in pages]
            chunks.append((False, knew[i, :, ksl], knew[i, :, vsl]))
            return _attend(qb, chunks, mask, Tq)

        def win(qb):
            chunks = [(True, cwin_ref[i, ksl, :], cwin_ref[i, vsl, :]), (False, wnew[i, :, ksl], wnew[i, :, vsl])]
            return _attend(qb, chunks, mask_w, Tq)

        return dict(qb=_stack_heads(q_ref[i], h), ckk=cks[h], ckv=cks[NSA_KV_HEADS + h], mask_c=mask_c,
                    slc=slc, win=win, finish=functools.partial(_gate_store, o_ref.at[i], sg, h, Tq))

    items = []
    for i in range(nb):
        knew[i, 0:Tq, :] = rnew_ref[i, :, 2 * LANES:4 * LANES]
        knew[i, Tq:PAGE_SIZE, :] = pad_rows
        wnew[i, 0:Tq, :] = wnew_ref[i]
        wnew[i, Tq:PAGE_SIZE, :] = pad_rows
        pages = page_refs[i * n_pages:(i + 1) * n_pages]
        parts = [_bdot_nt(ctab_ref[...], pr[0, 0:2 * LANES, :]) for pr in pages]
        cks = []
        for g in range(n_grp):
            r0, cs = g * 2 * per_page, slice(g * HEAD_DIM, (g + 1) * HEAD_DIM)
            cks.append(_combine_compressed(
                jnp.concatenate([pp[r0:r0 + per_page, cs] for pp in parts], axis=0),
                jnp.concatenate([pp[r0 + per_page:r0 + 2 * per_page, cs] for pp in parts], axis=0)))
        sg = jax.nn.sigmoid(hg_ref[i])
        items += [make_item(i, h, cks, sg, pages) for h in range(NSA_KV_HEADS)]
    _nsa_core(items, ov_ref[...], t_col, n_sel)


def _page_compress_table(cmp_w):
    per_page = PAGE_SIZE // CMP_STRIDE
    w = cmp_w.reshape(CMP_BLOCK, 2 * NSA_KV_HEADS).T
    s = jnp.arange(PAGE_SIZE)[None, :] - CMP_STRIDE * jnp.arange(per_page)[:, None]
    inside = (s >= 0) & (s < CMP_STRIDE)
    sc = jnp.clip(s, 0, CMP_STRIDE - 1)
    first = jnp.where(inside[None], w[:, sc], 0.0)
    second = jnp.where(inside[None], w[:, CMP_STRIDE + sc], 0.0)
    return jnp.concatenate([first, second], axis=1).reshape(-1, PAGE_SIZE).astype(BF16)


def _nsa_sample(q3, hg3, rows3, win3, cache_t, page_table, page_base, cwin_t, cwin_base, ctab, nb):
    B, Tq, _ = q3.shape
    n_pages = page_table.shape[1]
    P = n_pages * PAGE_SIZE
    Wb = cwin_t.shape[2]
    assert Wb % LANES == 0 and Wb > LANES and Tq <= SUBLANES and B % nb == 0
    n_rows = P // CMP_STRIDE
    n_sel = -(-(P + Tq) // SEL_BLOCK)
    ov = _overlap_matrix(n_rows, n_rows - 1, n_sel)
    cb0 = cwin_base // nb

    def page_spec(i, p):
        return pl.BlockSpec((1, 4 * LANES, PAGE_SIZE), lambda b, pt: (page_base + pt[b * nb + i, p], 0, 0))

    tok = lambda a: pl.BlockSpec((nb, Tq, a.shape[2]), lambda b, pt: (b, 0, 0))
    grid_spec = pltpu.PrefetchScalarGridSpec(
        num_scalar_prefetch=1,
        grid=(B // nb,),
        in_specs=[tok(q3), tok(hg3), tok(rows3), tok(win3),
                  pl.BlockSpec(ctab.shape, lambda b, pt: (0, 0)), pl.BlockSpec(ov.shape, lambda b, pt: (0, 0))]
        + [page_spec(i, p) for i in range(nb) for p in range(n_pages)]
        + [pl.BlockSpec((nb, 2 * LANES, Wb), lambda b, pt: (cb0 + b, 0, 0))],
        out_specs=tok(q3),
        scratch_shapes=[pltpu.VMEM((nb, PAGE_SIZE, 2 * LANES), F32), pltpu.VMEM((nb, PAGE_SIZE, 2 * LANES), F32)],
    )
    return pl.pallas_call(
        functools.partial(_nsa_sample_body, nb=nb, n_pages=n_pages, Tq=Tq, Wb=Wb),
        grid_spec=grid_spec,
        out_shape=jax.ShapeDtypeStruct((B, Tq, EV_Q), F32),
        compiler_params=_cparams("arbitrary"),
        name="nsa_sample",
    )(page_table, q3, hg3, rows3, win3, ctab, ov, *([cache_t] * (nb * n_pages)), cwin_t)


def _window_append_body(cwin_ref, new_ref, o_ref, pad, *, nb, Tq, Wb):
    new_lanes = _iota((1, LANES), 1) >= LANES - Tq
    pad[Tq:LANES, :] = jnp.zeros((LANES - Tq, pad.shape[1]), F32)
    for i in range(nb):
        pad[0:Tq, :] = new_ref[i]
        new_t = pltpu.roll(pad[...].T, LANES - Tq, 1)
        shifted = pltpu.roll(cwin_ref[i], Wb - Tq, 1)
        o_ref[i, :, 0:Wb - LANES] = shifted[:, 0:Wb - LANES]
        o_ref[i, :, Wb - LANES:Wb] = jnp.where(new_lanes, new_t, shifted[:, Wb - LANES:Wb])


def _window_append(cwin_t, new_rows, nb):
    S, C, Wb = cwin_t.shape
    Tq = new_rows.shape[1]
    assert Wb % LANES == 0 and Wb > LANES and Tq <= LANES and S % nb == 0
    return pl.pallas_call(
        functools.partial(_window_append_body, nb=nb, Tq=Tq, Wb=Wb),
        grid=(S // nb,),
        in_specs=[pl.BlockSpec((nb, C, Wb), lambda s: (s, 0, 0)), pl.BlockSpec((nb, Tq, C), lambda s: (s, 0, 0))],
        out_specs=pl.BlockSpec((nb, C, Wb), lambda s: (s, 0, 0)),
        out_shape=jax.ShapeDtypeStruct(cwin_t.shape, F32),
        scratch_shapes=[pltpu.VMEM((LANES, C), F32)],
        compiler_params=_cparams("parallel"),
        name="window_append",
    )(cwin_t, new_rows)


def _causal_conv(xp, w_ref, b_ref, tc, cols=None):
    K = w_ref.shape[0]
    cs = slice(None) if cols is None else cols
    acc = None
    for j in range(K):
        term = w_ref[j:j + 1, cs][None] * xp[:, HALO - (K - 1) + j:HALO - (K - 1) + j + tc, :]
        acc = term if acc is None else acc + term
    return b_ref[:, cs][None] + acc


def _lru_body(hx_ref, hgate_ref, conv0_ref, h0_ref, cw_ref, cb_ref, wg_ref, bg_ref, lam_ref,
              y_ref, hlast_ref, convn_ref, xp, *, nb, tc):
    K1 = SHORT_CONV - 1

    @pl.when(pl.program_id(1) == 0)
    def _():
        xp[:, HALO - K1:HALO, :] = conv0_ref[...]
        hlast_ref[...] = h0_ref[...]

    xp[:, HALO:HALO + tc, :] = hx_ref[...]
    xc = _causal_conv(xp, cw_ref, cb_ref, tc)
    tail = xp[:, HALO + tc - K1:HALO + tc, :]
    convn_ref[...] = tail
    xp[:, HALO - K1:HALO, :] = tail
    R = nb * tc
    xc2 = xc.reshape(R, LRU_WIDTH)
    gt = _bdot(xc2, wg_ref[...]) + bg_ref[...]
    r_gate = jax.nn.sigmoid(gt[:, :LRU_WIDTH])
    i_gate = jax.nn.sigmoid(gt[:, LRU_WIDTH:])
    log_a = -LRU_C * r_gate * jax.nn.softplus(-lam_ref[...])
    a = jnp.exp(log_a)
    th = jnp.tanh(log_a)
    u = jnp.sqrt(-2.0 * th / (1.0 - th)) * i_gate * xc2
    tpos = lax.rem(_iota((R, 1), 0), tc)
    d = 1
    while d < tc:
        valid = tpos >= d
        u = jnp.where(valid, a * pltpu.roll(u, d, 0) + u, u)
        a = jnp.where(valid, a * pltpu.roll(a, d, 0), a)
        d *= 2
    hprev = jnp.broadcast_to(hlast_ref[...], (nb, tc, LRU_WIDTH)).reshape(R, LRU_WIDTH)
    h = a * hprev + u
    y_ref[...] = (h * jax.nn.gelu(hgate_ref[...].reshape(R, LRU_WIDTH))).reshape(nb, tc, LRU_WIDTH)
    last = _iota((1, tc, 1), 1) == tc - 1
    hlast_ref[...] = jnp.sum(jnp.where(last, h.reshape(nb, tc, LRU_WIDTH), 0.0), axis=1, keepdims=True)


def _lru(hx3, hgate3, conv0, conv_base, h0, h_base, cw, cb, wg, bg, lam, nb, tc):
    B, T, C = hx3.shape
    K1 = SHORT_CONV - 1
    cb0 = conv_base // nb
    hb0 = h_base // nb
    tok = pl.BlockSpec((nb, tc, C), lambda b, t: (b, t, 0))
    full = lambda a: pl.BlockSpec(a.shape, lambda b, t: (0,) * a.ndim)
    return pl.pallas_call(
        functools.partial(_lru_body, nb=nb, tc=tc),
        grid=(B // nb, T // tc),
        in_specs=[tok, tok,
                  pl.BlockSpec((nb, K1, C), lambda b, t: (cb0 + b, 0, 0)),
                  pl.BlockSpec((nb, 1, C), lambda b, t: (hb0 + b, 0, 0)),
                  full(cw), full(cb), full(wg), full(bg), full(lam)],
        out_specs=[tok, pl.BlockSpec((nb, 1, C), lambda b, t: (b, 0, 0)),
                   pl.BlockSpec((nb, K1, C), lambda b, t: (b, 0, 0))],
        out_shape=[jax.ShapeDtypeStruct((B, T, C), F32), jax.ShapeDtypeStruct((B, 1, C), F32),
                   jax.ShapeDtypeStruct((B, K1, C), F32)],
        scratch_shapes=[pltpu.VMEM((nb, HALO + tc, C), F32)],
        compiler_params=_cparams("parallel", "arbitrary"),
        name="lru",
    )(hx3, hgate3, conv0, h0, cw, cb, wg, bg, lam)


def _ffn_body(x_ref, a1_ref, a2_ref, w1_ref, w2_ref, gm_ref, bm_ref, wup_ref, cw_ref, cb_ref, wdn_ref, buf0_ref,
              g_ref, b_ref, y_ref, bufn_ref, sg, sv, *, nb, tc, cw):
    K1 = FFN_CONV - 1

    @pl.when(pl.program_id(1) == 0)
    def _():
        bufn_ref[...] = buf0_ref[...]

    R = nb * tc
    mix = (_bdot(a1_ref[...].reshape(R, a1_ref.shape[2]), w1_ref[...])
           + _bdot(a2_ref[...].reshape(R, a2_ref.shape[2]), w2_ref[...]))
    x = _layer_norm(ALPHA * x_ref[...].reshape(R, D_MODEL) + mix, gm_ref[...], bm_ref[...])
    xb = x.astype(BF16)
    acc = jnp.zeros((R, D_MODEL), F32)
    for c in range(D_FF // cw):
        conv = []
        for half, scr in ((0, sg), (1, sv)):
            cols = slice(half * D_FF + c * cw, half * D_FF + (c + 1) * cw)
            u = jnp.dot(xb, wup_ref[:, cols], preferred_element_type=F32)
            scr[:, HALO:HALO + tc, :] = u.reshape(nb, tc, cw)
            scr[:, HALO - K1:HALO, :] = bufn_ref[:, :, cols]
            conv.append(_causal_conv(scr, cw_ref, cb_ref, tc, cols).reshape(R, cw))
            bufn_ref[:, :, cols] = scr[:, HALO + tc - K1:HALO + tc, :]
        act = jax.nn.gelu(conv[0]) * conv[1]
        acc = acc + jnp.dot(act.astype(BF16), wdn_ref[c * cw:(c + 1) * cw, :], preferred_element_type=F32)
    y = _layer_norm(ALPHA * x + acc, g_ref[...], b_ref[...])
    y_ref[...] = y.reshape(nb, tc, D_MODEL)


def _mix_ffn(x3, a1, a2, w1, w2, gm, bm, layer, wup, cw, cb, wdn, buf0, buf_base, g, b, nb, tc, cwid):
    B, T, _ = x3.shape
    U, K1 = 2 * D_FF, FFN_CONV - 1
    bb0 = buf_base // nb
    tokspec = lambda a: pl.BlockSpec((nb, tc, a.shape[2]), lambda bi, t: (bi, t, 0))
    tok = tokspec(x3)
    full = _resident
    of_layer = lambda a: pl.BlockSpec((None,) + a.shape[1:], lambda *_: (layer, 0, 0), pipeline_mode=pl.Buffered(1))
    return pl.pallas_call(
        functools.partial(_ffn_body, nb=nb, tc=tc, cw=cwid),
        grid=(B // nb, T // tc),
        in_specs=[tok, tokspec(a1), tokspec(a2), full(w1), full(w2), full(gm), full(bm),
                  of_layer(wup), full(cw), full(cb), of_layer(wdn),
                  pl.BlockSpec((nb, K1, U), lambda bi, t: (bb0 + bi, 0, 0)), full(g), full(b)],
        out_specs=[tok, pl.BlockSpec((nb, K1, U), lambda bi, t: (bi, 0, 0))],
        out_shape=[jax.ShapeDtypeStruct(x3.shape, F32), jax.ShapeDtypeStruct((B, K1, U), F32)],
        scratch_shapes=[pltpu.VMEM((nb, HALO + tc, cwid), F32), pltpu.VMEM((nb, HALO + tc, cwid), F32)],
        compiler_params=_cparams("parallel", "arbitrary"),
        name="mix_ffn",
    )(x3, a1, a2, w1, w2, gm, bm, wup, cw, cb, wdn, buf0, g, b)


def _ssd_body(*refs, nb, L, n_prev):
    xbc_ref, z_ref, sm_ref, conv0_ref, h0_ref, cw_ref, cb_ref, dtb_ref, alog_ref, dsk_ref, nw_ref = refs[:11]
    prev_ref = refs[11] if n_prev else None
    y_ref, hn_ref, convn_ref, xp, ysc = refs[11 + bool(n_prev):]
    K1 = SHORT_CONV - 1
    P, N = SSD_HEAD_DIM, SSD_STATE

    @pl.when(pl.program_id(1) == 0)
    def _():
        xp[:, HALO - K1:HALO, :] = conv0_ref[...]
        hn_ref[n_prev] = h0_ref[...]
        if n_prev:
            hn_ref[0:n_prev] = prev_ref[...]

    xp[:, HALO:HALO + L, :] = xbc_ref[...]
    xc = _causal_conv(xp, cw_ref, cb_ref, L)
    tail = xp[:, HALO + L - K1:HALO + L, :]
    convn_ref[...] = tail
    xp[:, HALO - K1:HALO, :] = tail
    head_lane = _iota((1, LANES), 1) < SSD_HEADS
    lower = _iota((L, L), 0) >= _iota((L, L), 1)
    J = SSD_HEADS // SSD_GROUPS
    for i in range(nb):
        xa = jax.nn.silu(xc[i])
        xs = xa[:, :SSD_INNER]
        bm = xa[:, SSD_INNER:SSD_INNER + SSD_GROUPS * N]
        cm = xa[:, SSD_INNER + SSD_GROUPS * N:]
        dt = jnp.where(head_lane, jax.nn.softplus(sm_ref[i] + dtb_ref[...]), 0.0)
        la = dt * -jnp.exp(alog_ref[...])
        acum = jnp.dot(lower.astype(F32), la, precision=HIGHEST, preferred_element_type=F32)
        acum_t = acum.T
        dt_t = dt.T
        xs_t = xs.T
        a_end = acum[L - 1:L, :]
        for g in range(SSD_GROUPS):
            Bg = bm[:, g * N:(g + 1) * N].astype(BF16)
            Cg = cm[:, g * N:(g + 1) * N].astype(BF16)
            CB = _bdot_nt(Cg, Bg)
            for j in range(J):
                h = g * J + j
                hs = slice(h * P, (h + 1) * P)
                col = acum[:, h:h + 1]
                row = acum_t[h:h + 1, :]
                decay = jnp.exp(jnp.where(lower, col - row, -jnp.inf))
                xh = xs[:, hs]
                y_diag = _bdot(CB * decay, xh * dt[:, h:h + 1])
                h_prev = hn_ref[n_prev, i, hs, :]
                y_off = _bdot_nt(Cg, h_prev) * jnp.exp(col)
                e_end = a_end[:, h:h + 1]
                xw_t = xs_t[hs, :] * (dt_t[h:h + 1, :] * jnp.exp(e_end - row))
                hn_ref[n_prev, i, hs, :] = jnp.exp(e_end) * h_prev + _bdot(xw_t, Bg)
                ysc[i, :, hs] = y_diag + y_off + dsk_ref[:, h:h + 1] * xh
        y = ysc[i] * jax.nn.silu(z_ref[i])
        y_ref[i] = y * lax.rsqrt(jnp.mean(y * y, -1, keepdims=True) + LN_EPS) * nw_ref[...]


def _ssd(xbc3, z3, sm3, conv0, conv_base, h0, h_base, cw, cb, dtb, alog, dsk, nw, prev, nb, L):
    B, T, _ = xbc3.shape
    K1 = SHORT_CONV - 1
    n_prev = 0 if prev is None else prev.shape[0]
    tokspec = lambda a: pl.BlockSpec((nb, L, a.shape[2]), lambda b, t: (b, t, 0))
    full = lambda a: pl.BlockSpec(a.shape, lambda b, t: (0,) * a.ndim)
    in_specs = [tokspec(xbc3), tokspec(z3), tokspec(sm3),
                pl.BlockSpec((nb, K1, SSD_CONV_DIM), lambda b, t: (conv_base // nb + b, 0, 0)),
                pl.BlockSpec((nb, SSD_INNER, SSD_STATE), lambda b, t: (h_base // nb + b, 0, 0)),
                full(cw), full(cb), full(dtb), full(alog), full(dsk), full(nw)]
    args = [xbc3, z3, sm3, conv0, h0, cw, cb, dtb, alog, dsk, nw]
    if n_prev:
        in_specs.append(pl.BlockSpec((n_prev, nb, SSD_INNER, SSD_STATE), lambda b, t: (0, b, 0, 0)))
        args.append(prev)
    return pl.pallas_call(
        functools.partial(_ssd_body, nb=nb, L=L, n_prev=n_prev),
        grid=(B // nb, T // L),
        in_specs=in_specs,
        out_specs=[tokspec(z3), pl.BlockSpec((n_prev + 1, nb, SSD_INNER, SSD_STATE), lambda b, t: (0, b, 0, 0)),
                   pl.BlockSpec((nb, K1, SSD_CONV_DIM), lambda b, t: (b, 0, 0))],
        out_shape=[jax.ShapeDtypeStruct(z3.shape, F32),
                   jax.ShapeDtypeStruct((n_prev + 1, B, SSD_INNER, SSD_STATE), F32),
                   jax.ShapeDtypeStruct((B, K1, SSD_CONV_DIM), F32)],
        scratch_shapes=[pltpu.VMEM((nb, HALO + L, SSD_CONV_DIM), F32), pltpu.VMEM((nb, L, SSD_INNER), F32)],
        compiler_params=_cparams("parallel", "arbitrary"),
        name="ssd",
    )(*args)


def _gla_body(*refs, nb, tc, l, n_prev):
    q_ref, k_ref, v_ref, gg_ref, sm_ref, s0_ref, wa_ref, ba_ref, nw_ref = refs[:9]
    prev_ref = refs[9] if n_prev else None
    o_ref, sn_ref, st = refs[9 + bool(n_prev):]
    K, V = GLA_DK, GLA_DV

    @pl.when(pl.program_id(1) == 0)
    def _():
        for i in range(nb):
            for h in range(GLA_HEADS):
                st[i, h * V:(h + 1) * V, :] = s0_ref[i, h * K:(h + 1) * K, :].T
        if n_prev:
            sn_ref[0:n_prev] = prev_ref[...]

    ri, ci = _iota((tc, tc), 0), _iota((tc, tc), 1)
    shift = int(np.log2(l))
    lower = (ri >= ci) & (jnp.right_shift(ri, shift) == jnp.right_shift(ci, shift))
    for i in range(nb):
        log_alpha = jax.nn.log_sigmoid(_bdot(sm_ref[i], wa_ref[...]) + ba_ref[...]) / GLA_TAU
        bc = jnp.dot(lower.astype(F32), log_alpha, precision=HIGHEST, preferred_element_type=F32)
        q = q_ref[i] * GLA_DK ** -0.5
        k = k_ref[i]
        v = v_ref[i]
        qe = q * jnp.exp(bc)
        ke = k * jnp.exp(-bc)
        for h in range(GLA_HEADS):
            cs = slice(h * K, (h + 1) * K)
            att = jnp.where(lower, _bdot_nt(qe[:, cs], ke[:, cs]), 0.0)
            o_h = _bdot(att, v[:, cs])
            subs = [slice(c * l, (c + 1) * l) for c in range(tc // l)]
            b_ends = [bc[rs.stop - 1:rs.stop, cs] for rs in subs]
            kvs = [_bdot_tn(v[rs, cs], k[rs, cs] * jnp.exp(b_end - bc[rs, cs])) for rs, b_end in zip(subs, b_ends)]
            states = [st[i, h * V:(h + 1) * V, :]]
            for b_end, kv in zip(b_ends, kvs):
                states.append(states[-1] * jnp.exp(b_end) + kv)
            inter = [_bdot_nt(qe[rs, cs], s_prev) for rs, s_prev in zip(subs, states)]
            s_t = states[-1]
            st[i, h * V:(h + 1) * V, :] = s_t
            sn_ref[n_prev, i, h * K:(h + 1) * K, :] = s_t.T
            o_h = o_h + (inter[0] if len(inter) == 1 else jnp.concatenate(inter, axis=0))
            o_h = o_h * lax.rsqrt(jnp.mean(o_h * o_h, -1, keepdims=True) + LN_EPS) * nw_ref[...]
            o_ref[i, :, cs] = o_h * jax.nn.silu(gg_ref[i, :, cs])


def _gla(gq3, gk3, gv3, gg3, sm3, s0, s_base, wa, ba, nw, prev, nb, tc):
    B, T, _ = gq3.shape
    l = min(GLA_CHUNK, T)
    n_prev = 0 if prev is None else prev.shape[0]
    tokspec = lambda a: pl.BlockSpec((nb, tc, a.shape[2]), lambda b, t: (b, t, 0))
    full = lambda a: pl.BlockSpec(a.shape, lambda b, t: (0,) * a.ndim)
    in_specs = [tokspec(gq3), tokspec(gk3), tokspec(gv3), tokspec(gg3), tokspec(sm3),
                pl.BlockSpec((nb, GLA_W, GLA_DV), lambda b, t: (s_base // nb + b, 0, 0)),
                full(wa), full(ba), full(nw)]
    args = [gq3, gk3, gv3, gg3, sm3, s0, wa, ba, nw]
    if n_prev:
        in_specs.append(pl.BlockSpec((n_prev, nb, GLA_W, GLA_DV), lambda b, t: (0, b, 0, 0)))
        args.append(prev)
    return pl.pallas_call(
        functools.partial(_gla_body, nb=nb, tc=tc, l=l, n_prev=n_prev),
        grid=(B // nb, T // tc),
        in_specs=in_specs,
        out_specs=[tokspec(gq3), pl.BlockSpec((n_prev + 1, nb, GLA_W, GLA_DV), lambda b, t: (0, b, 0, 0))],
        out_shape=[jax.ShapeDtypeStruct(gq3.shape, F32), jax.ShapeDtypeStruct((n_prev + 1, B, GLA_W, GLA_DV), F32)],
        scratch_shapes=[pltpu.VMEM((nb, GLA_HEADS * GLA_DV, GLA_DK), F32)],
        compiler_params=_cparams("parallel", "arbitrary"),
        name="gla",
    )(*args)


EVEN_SEGS = (
    (0, EV_Q, (True,) * 4),
    (EV_Q, 4 * LANES, (True, False, True, False)),
    (EV_Q + 4 * LANES, 2 * LANES, (True, False)),
    (EV_Q + EV_KV, LRU_WIDTH, None),
    (EV_Q + EV_KV + LRU_WIDTH, LRU_WIDTH, None),
    (EV_Q + EV_KV + 2 * LRU_WIDTH, LANES, None),
)
ODD_SEGS = (
    (0, SSD_INNER, None),
    (SSD_INNER, SSD_CONV_DIM, None),
    (SSD_INNER + SSD_CONV_DIM, GLA_W, None),
    (SSD_INNER + SSD_CONV_DIM + GLA_W, GLA_W, None),
    (SSD_INNER + SSD_CONV_DIM + 2 * GLA_W, GLA_W, None),
    (SSD_INNER + SSD_CONV_DIM + 3 * GLA_W, GLA_W, None),
    (SSD_INNER + SSD_CONV_DIM + 4 * GLA_W, LANES, None),
)


def _even_w_in(w):
    a = EV_Q + EV_KV
    pad = jnp.zeros((D_MODEL, LANES - EV_GATE), w.dtype)
    return jnp.concatenate([w[:, :a], w[:, a + EV_GATE:], w[:, a:a + EV_GATE], pad], axis=1).astype(BF16)


def _odd_w_in(w):
    a = SSD_INNER + SSD_CONV_DIM
    dt = w[:, a:a + SSD_HEADS]
    rest = w[:, a + SSD_HEADS:a + SSD_HEADS + 4 * GLA_W]
    ga = w[:, a + SSD_HEADS + 4 * GLA_W:]
    pad = jnp.zeros((D_MODEL, LANES - SSD_HEADS - GLA_RANK), w.dtype)
    return jnp.concatenate([w[:, :a], rest, dt, ga, pad], axis=1).astype(BF16)


def _lane_pad(v):
    return jnp.pad(v.astype(F32), (0, LANES - v.shape[0]))[None, :]


def _prep_layer(layer, P):
    j = layer // 2
    d = dict(
        ffn_wup=P['ffn_w_up_bf16'], ffn_cw=P['ffn_conv_w'][layer], ffn_cb=P['ffn_conv_b'][layer][None, :],
        ffn_wdn=P['ffn_w_down_bf16'],
        ln_mix_g=P['ln_mix_g'][layer][None, :], ln_mix_b=P['ln_mix_b'][layer][None, :],
        ln_ffn_g=P['ln_ffn_g'][layer][None, :], ln_ffn_b=P['ln_ffn_b'][layer][None, :])
    if layer % 2 == 0:
        w_out = P['w_out_even'][j].astype(BF16)
        wg = P['lru_w_gates'][j]
        eye = jnp.eye(LRU_BLOCKS, dtype=F32)
        wg = jnp.einsum('knde,nm->kndme', wg, eye).reshape(2, LRU_WIDTH, LRU_WIDTH)
        d.update(
            w_in=_even_w_in(P['w_in_even'][j]), w_out1=w_out[:EV_Q], w_out2=w_out[EV_Q:],
            cmp_tab=jnp.repeat(P['nsa_cmp_w'][j].reshape(CMP_BLOCK, 2 * NSA_KV_HEADS), HEAD_DIM, axis=1),
            cmp_page_tab=_page_compress_table(P['nsa_cmp_w'][j]),
            lru_cw=P['lru_conv_w'][j], lru_cb=P['lru_conv_b'][j][None, :],
            lru_wg=jnp.concatenate([wg[0], wg[1]], axis=1).astype(BF16),
            lru_bg=P['lru_b_gates'][j].reshape(1, 2 * LRU_WIDTH), lru_lam=P['lru_lambda'][j][None, :])
    else:
        w_out = P['w_out_odd'][j].astype(BF16)
        wa = jnp.zeros((LANES, GLA_W), F32).at[SSD_HEADS:SSD_HEADS + GLA_RANK].set(P['gla_w_alpha'][j])
        d.update(
            w_in=_odd_w_in(P['w_in_odd'][j]), w_out1=w_out[:SSD_INNER], w_out2=w_out[SSD_INNER:],
            ssd_cw=P['ssd_conv_w'][j], ssd_cb=P['ssd_conv_b'][j][None, :],
            ssd_dtb=_lane_pad(P['ssd_dt_bias'][j]), ssd_alog=_lane_pad(P['ssd_a_log'][j]),
            ssd_d=_lane_pad(P['ssd_d'][j]), ssd_nw=P['ssd_norm_w'][j][None, :],
            gla_wa=wa.astype(BF16), gla_ba=P['gla_b_alpha'][j][None, :], gla_nw=P['gla_norm_w'][j][None, :])
    return d


def _trunk(x3, prm, st, cfg):
    B, T, _ = x3.shape
    M = B * T
    out = dict(kv=[], win=[], lh=[], lc=[], sc=[], fc=[])
    ssd_states = gla_states = None
    for layer in range(DEPTH):
        p = prm[layer]
        j = layer // 2
        x2 = x3.reshape(M, D_MODEL)
        if layer % 2 == 0:
            q, rows, win, hx, hgate, hg = _proj(x2, p['w_in'], EVEN_SEGS, cfg['tm'], cfg['rope'], cfg['rope_blocks'])
            to3 = lambda a: a.reshape(B, T, a.shape[1])
            rows3, win3 = to3(rows), to3(win)
            if cfg['sample']:
                o_nsa = _nsa_sample(to3(q), to3(hg), rows3, win3, st['cache_t'], st['page_table'],
                                    j * st['n_phys'], st['cwin_t'], j * B, p['cmp_page_tab'], cfg['nsa_nb'])
                win_keep = win3
            else:
                H3 = _compress_prompt(rows3, p['cmp_tab'])
                o_nsa = _nsa_prompt(to3(q), to3(hg), H3, rows3, win3, cfg['tq'])
                win_keep = win3[:, T - min(WINDOW, T):]
            y_lru, h_last, conv_n = _lru(to3(hx), to3(hgate), st['lru_conv'], j * B, st['lru_h'], j * B,
                                         p['lru_cw'], p['lru_cb'], p['lru_wg'], p['lru_bg'], p['lru_lam'],
                                         cfg['lru_nb'], cfg['lru_tc'])
            a1, a2 = o_nsa, y_lru
            out['kv'].append(rows3.reshape(B, T, 4, NSA_KV_HEADS, HEAD_DIM))
            out['win'].append(win_keep.reshape(B, -1, 2, NSA_KV_HEADS, HEAD_DIM))
            out['lh'].append(h_last.reshape(B, LRU_WIDTH))
            out['lc'].append(conv_n)
        else:
            z, xbc, gq, gk, gv, gg, sm = _proj(x2, p['w_in'], ODD_SEGS, cfg['tm'])
            to3 = lambda a: a.reshape(B, T, a.shape[1])
            sm3 = to3(sm)
            y_ssd, ssd_states, conv_n = _ssd(
                to3(xbc), to3(z), sm3, st['ssd_conv'], j * B, st['ssd_h'], j * B, p['ssd_cw'], p['ssd_cb'],
                p['ssd_dtb'], p['ssd_alog'], p['ssd_d'], p['ssd_nw'], ssd_states, cfg['rec_nb'], cfg['ssd_L'])
            o_gla, gla_states = _gla(to3(gq), to3(gk), to3(gv), to3(gg), sm3, st['gla_s'], j * B,
                                     p['gla_wa'], p['gla_ba'], p['gla_nw'], gla_states, cfg['rec_nb'], cfg['gla_tc'])
            a1, a2 = y_ssd, o_gla
            out['sc'].append(conv_n)
        x3, fbuf = _mix_ffn(x3, a1, a2, p['w_out1'], p['w_out2'], p['ln_mix_g'], p['ln_mix_b'], layer,
                            p['ffn_wup'], p['ffn_cw'], p['ffn_cb'], p['ffn_wdn'], st['ffn_conv'], layer * B,
                            p['ln_ffn_g'], p['ln_ffn_b'], cfg['ffn_nb'], cfg['ffn_tc'], cfg['ffn_cw'])
        out['fc'].append(fbuf)
    out = {k: jnp.stack(v) for k, v in out.items()}
    out['sh'] = ssd_states.reshape(-1, B, SSD_HEADS, SSD_HEAD_DIM, SSD_STATE)
    out['gs'] = gla_states.reshape(-1, B, GLA_HEADS, GLA_DK, GLA_DV)
    return x3, out


def _largest_tile(n, cap):
    t = min(n, cap)
    while n % t:
        t //= 2
    return t


def kernel(x_prompt, x_sample, cache_nsa_kv, cache_nsa_win, state_lru_h, state_lru_conv, state_ssd, state_ssd_conv, state_gla, state_ffn_conv, page_table, w_in_even, w_out_even, nsa_cmp_w, lru_conv_w, lru_conv_b, lru_w_gates, lru_b_gates, lru_lambda, w_in_odd, w_out_odd, ssd_conv_w, ssd_conv_b, ssd_dt_bias, ssd_a_log, ssd_d, ssd_norm_w, gla_w_alpha, gla_b_alpha, gla_norm_w, ffn_w_up, ffn_conv_w, ffn_conv_b, ffn_w_down, ln_mix_g, ln_mix_b, ln_ffn_g, ln_ffn_b):
    P = dict(w_in_even=w_in_even, w_out_even=w_out_even, nsa_cmp_w=nsa_cmp_w, lru_conv_w=lru_conv_w,
             lru_conv_b=lru_conv_b, lru_w_gates=lru_w_gates, lru_b_gates=lru_b_gates, lru_lambda=lru_lambda,
             w_in_odd=w_in_odd, w_out_odd=w_out_odd, ssd_conv_w=ssd_conv_w, ssd_conv_b=ssd_conv_b,
             ssd_dt_bias=ssd_dt_bias, ssd_a_log=ssd_a_log, ssd_d=ssd_d, ssd_norm_w=ssd_norm_w,
             gla_w_alpha=gla_w_alpha, gla_b_alpha=gla_b_alpha, gla_norm_w=gla_norm_w, ffn_w_up=ffn_w_up,
             ffn_conv_w=ffn_conv_w, ffn_conv_b=ffn_conv_b, ffn_w_down=ffn_w_down, ln_mix_g=ln_mix_g,
             ln_mix_b=ln_mix_b, ln_ffn_g=ln_ffn_g, ln_ffn_b=ln_ffn_b)
    P['ffn_w_up_bf16'] = ffn_w_up.astype(BF16)
    P['ffn_w_down_bf16'] = ffn_w_down.astype(BF16)
    prm = [_prep_layer(layer, P) for layer in range(DEPTH)]
    n_even, n_odd = w_in_even.shape[0], w_in_odd.shape[0]
    ffn_buf = (FFN_CONV - 1, 2 * D_FF)

    Bp, Tp, _ = x_prompt.shape
    tm_p = _largest_tile(Tp, 512)
    st_p = dict(
        lru_conv=jnp.zeros((n_even * Bp, SHORT_CONV - 1, LRU_WIDTH), F32), lru_h=jnp.zeros((n_even * Bp, 1, LRU_WIDTH), F32),
        ssd_conv=jnp.zeros((n_odd * Bp, SHORT_CONV - 1, SSD_CONV_DIM), F32),
        ssd_h=jnp.zeros((n_odd * Bp, SSD_INNER, SSD_STATE), F32), gla_s=jnp.zeros((n_odd * Bp, GLA_W, GLA_DV), F32),
        ffn_conv=jnp.zeros((DEPTH * Bp,) + ffn_buf, F32))
    cfg_p = dict(sample=False, tm=tm_p, rope=_rope_tables(jnp.arange(Tp)), rope_blocks=Tp // tm_p,
                 tq=_largest_tile(Tp, 512), lru_nb=1, lru_tc=_largest_tile(Tp, 512),
                 rec_nb=1, ssd_L=_largest_tile(Tp, 512), gla_tc=_largest_tile(Tp, 256),
                 ffn_nb=1, ffn_tc=_largest_tile(Tp, 512), ffn_cw=D_FF // 2)
    y_p, o_p = _trunk(x_prompt, prm, st_p, cfg_p)

    Bs, Ts, _ = x_sample.shape
    n_phys = cache_nsa_kv.shape[1]
    past_len = page_table.shape[1] * PAGE_SIZE
    tm_s = _largest_tile(Bs * Ts, 256)
    pos_s = past_len + jnp.arange(tm_s) % Ts
    nb_s = _largest_tile(Bs, 32)
    st_s = dict(
        cache_t=cache_nsa_kv.transpose(0, 1, 3, 4, 5, 2).reshape(n_even * n_phys, 4 * LANES, PAGE_SIZE),
        n_phys=n_phys, page_table=page_table,
        cwin_t=cache_nsa_win.transpose(0, 1, 3, 4, 5, 2).reshape(n_even * Bs, 2 * LANES, cache_nsa_win.shape[2]),
        lru_conv=state_lru_conv.reshape(n_even * Bs, SHORT_CONV - 1, LRU_WIDTH),
        lru_h=state_lru_h.reshape(n_even * Bs, 1, LRU_WIDTH),
        ssd_conv=state_ssd_conv.reshape(n_odd * Bs, SHORT_CONV - 1, SSD_CONV_DIM),
        ssd_h=state_ssd.reshape(n_odd * Bs, SSD_INNER, SSD_STATE),
        gla_s=state_gla.reshape(n_odd * Bs, GLA_W, GLA_DV),
        ffn_conv=state_ffn_conv.reshape((DEPTH * Bs,) + ffn_buf))
    cfg_s = dict(sample=True, tm=tm_s, rope=_rope_tables(pos_s), rope_blocks=1, nsa_nb=_largest_tile(Bs, 4),
                 rec_nb=_largest_tile(Bs, 4),
                 lru_nb=nb_s, lru_tc=Ts, ssd_L=Ts, gla_tc=Ts, ffn_nb=nb_s, ffn_tc=Ts, ffn_cw=D_FF // 2)
    y_s, o_s = _trunk(x_sample, prm, st_s, cfg_s)
    win_t = _window_append(st_s['cwin_t'], o_s['win'].reshape(n_even * Bs, Ts, 2 * LANES),
                           _largest_tile(n_even * Bs, SUBLANES))
    win_s = win_t.reshape(n_even, Bs, 2, NSA_KV_HEADS, HEAD_DIM, -1).transpose(0, 1, 5, 2, 3, 4)

    return (y_p, y_s, o_p['kv'], o_s['kv'], o_p['win'], win_s, o_p['lh'], o_s['lh'], o_p['lc'], o_s['lc'],
            o_p['sh'], o_s['sh'], o_p['sc'], o_s['sc'], o_p['gs'], o_s['gs'], o_p['fc'], o_s['fc'])
```

```python
import functools

import jax
import jax.numpy as jnp
import numpy as np
from jax import lax
from jax.experimental import pallas as pl
from jax.experimental.pallas import tpu as pltpu

F32 = jnp.float32
BF16 = jnp.bfloat16
HIGHEST = lax.Precision.HIGHEST

D_MODEL = 1024
DEPTH = 4
PAGE_SIZE = 128
HEAD_DIM = 64
ROPE_DIM = HEAD_DIM // 4
ROPE_THETA = 500000.0
NSA_HEADS = 8
NSA_KV_HEADS = 2
NSA_GROUP = NSA_HEADS // NSA_KV_HEADS
CMP_BLOCK = 32
CMP_STRIDE = 16
SEL_BLOCK = 64
N_SEL = 8
WINDOW = 512
LRU_WIDTH = D_MODEL // 2
LRU_BLOCKS = 8
LRU_BLOCK_DIM = LRU_WIDTH // LRU_BLOCKS
LRU_C = 8.0
SHORT_CONV = 4
SSD_HEADS = 16
SSD_HEAD_DIM = 64
SSD_INNER = SSD_HEADS * SSD_HEAD_DIM
SSD_GROUPS = 2
SSD_STATE = 128
SSD_CONV_DIM = SSD_INNER + 2 * SSD_GROUPS * SSD_STATE
GLA_HEADS = 4
GLA_DK = 128
GLA_DV = 128
GLA_RANK = 16
GLA_TAU = 16.0
GLA_CHUNK = 32
D_FF = 2816
FFN_CONV = 3
ALPHA = (2.0 * DEPTH) ** 0.25
LN_EPS = 1e-5
NEG = -1e30
EV_Q = NSA_HEADS * HEAD_DIM
EV_KV = 6 * NSA_KV_HEADS * HEAD_DIM
EV_GATE = 3 * NSA_HEADS
GLA_W = GLA_HEADS * GLA_DK

LANES = 128
SUBLANES = 8
V7X_VMEM_BYTES = 64 * 1024 * 1024
VMEM_LIMIT = V7X_VMEM_BYTES - 8 * 1024 * 1024
HALO = SUBLANES


def _cparams(*sem):
    return pltpu.CompilerParams(dimension_semantics=sem, vmem_limit_bytes=VMEM_LIMIT)


def _resident(a):
    return pl.BlockSpec(a.shape, lambda *_: (0,) * a.ndim, pipeline_mode=pl.Buffered(1))


def _bdot(a, b):
    return jnp.dot(a.astype(BF16), b.astype(BF16), preferred_element_type=F32)


def _bdot_nt(a, b):
    return lax.dot_general(a.astype(BF16), b.astype(BF16), (((1,), (1,)), ((), ())), preferred_element_type=F32)


def _bdot_tn(a, b):
    return lax.dot_general(a.astype(BF16), b.astype(BF16), (((0,), (0,)), ((), ())), preferred_element_type=F32)


def _iota(shape, axis):
    return lax.broadcasted_iota(jnp.int32, shape, axis)


def _layer_norm(y, g, b):
    mu = jnp.mean(y, -1, keepdims=True)
    d = y - mu
    var = jnp.mean(d * d, -1, keepdims=True)
    return d * lax.rsqrt(var + LN_EPS) * g + b


def _masked_softmax(s, mask):
    s = jnp.where(mask, s, NEG)
    m = jnp.max(s, -1, keepdims=True)
    e = jnp.exp(s - m)
    p = e / jnp.sum(e, -1, keepdims=True)
    return jnp.where(mask, p, 0.0)


def _proj_body(*refs, segs, has_rope):
    if has_rope:
        x_ref, w_ref, c_ref, s1_ref, s2_ref = refs[:5]
        out_refs = refs[5:]
    else:
        x_ref, w_ref = refs[:2]
        out_refs = refs[2:]
    xb = x_ref[...].astype(BF16)
    for o_ref, (start, width, rope) in zip(out_refs, segs):
        acc = jnp.dot(xb, w_ref[:, start:start + width], preferred_element_type=F32)
        if rope is None:
            o_ref[...] = acc
            continue
        for c, flag in enumerate(rope):
            chunk = acc[:, c * LANES:(c + 1) * LANES]
            if flag:
                chunk = (chunk * c_ref[...] + pltpu.roll(chunk, LANES - ROPE_DIM // 2, 1) * s2_ref[...]
                         + pltpu.roll(chunk, ROPE_DIM // 2, 1) * s1_ref[...])
            o_ref[:, c * LANES:(c + 1) * LANES] = chunk


def _proj(x2d, w, segs, tm, tabs=None, tab_blocks=1):
    M, K = x2d.shape
    N = w.shape[1]
    has_rope = tabs is not None
    in_specs = [pl.BlockSpec((tm, K), lambda i: (i, 0)), _resident(w)]
    args = [x2d, w]
    if has_rope:
        in_specs += [pl.BlockSpec((tm, LANES), lambda i: (i % tab_blocks, 0))] * 3
        args += list(tabs)
    return pl.pallas_call(
        functools.partial(_proj_body, segs=segs, has_rope=has_rope),
        grid=(M // tm,),
        in_specs=in_specs,
        out_specs=[pl.BlockSpec((tm, wd), lambda i: (i, 0)) for _, wd, _ in segs],
        out_shape=[jax.ShapeDtypeStruct((M, wd), F32) for _, wd, _ in segs],
        compiler_params=_cparams("parallel"),
        name="proj",
    )(*args)


def _rope_tables(pos):
    half = ROPE_DIM // 2
    inv = jnp.power(ROPE_THETA, -2.0 * jnp.arange(half, dtype=F32) / ROPE_DIM)
    ang = pos.astype(F32)[:, None] * inv[None, :]
    cos, sin = jnp.cos(ang), jnp.sin(ang)
    R = pos.shape[0]
    rest = HEAD_DIM - ROPE_DIM
    c64 = jnp.concatenate([cos, cos, jnp.ones((R, rest), F32)], 1)
    s1 = jnp.concatenate([jnp.zeros((R, half), F32), sin, jnp.zeros((R, rest), F32)], 1)
    s2 = jnp.concatenate([-sin, jnp.zeros((R, half + rest), F32)], 1)
    return tuple(jnp.concatenate([t, t], 1) for t in (c64, s1, s2))


def _compress_rows(x, w_ref):
    x3 = x.reshape(x.shape[0] // CMP_STRIDE, CMP_STRIDE, 2 * LANES)
    return (jnp.sum(x3 * w_ref[0:CMP_STRIDE, :][None], axis=1),
            jnp.sum(x3 * w_ref[CMP_STRIDE:CMP_BLOCK, :][None], axis=1))


def _compress_body(kv_ref, w_ref, o_ref):
    h0, h1 = _compress_rows(kv_ref[0], w_ref)
    o_ref[0, :, 0:2 * LANES] = h0
    o_ref[0, :, 2 * LANES:4 * LANES] = h1


def _compress_prompt(rows3, wtab):
    B, T, _ = rows3.shape
    return pl.pallas_call(
        _compress_body,
        grid=(B,),
        in_specs=[pl.BlockSpec((1, T, 2 * LANES), lambda b: (b, 0, 0)),
                  pl.BlockSpec((CMP_BLOCK, 2 * LANES), lambda b: (0, 0))],
        out_specs=pl.BlockSpec((1, T // CMP_STRIDE, 4 * LANES), lambda b: (b, 0, 0)),
        out_shape=jax.ShapeDtypeStruct((B, T // CMP_STRIDE, 4 * LANES), F32),
        compiler_params=_cparams("parallel"),
        name="compress_prompt",
    )(rows3, wtab)


def _overlap_matrix(n_rows, n_cmp, n_sel):
    s1 = np.arange(n_rows)[:, None] * CMP_STRIDE
    s2 = np.arange(LANES)[None, :] * SEL_BLOCK
    ov = np.clip(np.minimum(s1 + CMP_BLOCK, s2 + SEL_BLOCK) - np.maximum(s1, s2), 0, None) / CMP_BLOCK
    ov = ov * (np.arange(n_rows)[:, None] < n_cmp) * (np.arange(LANES)[None, :] < n_sel)
    return jnp.asarray(ov, dtype=F32)


def _select_blocks(psum, ov, t_col, n_sel):
    imp = jnp.dot(psum, ov, precision=HIGHEST, preferred_element_type=F32)
    blk = _iota(imp.shape, 1)
    cur = jnp.right_shift(t_col, int(np.log2(SEL_BLOCK)))
    future = blk * SEL_BLOCK > t_col
    forced = (blk == 0) | (blk == cur) | (blk == cur - 1)
    w = jnp.where(future, -1.0, jnp.where(forced, 1e6, imp))
    w = jnp.where(blk < n_sel, w, -jnp.inf)
    n_pad = -(-n_sel // SUBLANES) * SUBLANES
    wt = w.T[0:n_pad, :]
    sub = _iota((n_pad, 1), 0)
    rank = jnp.zeros(wt.shape, F32)
    for i in range(n_sel):
        wi = wt[i:i + 1, :]
        rank = rank + jnp.where((wi > wt) | ((wi == wt) & (sub > i)), 1.0, 0.0)
    sel_t = jnp.where((rank < min(N_SEL, n_sel)) & (sub < n_sel), 1.0, 0.0)
    if n_pad < LANES:
        sel_t = jnp.concatenate([sel_t, jnp.zeros((LANES - n_pad, sel_t.shape[1]), F32)], axis=0)
    return sel_t.T


def _stack_heads(q, h):
    G = NSA_GROUP
    qh = jnp.concatenate([q[:, (h * G + g) * HEAD_DIM:(h * G + g + 1) * HEAD_DIM] for g in range(G)], axis=0)
    return (qh * HEAD_DIM ** -0.5).astype(BF16)


def _expand_matrix(K):
    blk = jnp.right_shift(_iota((LANES, K), 1), int(np.log2(SEL_BLOCK)))
    return (blk == _iota((LANES, K), 0)).astype(BF16)


def _attend(qb, chunks, mask, tq):
    G = NSA_GROUP
    dh = HEAD_DIM
    bias = jnp.where(mask, 0.0, NEG)
    ks = [k.astype(BF16) for _, k, _ in chunks]
    vs = [jnp.concatenate([v.astype(BF16), jnp.ones(v.shape, BF16)], axis=0 if tr else 1) for tr, _, v in chunks]
    reps = G if tq >= LANES else 1
    rows = G * tq // reps
    outs = []
    for r in range(reps):
        q_r = qb[r * rows:(r + 1) * rows]
        s = [jnp.dot(q_r, k, preferred_element_type=F32) if tr else
             lax.dot_general(q_r, k, (((1,), (1,)), ((), ())), preferred_element_type=F32)
             for (tr, _, _), k in zip(chunks, ks)]
        s = s[0] if len(s) == 1 else jnp.concatenate(s, axis=-1)
        K = s.shape[-1]
        s = (s.reshape(rows // tq, tq, K) + bias[None]).reshape(rows, K)
        e = jnp.exp(s - jnp.max(s, -1, keepdims=True)).astype(BF16)
        ox, off = None, 0
        for (tr, _, _), v in zip(chunks, vs):
            kc = v.shape[1] if tr else v.shape[0]
            part = (lax.dot_general(e[:, off:off + kc], v, (((1,), (1,)), ((), ())), preferred_element_type=F32)
                    if tr else jnp.dot(e[:, off:off + kc], v, preferred_element_type=F32))
            ox = part if ox is None else ox + part
            off += kc
        outs.append(ox[:, :dh] / ox[:, dh:dh + 1])
    return outs[0] if reps == 1 else jnp.concatenate(outs, axis=0)


def _gate_store(o_ref, sg, h, tq, o_c, o_s, o_w):
    for g in range(NSA_GROUP):
        hh = h * NSA_GROUP + g
        r = slice(g * tq, (g + 1) * tq)
        o_ref[:, hh * HEAD_DIM:(hh + 1) * HEAD_DIM] = (
            sg[:, 3 * hh:3 * hh + 1] * o_c[r] + sg[:, 3 * hh + 1:3 * hh + 2] * o_s[r]
            + sg[:, 3 * hh + 2:3 * hh + 3] * o_w[r])


def _nsa_core(items, ov, t_col, n_sel):
    tq = t_col.shape[0]
    G = NSA_GROUP
    o_cs, psums = [], []
    for it in items:
        n_rows = it['ckk'].shape[0]
        p_c = _masked_softmax(_bdot_nt(it['qb'], it['ckk']).reshape(G, tq, n_rows), it['mask_c'][None])
        o_cs.append(_bdot(p_c.reshape(G * tq, n_rows), it['ckv']))
        psum = p_c[0]
        for g in range(1, G):
            psum = psum + p_c[g]
        psums.append(psum)
    o_ws = [it['win'](it['qb']) for it in items]
    n = len(items)
    sel = _select_blocks(jnp.concatenate(psums, axis=0), ov, jnp.concatenate([t_col] * n, axis=0), n_sel)
    for i, it in enumerate(items):
        o_s = it['slc'](it['qb'], sel[i * tq:(i + 1) * tq])
        it['finish'](o_cs[i], o_s, o_ws[i])


def _combine_compressed(h0, h1):
    return h0 + pltpu.roll(h1, h1.shape[0] - 1, 0)


def _nsa_prompt_body(q_ref, hg_ref, H_ref, rows_ref, win_ref, ov_ref, o_ref, *, T, tq):
    qi = pl.program_id(1)
    n_rows = T // CMP_STRIDE
    n_cmp = n_rows - CMP_BLOCK // CMP_STRIDE + 1
    n_sel = T // SEL_BLOCK
    t_col = qi * tq + _iota((tq, 1), 0)
    ck = _combine_compressed(H_ref[0, :, 0:2 * LANES], H_ref[0, :, 2 * LANES:4 * LANES])
    ncol = _iota((1, n_rows), 1)
    mask_c = (ncol * CMP_STRIDE + CMP_BLOCK - 1 <= t_col) & (ncol < n_cmp)
    band = WINDOW + tq
    wstart = pl.multiple_of(jnp.clip(qi * tq - WINDOW, 0, T - band), tq)
    wpos = wstart + _iota((1, band), 1)
    mask_w = (wpos <= t_col) & (wpos > t_col - WINDOW)
    q = q_ref[0]
    sg = jax.nn.sigmoid(hg_ref[0])
    kstep = min(T, 4 * LANES)

    def make_item(h):
        def slc_span(K):
            def run(qb, sel):
                allowed = jnp.dot(sel.astype(BF16), _expand_matrix(K), preferred_element_type=F32) > 0.5
                mask = allowed & (_iota((1, K), 1) <= t_col)
                k = rows_ref[0, 0:K, 2 * LANES + h * HEAD_DIM:2 * LANES + (h + 1) * HEAD_DIM]
                v = rows_ref[0, 0:K, 3 * LANES + h * HEAD_DIM:3 * LANES + (h + 1) * HEAD_DIM]
                return _attend(qb, [(False, k, v)], mask, tq)
            return run

        def slc(qb, sel):
            spans = [slc_span((i + 1) * kstep) for i in range(T // kstep)]
            if len(spans) == 1:
                return spans[0](qb, sel)
            return lax.switch(lax.div(qi * tq + (tq - 1), kstep), spans, qb, sel)

        def win(qb):
            k = win_ref[0, pl.ds(wstart, band), h * HEAD_DIM:(h + 1) * HEAD_DIM]
            v = win_ref[0, pl.ds(wstart, band), LANES + h * HEAD_DIM:LANES + (h + 1) * HEAD_DIM]
            return _attend(qb, [(False, k, v)], mask_w, tq)

        return dict(qb=_stack_heads(q, h), ckk=ck[:, h * HEAD_DIM:(h + 1) * HEAD_DIM],
                    ckv=ck[:, LANES + h * HEAD_DIM:LANES + (h + 1) * HEAD_DIM], mask_c=mask_c,
                    slc=slc, win=win, finish=functools.partial(_gate_store, o_ref.at[0], sg, h, tq))

    _nsa_core([make_item(h) for h in range(NSA_KV_HEADS)], ov_ref[...], t_col, n_sel)


def _nsa_prompt(q3, hg3, H3, rows3, win3, tq):
    B, T, _ = q3.shape
    n_rows = T // CMP_STRIDE
    ov = _overlap_matrix(n_rows, n_rows - 1, T // SEL_BLOCK)
    return pl.pallas_call(
        functools.partial(_nsa_prompt_body, T=T, tq=tq),
        grid=(B, T // tq),
        in_specs=[pl.BlockSpec((1, tq, EV_Q), lambda b, i: (b, i, 0)),
                  pl.BlockSpec((1, tq, LANES), lambda b, i: (b, i, 0)),
                  pl.BlockSpec((1, n_rows, 4 * LANES), lambda b, i: (b, 0, 0)),
                  pl.BlockSpec((1, T, 4 * LANES), lambda b, i: (b, 0, 0)),
                  pl.BlockSpec((1, T, 2 * LANES), lambda b, i: (b, 0, 0)),
                  pl.BlockSpec((n_rows, LANES), lambda b, i: (0, 0))],
        out_specs=pl.BlockSpec((1, tq, EV_Q), lambda b, i: (b, i, 0)),
        out_shape=jax.ShapeDtypeStruct((B, T, EV_Q), F32),
        compiler_params=_cparams("parallel", "arbitrary"),
        name="nsa_prompt",
    )(q3, hg3, H3, rows3, win3, ov)


def _nsa_sample_body(*refs, nb, n_pages, Tq, Wb):
    q_ref, hg_ref, rnew_ref, wnew_ref, ctab_ref, ov_ref = refs[1:7]
    page_refs = refs[7:7 + nb * n_pages]
    cwin_ref, o_ref, knew, wnew = refs[7 + nb * n_pages:]
    P = n_pages * PAGE_SIZE
    Ks = P + PAGE_SIZE
    Kw = Wb + PAGE_SIZE
    n_rows = P // CMP_STRIDE
    n_cmp = n_rows - CMP_BLOCK // CMP_STRIDE + 1
    n_sel = -(-(P + Tq) // SEL_BLOCK)
    tt = _iota((Tq, 1), 0)
    t_col = P + tt
    mask_c = (_iota((1, n_rows), 1) < n_cmp) & (tt >= 0)
    causal = _iota((1, Ks), 1) <= t_col
    wpos = _iota((1, Kw), 1)
    mask_w = (wpos <= Wb + tt) & (wpos > Wb + tt - WINDOW)
    expand = _expand_matrix(Ks)
    pad_rows = jnp.zeros((PAGE_SIZE - Tq, 2 * LANES), F32)

    n_grp = 2 * NSA_KV_HEADS
    per_page = PAGE_SIZE // CMP_STRIDE

    def make_item(i, h, cks, sg, pages):
        ksl = slice(h * HEAD_DIM, (h + 1) * HEAD_DIM)
        vsl = slice(LANES + h * HEAD_DIM, LANES + (h + 1) * HEAD_DIM)

        def slc(qb, sel):
            mask = (jnp.dot(sel.astype(BF16), expand, preferred_element_type=F32) > 0.5) & causal
            chunks = [(True, pr[0, 2 * LANES + ksl.start:2 * LANES + ksl.stop, :],
                       pr[0, 2 * LANES + vsl.start:2 * LANES + vsl.stop, :]) for pr in pages]
            chunks.append((False, knew[i, :, ksl], knew[i, :, vsl]))
            return _attend(qb, chunks, mask, Tq)

        def win(qb):
            chunks = [(True, cwin_ref[i, ksl, :], cwin_ref[i, vsl, :]), (False, wnew[i, :, ksl], wnew[i, :, vsl])]
            return _attend(qb, chunks, mask_w, Tq)

        return dict(qb=_stack_heads(q_ref[i], h), ckk=cks[h], ckv=cks[NSA_KV_HEADS + h], mask_c=mask_c,
                    slc=slc, win=win, finish=functools.partial(_gate_store, o_ref.at[i], sg, h, Tq))

    items = []
    for i in range(nb):
        knew[i, 0:Tq, :] = rnew_ref[i, :, 2 * LANES:4 * LANES]
        knew[i, Tq:PAGE_SIZE, :] = pad_rows
        wnew[i, 0:Tq, :] = wnew_ref[i]
        wnew[i, Tq:PAGE_SIZE, :] = pad_rows
        pages = page_refs[i * n_pages:(i + 1) * n_pages]
        parts = [_bdot_nt(ctab_ref[...], pr[0, 0:2 * LANES, :]) for pr in pages]
        cks = []
        for g in range(n_grp):
            r0, cs = g * 2 * per_page, slice(g * HEAD_DIM, (g + 1) * HEAD_DIM)
            cks.append(_combine_compressed(
                jnp.concatenate([pp[r0:r0 + per_page, cs] for pp in parts], axis=0),
                jnp.concatenate([pp[r0 + per_page:r0 + 2 * per_page, cs] for pp in parts], axis=0)))
        sg = jax.nn.sigmoid(hg_ref[i])
        items += [make_item(i, h, cks, sg, pages) for h in range(NSA_KV_HEADS)]
    _nsa_core(items, ov_ref[...], t_col, n_sel)


def _page_compress_table(cmp_w):
    per_page = PAGE_SIZE // CMP_STRIDE
    w = cmp_w.reshape(CMP_BLOCK, 2 * NSA_KV_HEADS).T
    s = jnp.arange(PAGE_SIZE)[None, :] - CMP_STRIDE * jnp.arange(per_page)[:, None]
    inside = (s >= 0) & (s < CMP_STRIDE)
    sc = jnp.clip(s, 0, CMP_STRIDE - 1)
    first = jnp.where(inside[None], w[:, sc], 0.0)
    second = jnp.where(inside[None], w[:, CMP_STRIDE + sc], 0.0)
    return jnp.concatenate([first, second], axis=1).reshape(-1, PAGE_SIZE).astype(BF16)


def _nsa_sample(q3, hg3, rows3, win3, cache_t, page_table, page_base, cwin_t, cwin_base, ctab, nb):
    B, Tq, _ = q3.shape
    n_pages = page_table.shape[1]
    P = n_pages * PAGE_SIZE
    Wb = cwin_t.shape[2]
    assert Wb % LANES == 0 and Wb > LANES and Tq <= SUBLANES and B % nb == 0
    n_rows = P // CMP_STRIDE
    n_sel = -(-(P + Tq) // SEL_BLOCK)
    ov = _overlap_matrix(n_rows, n_rows - 1, n_sel)
    cb0 = cwin_base // nb

    def page_spec(i, p):
        return pl.BlockSpec((1, 4 * LANES, PAGE_SIZE), lambda b, pt: (page_base + pt[b * nb + i, p], 0, 0))

    tok = lambda a: pl.BlockSpec((nb, Tq, a.shape[2]), lambda b, pt: (b, 0, 0))
    grid_spec = pltpu.PrefetchScalarGridSpec(
        num_scalar_prefetch=1,
        grid=(B // nb,),
        in_specs=[tok(q3), tok(hg3), tok(rows3), tok(win3),
                  pl.BlockSpec(ctab.shape, lambda b, pt: (0, 0)), pl.BlockSpec(ov.shape, lambda b, pt: (0, 0))]
        + [page_spec(i, p) for i in range(nb) for p in range(n_pages)]
        + [pl.BlockSpec((nb, 2 * LANES, Wb), lambda b, pt: (cb0 + b, 0, 0))],
        out_specs=tok(q3),
        scratch_shapes=[pltpu.VMEM((nb, PAGE_SIZE, 2 * LANES), F32), pltpu.VMEM((nb, PAGE_SIZE, 2 * LANES), F32)],
    )
    return pl.pallas_call(
        functools.partial(_nsa_sample_body, nb=nb, n_pages=n_pages, Tq=Tq, Wb=Wb),
        grid_spec=grid_spec,
        out_shape=jax.ShapeDtypeStruct((B, Tq, EV_Q), F32),
        compiler_params=_cparams("arbitrary"),
        name="nsa_sample",
    )(page_table, q3, hg3, rows3, win3, ctab, ov, *([cache_t] * (nb * n_pages)), cwin_t)


def _window_append_body(cwin_ref, new_ref, o_ref, pad, *, nb, Tq, Wb):
    new_lanes = _iota((1, LANES), 1) >= LANES - Tq
    pad[Tq:LANES, :] = jnp.zeros((LANES - Tq, pad.shape[1]), F32)
    for i in range(nb):
        pad[0:Tq, :] = new_ref[i]
        new_t = pltpu.roll(pad[...].T, LANES - Tq, 1)
        shifted = pltpu.roll(cwin_ref[i], Wb - Tq, 1)
        o_ref[i, :, 0:Wb - LANES] = shifted[:, 0:Wb - LANES]
        o_ref[i, :, Wb - LANES:Wb] = jnp.where(new_lanes, new_t, shifted[:, Wb - LANES:Wb])


def _window_append(cwin_t, new_rows, nb):
    S, C, Wb = cwin_t.shape
    Tq = new_rows.shape[1]
    assert Wb % LANES == 0 and Wb > LANES and Tq <= LANES and S % nb == 0
    return pl.pallas_call(
        functools.partial(_window_append_body, nb=nb, Tq=Tq, Wb=Wb),
        grid=(S // nb,),
        in_specs=[pl.BlockSpec((nb, C, Wb), lambda s: (s, 0, 0)), pl.BlockSpec((nb, Tq, C), lambda s: (s, 0, 0))],
        out_specs=pl.BlockSpec((nb, C, Wb), lambda s: (s, 0, 0)),
        out_shape=jax.ShapeDtypeStruct(cwin_t.shape, F32),
        scratch_shapes=[pltpu.VMEM((LANES, C), F32)],
        compiler_params=_cparams("parallel"),
        name="window_append",
    )(cwin_t, new_rows)


def _causal_conv(xp, w_ref, b_ref, tc, cols=None):
    K = w_ref.shape[0]
    cs = slice(None) if cols is None else cols
    acc = None
    for j in range(K):
        term = w_ref[j:j + 1, cs][None] * xp[:, HALO - (K - 1) + j:HALO - (K - 1) + j + tc, :]
        acc = term if acc is None else acc + term
    return b_ref[:, cs][None] + acc


def _lru_body(hx_ref, hgate_ref, conv0_ref, h0_ref, cw_ref, cb_ref, wg_ref, bg_ref, lam_ref,
              y_ref, hlast_ref, convn_ref, xp, *, nb, tc):
    K1 = SHORT_CONV - 1

    @pl.when(pl.program_id(1) == 0)
    def _():
        xp[:, HALO - K1:HALO, :] = conv0_ref[...]
        hlast_ref[...] = h0_ref[...]

    xp[:, HALO:HALO + tc, :] = hx_ref[...]
    xc = _causal_conv(xp, cw_ref, cb_ref, tc)
    tail = xp[:, HALO + tc - K1:HALO + tc, :]
    convn_ref[...] = tail
    xp[:, HALO - K1:HALO, :] = tail
    R = nb * tc
    xc2 = xc.reshape(R, LRU_WIDTH)
    gt = _bdot(xc2, wg_ref[...]) + bg_ref[...]
    r_gate = jax.nn.sigmoid(gt[:, :LRU_WIDTH])
    i_gate = jax.nn.sigmoid(gt[:, LRU_WIDTH:])
    log_a = -LRU_C * r_gate * jax.nn.softplus(-lam_ref[...])
    a = jnp.exp(log_a)
    th = jnp.tanh(log_a)
    u = jnp.sqrt(-2.0 * th / (1.0 - th)) * i_gate * xc2
    tpos = lax.rem(_iota((R, 1), 0), tc)
    d = 1
    while d < tc:
        valid = tpos >= d
        u = jnp.where(valid, a * pltpu.roll(u, d, 0) + u, u)
        a = jnp.where(valid, a * pltpu.roll(a, d, 0), a)
        d *= 2
    hprev = jnp.broadcast_to(hlast_ref[...], (nb, tc, LRU_WIDTH)).reshape(R, LRU_WIDTH)
    h = a * hprev + u
    y_ref[...] = (h * jax.nn.gelu(hgate_ref[...].reshape(R, LRU_WIDTH))).reshape(nb, tc, LRU_WIDTH)
    last = _iota((1, tc, 1), 1) == tc - 1
    hlast_ref[...] = jnp.sum(jnp.where(last, h.reshape(nb, tc, LRU_WIDTH), 0.0), axis=1, keepdims=True)


def _lru(hx3, hgate3, conv0, conv_base, h0, h_base, cw, cb, wg, bg, lam, nb, tc):
    B, T, C = hx3.shape
    K1 = SHORT_CONV - 1
    cb0 = conv_base // nb
    hb0 = h_base // nb
    tok = pl.BlockSpec((nb, tc, C), lambda b, t: (b, t, 0))
    full = lambda a: pl.BlockSpec(a.shape, lambda b, t: (0,) * a.ndim)
    return pl.pallas_call(
        functools.partial(_lru_body, nb=nb, tc=tc),
        grid=(B // nb, T // tc),
        in_specs=[tok, tok,
                  pl.BlockSpec((nb, K1, C), lambda b, t: (cb0 + b, 0, 0)),
                  pl.BlockSpec((nb, 1, C), lambda b, t: (hb0 + b, 0, 0)),
                  full(cw), full(cb), full(wg), full(bg), full(lam)],
        out_specs=[tok, pl.BlockSpec((nb, 1, C), lambda b, t: (b, 0, 0)),
                   pl.BlockSpec((nb, K1, C), lambda b, t: (b, 0, 0))],
        out_shape=[jax.ShapeDtypeStruct((B, T, C), F32), jax.ShapeDtypeStruct((B, 1, C), F32),
                   jax.ShapeDtypeStruct((B, K1, C), F32)],
        scratch_shapes=[pltpu.VMEM((nb, HALO + tc, C), F32)],
        compiler_params=_cparams("parallel", "arbitrary"),
        name="lru",
    )(hx3, hgate3, conv0, h0, cw, cb, wg, bg, lam)


def _ffn_body(x_ref, a1_ref, a2_ref, w1_ref, w2_ref, gm_ref, bm_ref, wup_ref, cw_ref, cb_ref, wdn_ref, buf0_ref,
              g_ref, b_ref, y_ref, bufn_ref, sg, sv, *, nb, tc, cw):
    K1 = FFN_CONV - 1

    @pl.when(pl.program_id(1) == 0)
    def _():
        bufn_ref[...] = buf0_ref[...]

    R = nb * tc
    mix = (_bdot(a1_ref[...].reshape(R, a1_ref.shape[2]), w1_ref[...])
           + _bdot(a2_ref[...].reshape(R, a2_ref.shape[2]), w2_ref[...]))
    x = _layer_norm(ALPHA * x_ref[...].reshape(R, D_MODEL) + mix, gm_ref[...], bm_ref[...])
    xb = x.astype(BF16)
    acc = jnp.zeros((R, D_MODEL), F32)
    for c in range(D_FF // cw):
        conv = []
        for half, scr in ((0, sg), (1, sv)):
            cols = slice(half * D_FF + c * cw, half * D_FF + (c + 1) * cw)
            u = jnp.dot(xb, wup_ref[:, cols], preferred_element_type=F32)
            scr[:, HALO:HALO + tc, :] = u.reshape(nb, tc, cw)
            scr[:, HALO - K1:HALO, :] = bufn_ref[:, :, cols]
            conv.append(_causal_conv(scr, cw_ref, cb_ref, tc, cols).reshape(R, cw))
            bufn_ref[:, :, cols] = scr[:, HALO + tc - K1:HALO + tc, :]
        act = jax.nn.gelu(conv[0]) * conv[1]
        acc = acc + jnp.dot(act.astype(BF16), wdn_ref[c * cw:(c + 1) * cw, :], preferred_element_type=F32)
    y = _layer_norm(ALPHA * x + acc, g_ref[...], b_ref[...])
    y_ref[...] = y.reshape(nb, tc, D_MODEL)


def _mix_ffn(x3, a1, a2, w1, w2, gm, bm, layer, wup, cw, cb, wdn, buf0, buf_base, g, b, nb, tc, cwid):
    B, T, _ = x3.shape
    U, K1 = 2 * D_FF, FFN_CONV - 1
    bb0 = buf_base // nb
    tokspec = lambda a: pl.BlockSpec((nb, tc, a.shape[2]), lambda bi, t: (bi, t, 0))
    tok = tokspec(x3)
    full = _resident
    of_layer = lambda a: pl.BlockSpec((None,) + a.shape[1:], lambda *_: (layer, 0, 0), pipeline_mode=pl.Buffered(1))
    return pl.pallas_call(
        functools.partial(_ffn_body, nb=nb, tc=tc, cw=cwid),
        grid=(B // nb, T // tc),
        in_specs=[tok, tokspec(a1), tokspec(a2), full(w1), full(w2), full(gm), full(bm),
                  of_layer(wup), full(cw), full(cb), of_layer(wdn),
                  pl.BlockSpec((nb, K1, U), lambda bi, t: (bb0 + bi, 0, 0)), full(g), full(b)],
        out_specs=[tok, pl.BlockSpec((nb, K1, U), lambda bi, t: (bi, 0, 0))],
        out_shape=[jax.ShapeDtypeStruct(x3.shape, F32), jax.ShapeDtypeStruct((B, K1, U), F32)],
        scratch_shapes=[pltpu.VMEM((nb, HALO + tc, cwid), F32), pltpu.VMEM((nb, HALO + tc, cwid), F32)],
        compiler_params=_cparams("parallel", "arbitrary"),
        name="mix_ffn",
    )(x3, a1, a2, w1, w2, gm, bm, wup, cw, cb, wdn, buf0, g, b)


def _ssd_body(*refs, nb, L, n_prev):
    xbc_ref, z_ref, sm_ref, conv0_ref, h0_ref, cw_ref, cb_ref, dtb_ref, alog_ref, dsk_ref, nw_ref = refs[:11]
    prev_ref = refs[11] if n_prev else None
    y_ref, hn_ref, convn_ref, xp, ysc = refs[11 + bool(n_prev):]
    K1 = SHORT_CONV - 1
    P, N = SSD_HEAD_DIM, SSD_STATE

    @pl.when(pl.program_id(1) == 0)
    def _():
        xp[:, HALO - K1:HALO, :] = conv0_ref[...]
        hn_ref[n_prev] = h0_ref[...]
        if n_prev:
            hn_ref[0:n_prev] = prev_ref[...]

    xp[:, HALO:HALO + L, :] = xbc_ref[...]
    xc = _causal_conv(xp, cw_ref, cb_ref, L)
    tail = xp[:, HALO + L - K1:HALO + L, :]
    convn_ref[...] = tail
    xp[:, HALO - K1:HALO, :] = tail
    head_lane = _iota((1, LANES), 1) < SSD_HEADS
    lower = _iota((L, L), 0) >= _iota((L, L), 1)
    J = SSD_HEADS // SSD_GROUPS
    for i in range(nb):
        xa = jax.nn.silu(xc[i])
        xs = xa[:, :SSD_INNER]
        bm = xa[:, SSD_INNER:SSD_INNER + SSD_GROUPS * N]
        cm = xa[:, SSD_INNER + SSD_GROUPS * N:]
        dt = jnp.where(head_lane, jax.nn.softplus(sm_ref[i] + dtb_ref[...]), 0.0)
        la = dt * -jnp.exp(alog_ref[...])
        acum = jnp.dot(lower.astype(F32), la, precision=HIGHEST, preferred_element_type=F32)
        acum_t = acum.T
        dt_t = dt.T
        xs_t = xs.T
        a_end = acum[L - 1:L, :]
        for g in range(SSD_GROUPS):
            Bg = bm[:, g * N:(g + 1) * N].astype(BF16)
            Cg = cm[:, g * N:(g + 1) * N].astype(BF16)
            CB = _bdot_nt(Cg, Bg)
            for j in range(J):
                h = g * J + j
                hs = slice(h * P, (h + 1) * P)
                col = acum[:, h:h + 1]
                row = acum_t[h:h + 1, :]
                decay = jnp.exp(jnp.where(lower, col - row, -jnp.inf))
                xh = xs[:, hs]
                y_diag = _bdot(CB * decay, xh * dt[:, h:h + 1])
                h_prev = hn_ref[n_prev, i, hs, :]
                y_off = _bdot_nt(Cg, h_prev) * jnp.exp(col)
                e_end = a_end[:, h:h + 1]
                xw_t = xs_t[hs, :] * (dt_t[h:h + 1, :] * jnp.exp(e_end - row))
                hn_ref[n_prev, i, hs, :] = jnp.exp(e_end) * h_prev + _bdot(xw_t, Bg)
                ysc[i, :, hs] = y_diag + y_off + dsk_ref[:, h:h + 1] * xh
        y = ysc[i] * jax.nn.silu(z_ref[i])
        y_ref[i] = y * lax.rsqrt(jnp.mean(y * y, -1, keepdims=True) + LN_EPS) * nw_ref[...]


def _ssd(xbc3, z3, sm3, conv0, conv_base, h0, h_base, cw, cb, dtb, alog, dsk, nw, prev, nb, L):
    B, T, _ = xbc3.shape
    K1 = SHORT_CONV - 1
    n_prev = 0 if prev is None else prev.shape[0]
    tokspec = lambda a: pl.BlockSpec((nb, L, a.shape[2]), lambda b, t: (b, t, 0))
    full = lambda a: pl.BlockSpec(a.shape, lambda b, t: (0,) * a.ndim)
    in_specs = [tokspec(xbc3), tokspec(z3), tokspec(sm3),
                pl.BlockSpec((nb, K1, SSD_CONV_DIM), lambda b, t: (conv_base // nb + b, 0, 0)),
                pl.BlockSpec((nb, SSD_INNER, SSD_STATE), lambda b, t: (h_base // nb + b, 0, 0)),
                full(cw), full(cb), full(dtb), full(alog), full(dsk), full(nw)]
    args = [xbc3, z3, sm3, conv0, h0, cw, cb, dtb, alog, dsk, nw]
    if n_prev:
        in_specs.append(pl.BlockSpec((n_prev, nb, SSD_INNER, SSD_STATE), lambda b, t: (0, b, 0, 0)))
        args.append(prev)
    return pl.pallas_call(
        functools.partial(_ssd_body, nb=nb, L=L, n_prev=n_prev),
        grid=(B // nb, T // L),
        in_specs=in_specs,
        out_specs=[tokspec(z3), pl.BlockSpec((n_prev + 1, nb, SSD_INNER, SSD_STATE), lambda b, t: (0, b, 0, 0)),
                   pl.BlockSpec((nb, K1, SSD_CONV_DIM), lambda b, t: (b, 0, 0))],
        out_shape=[jax.ShapeDtypeStruct(z3.shape, F32),
                   jax.ShapeDtypeStruct((n_prev + 1, B, SSD_INNER, SSD_STATE), F32),
                   jax.ShapeDtypeStruct((B, K1, SSD_CONV_DIM), F32)],
        scratch_shapes=[pltpu.VMEM((nb, HALO + L, SSD_CONV_DIM), F32), pltpu.VMEM((nb, L, SSD_INNER), F32)],
        compiler_params=_cparams("parallel", "arbitrary"),
        name="ssd",
    )(*args)


def _gla_body(*refs, nb, tc, l, n_prev):
    q_ref, k_ref, v_ref, gg_ref, sm_ref, s0_ref, wa_ref, ba_ref, nw_ref = refs[:9]
    prev_ref = refs[9] if n_prev else None
    o_ref, sn_ref, st = refs[9 + bool(n_prev):]
    K, V = GLA_DK, GLA_DV

    @pl.when(pl.program_id(1) == 0)
    def _():
        for i in range(nb):
            for h in range(GLA_HEADS):
                st[i, h * V:(h + 1) * V, :] = s0_ref[i, h * K:(h + 1) * K, :].T
        if n_prev:
            sn_ref[0:n_prev] = prev_ref[...]

    ri, ci = _iota((tc, tc), 0), _iota((tc, tc), 1)
    shift = int(np.log2(l))
    lower = (ri >= ci) & (jnp.right_shift(ri, shift) == jnp.right_shift(ci, shift))
    for i in range(nb):
        log_alpha = jax.nn.log_sigmoid(_bdot(sm_ref[i], wa_ref[...]) + ba_ref[...]) / GLA_TAU
        bc = jnp.dot(lower.astype(F32), log_alpha, precision=HIGHEST, preferred_element_type=F32)
        q = q_ref[i] * GLA_DK ** -0.5
        k = k_ref[i]
        v = v_ref[i]
        qe = q * jnp.exp(bc)
        ke = k * jnp.exp(-bc)
        for h in range(GLA_HEADS):
            cs = slice(h * K, (h + 1) * K)
            att = jnp.where(lower, _bdot_nt(qe[:, cs], ke[:, cs]), 0.0)
            o_h = _bdot(att, v[:, cs])
            subs = [slice(c * l, (c + 1) * l) for c in range(tc // l)]
            b_ends = [bc[rs.stop - 1:rs.stop, cs] for rs in subs]
            kvs = [_bdot_tn(v[rs, cs], k[rs, cs] * jnp.exp(b_end - bc[rs, cs])) for rs, b_end in zip(subs, b_ends)]
            states = [st[i, h * V:(h + 1) * V, :]]
            for b_end, kv in zip(b_ends, kvs):
                states.append(states[-1] * jnp.exp(b_end) + kv)
            inter = [_bdot_nt(qe[rs, cs], s_prev) for rs, s_prev in zip(subs, states)]
            s_t = states[-1]
            st[i, h * V:(h + 1) * V, :] = s_t
            sn_ref[n_prev, i, h * K:(h + 1) * K, :] = s_t.T
            o_h = o_h + (inter[0] if len(inter) == 1 else jnp.concatenate(inter, axis=0))
            o_h = o_h * lax.rsqrt(jnp.mean(o_h * o_h, -1, keepdims=True) + LN_EPS) * nw_ref[...]
            o_ref[i, :, cs] = o_h * jax.nn.silu(gg_ref[i, :, cs])


def _gla(gq3, gk3, gv3, gg3, sm3, s0, s_base, wa, ba, nw, prev, nb, tc):
    B, T, _ = gq3.shape
    l = min(GLA_CHUNK, T)
    n_prev = 0 if prev is None else prev.shape[0]
    tokspec = lambda a: pl.BlockSpec((nb, tc, a.shape[2]), lambda b, t: (b, t, 0))
    full = lambda a: pl.BlockSpec(a.shape, lambda b, t: (0,) * a.ndim)
    in_specs = [tokspec(gq3), tokspec(gk3), tokspec(gv3), tokspec(gg3), tokspec(sm3),
                pl.BlockSpec((nb, GLA_W, GLA_DV), lambda b, t: (s_base // nb + b, 0, 0)),
                full(wa), full(ba), full(nw)]
    args = [gq3, gk3, gv3, gg3, sm3, s0, wa, ba, nw]
    if n_prev:
        in_specs.append(pl.BlockSpec((n_prev, nb, GLA_W, GLA_DV), lambda b, t: (0, b, 0, 0)))
        args.append(prev)
    return pl.pallas_call(
        functools.partial(_gla_body, nb=nb, tc=tc, l=l, n_prev=n_prev),
        grid=(B // nb, T // tc),
        in_specs=in_specs,
        out_specs=[tokspec(gq3), pl.BlockSpec((n_prev + 1, nb, GLA_W, GLA_DV), lambda b, t: (0, b, 0, 0))],
        out_shape=[jax.ShapeDtypeStruct(gq3.shape, F32), jax.ShapeDtypeStruct((n_prev + 1, B, GLA_W, GLA_DV), F32)],
        scratch_shapes=[pltpu.VMEM((nb, GLA_HEADS * GLA_DV, GLA_DK), F32)],
        compiler_params=_cparams("parallel", "arbitrary"),
        name="gla",
    )(*args)


EVEN_SEGS = (
    (0, EV_Q, (True,) * 4),
    (EV_Q, 4 * LANES, (True, False, True, False)),
    (EV_Q + 4 * LANES, 2 * LANES, (True, False)),
    (EV_Q + EV_KV, LRU_WIDTH, None),
    (EV_Q + EV_KV + LRU_WIDTH, LRU_WIDTH, None),
    (EV_Q + EV_KV + 2 * LRU_WIDTH, LANES, None),
)
ODD_SEGS = (
    (0, SSD_INNER, None),
    (SSD_INNER, SSD_CONV_DIM, None),
    (SSD_INNER + SSD_CONV_DIM, GLA_W, None),
    (SSD_INNER + SSD_CONV_DIM + GLA_W, GLA_W, None),
    (SSD_INNER + SSD_CONV_DIM + 2 * GLA_W, GLA_W, None),
    (SSD_INNER + SSD_CONV_DIM + 3 * GLA_W, GLA_W, None),
    (SSD_INNER + SSD_CONV_DIM + 4 * GLA_W, LANES, None),
)


def _even_w_in(w):
    a = EV_Q + EV_KV
    pad = jnp.zeros((D_MODEL, LANES - EV_GATE), w.dtype)
    return jnp.concatenate([w[:, :a], w[:, a + EV_GATE:], w[:, a:a + EV_GATE], pad], axis=1).astype(BF16)


def _odd_w_in(w):
    a = SSD_INNER + SSD_CONV_DIM
    dt = w[:, a:a + SSD_HEADS]
    rest = w[:, a + SSD_HEADS:a + SSD_HEADS + 4 * GLA_W]
    ga = w[:, a + SSD_HEADS + 4 * GLA_W:]
    pad = jnp.zeros((D_MODEL, LANES - SSD_HEADS - GLA_RANK), w.dtype)
    return jnp.concatenate([w[:, :a], rest, dt, ga, pad], axis=1).astype(BF16)


def _lane_pad(v):
    return jnp.pad(v.astype(F32), (0, LANES - v.shape[0]))[None, :]


def _prep_layer(layer, P):
    j = layer // 2
    d = dict(
        ffn_wup=P['ffn_w_up_bf16'], ffn_cw=P['ffn_conv_w'][layer], ffn_cb=P['ffn_conv_b'][layer][None, :],
        ffn_wdn=P['ffn_w_down_bf16'],
        ln_mix_g=P['ln_mix_g'][layer][None, :], ln_mix_b=P['ln_mix_b'][layer][None, :],
        ln_ffn_g=P['ln_ffn_g'][layer][None, :], ln_ffn_b=P['ln_ffn_b'][layer][None, :])
    if layer % 2 == 0:
        w_out = P['w_out_even'][j].astype(BF16)
        wg = P['lru_w_gates'][j]
        eye = jnp.eye(LRU_BLOCKS, dtype=F32)
        wg = jnp.einsum('knde,nm->kndme', wg, eye).reshape(2, LRU_WIDTH, LRU_WIDTH)
        d.update(
            w_in=_even_w_in(P['w_in_even'][j]), w_out1=w_out[:EV_Q], w_out2=w_out[EV_Q:],
            cmp_tab=jnp.repeat(P['nsa_cmp_w'][j].reshape(CMP_BLOCK, 2 * NSA_KV_HEADS), HEAD_DIM, axis=1),
            cmp_page_tab=_page_compress_table(P['nsa_cmp_w'][j]),
            lru_cw=P['lru_conv_w'][j], lru_cb=P['lru_conv_b'][j][None, :],
            lru_wg=jnp.concatenate([wg[0], wg[1]], axis=1).astype(BF16),
            lru_bg=P['lru_b_gates'][j].reshape(1, 2 * LRU_WIDTH), lru_lam=P['lru_lambda'][j][None, :])
    else:
        w_out = P['w_out_odd'][j].astype(BF16)
        wa = jnp.zeros((LANES, GLA_W), F32).at[SSD_HEADS:SSD_HEADS + GLA_RANK].set(P['gla_w_alpha'][j])
        d.update(
            w_in=_odd_w_in(P['w_in_odd'][j]), w_out1=w_out[:SSD_INNER], w_out2=w_out[SSD_INNER:],
            ssd_cw=P['ssd_conv_w'][j], ssd_cb=P['ssd_conv_b'][j][None, :],
            ssd_dtb=_lane_pad(P['ssd_dt_bias'][j]), ssd_alog=_lane_pad(P['ssd_a_log'][j]),
            ssd_d=_lane_pad(P['ssd_d'][j]), ssd_nw=P['ssd_norm_w'][j][None, :],
            gla_wa=wa.astype(BF16), gla_ba=P['gla_b_alpha'][j][None, :], gla_nw=P['gla_norm_w'][j][None, :])
    return d


def _trunk(x3, prm, st, cfg):
    B, T, _ = x3.shape
    M = B * T
    out = dict(kv=[], win=[], lh=[], lc=[], sc=[], fc=[])
    ssd_states = gla_states = None
    for layer in range(DEPTH):
        p = prm[layer]
        j = layer // 2
        x2 = x3.reshape(M, D_MODEL)
        if layer % 2 == 0:
            q, rows, win, hx, hgate, hg = _proj(x2, p['w_in'], EVEN_SEGS, cfg['tm'], cfg['rope'], cfg['rope_blocks'])
            to3 = lambda a: a.reshape(B, T, a.shape[1])
            rows3, win3 = to3(rows), to3(win)
            if cfg['sample']:
                o_nsa = _nsa_sample(to3(q), to3(hg), rows3, win3, st['cache_t'], st['page_table'],
                                    j * st['n_phys'], st['cwin_t'], j * B, p['cmp_page_tab'], cfg['nsa_nb'])
                win_keep = win3
            else:
                H3 = _compress_prompt(rows3, p['cmp_tab'])
                o_nsa = _nsa_prompt(to3(q), to3(hg), H3, rows3, win3, cfg['tq'])
                win_keep = win3[:, T - min(WINDOW, T):]
            y_lru, h_last, conv_n = _lru(to3(hx), to3(hgate), st['lru_conv'], j * B, st['lru_h'], j * B,
                                         p['lru_cw'], p['lru_cb'], p['lru_wg'], p['lru_bg'], p['lru_lam'],
                                         cfg['lru_nb'], cfg['lru_tc'])
            a1, a2 = o_nsa, y_lru
            out['kv'].append(rows3.reshape(B, T, 4, NSA_KV_HEADS, HEAD_DIM))
            out['win'].append(win_keep.reshape(B, -1, 2, NSA_KV_HEADS, HEAD_DIM))
            out['lh'].append(h_last.reshape(B, LRU_WIDTH))
            out['lc'].append(conv_n)
        else:
            z, xbc, gq, gk, gv, gg, sm = _proj(x2, p['w_in'], ODD_SEGS, cfg['tm'])
            to3 = lambda a: a.reshape(B, T, a.shape[1])
            sm3 = to3(sm)
            y_ssd, ssd_states, conv_n = _ssd(
                to3(xbc), to3(z), sm3, st['ssd_conv'], j * B, st['ssd_h'], j * B, p['ssd_cw'], p['ssd_cb'],
                p['ssd_dtb'], p['ssd_alog'], p['ssd_d'], p['ssd_nw'], ssd_states, cfg['rec_nb'], cfg['ssd_L'])
            o_gla, gla_states = _gla(to3(gq), to3(gk), to3(gv), to3(gg), sm3, st['gla_s'], j * B,
                                     p['gla_wa'], p['gla_ba'], p['gla_nw'], gla_states, cfg['rec_nb'], cfg['gla_tc'])
            a1, a2 = y_ssd, o_gla
            out['sc'].append(conv_n)
        x3, fbuf = _mix_ffn(x3, a1, a2, p['w_out1'], p['w_out2'], p['ln_mix_g'], p['ln_mix_b'], layer,
                            p['ffn_wup'], p['ffn_cw'], p['ffn_cb'], p['ffn_wdn'], st['ffn_conv'], layer * B,
                            p['ln_ffn_g'], p['ln_ffn_b'], cfg['ffn_nb'], cfg['ffn_tc'], cfg['ffn_cw'])
        out['fc'].append(fbuf)
    out = {k: jnp.stack(v) for k, v in out.items()}
    out['sh'] = ssd_states.reshape(-1, B, SSD_HEADS, SSD_HEAD_DIM, SSD_STATE)
    out['gs'] = gla_states.reshape(-1, B, GLA_HEADS, GLA_DK, GLA_DV)
    return x3, out


def _largest_tile(n, cap):
    t = min(n, cap)
    while n % t:
        t //= 2
    return t


def kernel(x_prompt, x_sample, cache_nsa_kv, cache_nsa_win, state_lru_h, state_lru_conv, state_ssd, state_ssd_conv, state_gla, state_ffn_conv, page_table, w_in_even, w_out_even, nsa_cmp_w, lru_conv_w, lru_conv_b, lru_w_gates, lru_b_gates, lru_lambda, w_in_odd, w_out_odd, ssd_conv_w, ssd_conv_b, ssd_dt_bias, ssd_a_log, ssd_d, ssd_norm_w, gla_w_alpha, gla_b_alpha, gla_norm_w, ffn_w_up, ffn_conv_w, ffn_conv_b, ffn_w_down, ln_mix_g, ln_mix_b, ln_ffn_g, ln_ffn_b):
    P = dict(w_in_even=w_in_even, w_out_even=w_out_even, nsa_cmp_w=nsa_cmp_w, lru_conv_w=lru_conv_w,
             lru_conv_b=lru_conv_b, lru_w_gates=lru_w_gates, lru_b_gates=lru_b_gates, lru_lambda=lru_lambda,
             w_in_odd=w_in_odd, w_out_odd=w_out_odd, ssd_conv_w=ssd_conv_w, ssd_conv_b=ssd_conv_b,
             ssd_dt_bias=ssd_dt_bias, ssd_a_log=ssd_a_log, ssd_d=ssd_d, ssd_norm_w=ssd_norm_w,
             gla_w_alpha=gla_w_alpha, gla_b_alpha=gla_b_alpha, gla_norm_w=gla_norm_w, ffn_w_up=ffn_w_up,
             ffn_conv_w=ffn_conv_w, ffn_conv_b=ffn_conv_b, ffn_w_down=ffn_w_down, ln_mix_g=ln_mix_g,
             ln_mix_b=ln_mix_b, ln_ffn_g=ln_ffn_g, ln_ffn_b=ln_ffn_b)
    P['ffn_w_up_bf16'] = ffn_w_up.astype(BF16)
    P['ffn_w_down_bf16'] = ffn_w_down.astype(BF16)
    prm = [_prep_layer(layer, P) for layer in range(DEPTH)]
    n_even, n_odd = w_in_even.shape[0], w_in_odd.shape[0]
    ffn_buf = (FFN_CONV - 1, 2 * D_FF)

    Bp, Tp, _ = x_prompt.shape
    tm_p = _largest_tile(Tp, 512)
    st_p = dict(
        lru_conv=jnp.zeros((n_even * Bp, SHORT_CONV - 1, LRU_WIDTH), F32), lru_h=jnp.zeros((n_even * Bp, 1, LRU_WIDTH), F32),
        ssd_conv=jnp.zeros((n_odd * Bp, SHORT_CONV - 1, SSD_CONV_DIM), F32),
        ssd_h=jnp.zeros((n_odd * Bp, SSD_INNER, SSD_STATE), F32), gla_s=jnp.zeros((n_odd * Bp, GLA_W, GLA_DV), F32),
        ffn_conv=jnp.zeros((DEPTH * Bp,) + ffn_buf, F32))
    cfg_p = dict(sample=False, tm=tm_p, rope=_rope_tables(jnp.arange(Tp)), rope_blocks=Tp // tm_p,
                 tq=_largest_tile(Tp, 256), lru_nb=1, lru_tc=_largest_tile(Tp, 512),
                 rec_nb=1, ssd_L=_largest_tile(Tp, 512), gla_tc=_largest_tile(Tp, 256),
                 ffn_nb=1, ffn_tc=_largest_tile(Tp, 512), ffn_cw=D_FF // 2)
    y_p, o_p = _trunk(x_prompt, prm, st_p, cfg_p)

    Bs, Ts, _ = x_sample.shape
    n_phys = cache_nsa_kv.shape[1]
    past_len = page_table.shape[1] * PAGE_SIZE
    tm_s = _largest_tile(Bs * Ts, 256)
    pos_s = past_len + jnp.arange(tm_s) % Ts
    nb_s = _largest_tile(Bs, 32)
    st_s = dict(
        cache_t=cache_nsa_kv.transpose(0, 1, 3, 4, 5, 2).reshape(n_even * n_phys, 4 * LANES, PAGE_SIZE),
        n_phys=n_phys, page_table=page_table,
        cwin_t=cache_nsa_win.transpose(0, 1, 3, 4, 5, 2).reshape(n_even * Bs, 2 * LANES, cache_nsa_win.shape[2]),
        lru_conv=state_lru_conv.reshape(n_even * Bs, SHORT_CONV - 1, LRU_WIDTH),
        lru_h=state_lru_h.reshape(n_even * Bs, 1, LRU_WIDTH),
        ssd_conv=state_ssd_conv.reshape(n_odd * Bs, SHORT_CONV - 1, SSD_CONV_DIM),
        ssd_h=state_ssd.reshape(n_odd * Bs, SSD_INNER, SSD_STATE),
        gla_s=state_gla.reshape(n_odd * Bs, GLA_W, GLA_DV),
        ffn_conv=state_ffn_conv.reshape((DEPTH * Bs,) + ffn_buf))
    cfg_s = dict(sample=True, tm=tm_s, rope=_rope_tables(pos_s), rope_blocks=1, nsa_nb=_largest_tile(Bs, 4),
                 rec_nb=_largest_tile(Bs, 4),
                 lru_nb=nb_s, lru_tc=Ts, ssd_L=Ts, gla_tc=Ts, ffn_nb=nb_s, ffn_tc=Ts, ffn_cw=D_FF // 2)
    y_s, o_s = _trunk(x_sample, prm, st_s, cfg_s)
    win_t = _window_append(st_s['cwin_t'], o_s['win'].reshape(n_even * Bs, Ts, 2 * LANES),
                           _largest_tile(n_even * Bs, SUBLANES))
    win_s = win_t.reshape(n_even, Bs, 2, NSA_KV_HEADS, HEAD_DIM, -1).transpose(0, 1, 5, 2, 3, 4)

    return (y_p, y_s, o_p['kv'], o_s['kv'], o_p['win'], win_s, o_p['lh'], o_s['lh'], o_p['lc'], o_s['lc'],
            o_p['sh'], o_s['sh'], o_p['sc'], o_s['sc'], o_p['gs'], o_s['gs'], o_p['fc'], o_s['fc'])
```

```python
import functools

import jax
import jax.numpy as jnp
import numpy as np
from jax import lax
from jax.experimental import pallas as pl
from jax.experimental.pallas import tpu as pltpu

F32 = jnp.float32
BF16 = jnp.bfloat16
HIGHEST = lax.Precision.HIGHEST

D_MODEL = 1024
DEPTH = 4
PAGE_SIZE = 128
HEAD_DIM = 64
ROPE_DIM = HEAD_DIM // 4
ROPE_THETA = 500000.0
NSA_HEADS = 8
NSA_KV_HEADS = 2
NSA_GROUP = NSA_HEADS // NSA_KV_HEADS
CMP_BLOCK = 32
CMP_STRIDE = 16
SEL_BLOCK = 64
N_SEL = 8
WINDOW = 512
LRU_WIDTH = D_MODEL // 2
LRU_BLOCKS = 8
LRU_BLOCK_DIM = LRU_WIDTH // LRU_BLOCKS
LRU_C = 8.0
SHORT_CONV = 4
SSD_HEADS = 16
SSD_HEAD_DIM = 64
SSD_INNER = SSD_HEADS * SSD_HEAD_DIM
SSD_GROUPS = 2
SSD_STATE = 128
SSD_CONV_DIM = SSD_INNER + 2 * SSD_GROUPS * SSD_STATE
GLA_HEADS = 4
GLA_DK = 128
GLA_DV = 128
GLA_RANK = 16
GLA_TAU = 16.0
GLA_CHUNK = 32
D_FF = 2816
FFN_CONV = 3
ALPHA = (2.0 * DEPTH) ** 0.25
LN_EPS = 1e-5
NEG = -1e30
EV_Q = NSA_HEADS * HEAD_DIM
EV_KV = 6 * NSA_KV_HEADS * HEAD_DIM
EV_GATE = 3 * NSA_HEADS
GLA_W = GLA_HEADS * GLA_DK

LANES = 128
SUBLANES = 8
V7X_VMEM_BYTES = 64 * 1024 * 1024
VMEM_LIMIT = V7X_VMEM_BYTES - 8 * 1024 * 1024
HALO = SUBLANES


def _cparams(*sem):
    return pltpu.CompilerParams(dimension_semantics=sem, vmem_limit_bytes=VMEM_LIMIT)


def _resident(a):
    return pl.BlockSpec(a.shape, lambda *_: (0,) * a.ndim, pipeline_mode=pl.Buffered(1))


def _bdot(a, b):
    return jnp.dot(a.astype(BF16), b.astype(BF16), preferred_element_type=F32)


def _bdot_nt(a, b):
    return lax.dot_general(a.astype(BF16), b.astype(BF16), (((1,), (1,)), ((), ())), preferred_element_type=F32)


def _bdot_tn(a, b):
    return lax.dot_general(a.astype(BF16), b.astype(BF16), (((0,), (0,)), ((), ())), preferred_element_type=F32)


def _iota(shape, axis):
    return lax.broadcasted_iota(jnp.int32, shape, axis)


def _layer_norm(y, g, b):
    mu = jnp.mean(y, -1, keepdims=True)
    d = y - mu
    var = jnp.mean(d * d, -1, keepdims=True)
    return d * lax.rsqrt(var + LN_EPS) * g + b


def _masked_softmax(s, mask):
    s = jnp.where(mask, s, NEG)
    m = jnp.max(s, -1, keepdims=True)
    e = jnp.exp(s - m)
    p = e / jnp.sum(e, -1, keepdims=True)
    return jnp.where(mask, p, 0.0)


def _proj_body(*refs, segs, has_rope):
    if has_rope:
        x_ref, w_ref, c_ref, s1_ref, s2_ref = refs[:5]
        out_refs = refs[5:]
    else:
        x_ref, w_ref = refs[:2]
        out_refs = refs[2:]
    xb = x_ref[...].astype(BF16)
    for o_ref, (start, width, rope) in zip(out_refs, segs):
        acc = jnp.dot(xb, w_ref[:, start:start + width], preferred_element_type=F32)
        if rope is None:
            o_ref[...] = acc
            continue
        for c, flag in enumerate(rope):
            chunk = acc[:, c * LANES:(c + 1) * LANES]
            if flag:
                chunk = (chunk * c_ref[...] + pltpu.roll(chunk, LANES - ROPE_DIM // 2, 1) * s2_ref[...]
                         + pltpu.roll(chunk, ROPE_DIM // 2, 1) * s1_ref[...])
            o_ref[:, c * LANES:(c + 1) * LANES] = chunk


def _proj(x2d, w, segs, tm, tabs=None, tab_blocks=1):
    M, K = x2d.shape
    N = w.shape[1]
    has_rope = tabs is not None
    in_specs = [pl.BlockSpec((tm, K), lambda i: (i, 0)), _resident(w)]
    args = [x2d, w]
    if has_rope:
        in_specs += [pl.BlockSpec((tm, LANES), lambda i: (i % tab_blocks, 0))] * 3
        args += list(tabs)
    return pl.pallas_call(
        functools.partial(_proj_body, segs=segs, has_rope=has_rope),
        grid=(M // tm,),
        in_specs=in_specs,
        out_specs=[pl.BlockSpec((tm, wd), lambda i: (i, 0)) for _, wd, _ in segs],
        out_shape=[jax.ShapeDtypeStruct((M, wd), F32) for _, wd, _ in segs],
        compiler_params=_cparams("parallel"),
        name="proj",
    )(*args)


def _rope_tables(pos):
    half = ROPE_DIM // 2
    inv = jnp.power(ROPE_THETA, -2.0 * jnp.arange(half, dtype=F32) / ROPE_DIM)
    ang = pos.astype(F32)[:, None] * inv[None, :]
    cos, sin = jnp.cos(ang), jnp.sin(ang)
    R = pos.shape[0]
    rest = HEAD_DIM - ROPE_DIM
    c64 = jnp.concatenate([cos, cos, jnp.ones((R, rest), F32)], 1)
    s1 = jnp.concatenate([jnp.zeros((R, half), F32), sin, jnp.zeros((R, rest), F32)], 1)
    s2 = jnp.concatenate([-sin, jnp.zeros((R, half + rest), F32)], 1)
    return tuple(jnp.concatenate([t, t], 1) for t in (c64, s1, s2))


def _compress_rows(x, w_ref):
    x3 = x.reshape(x.shape[0] // CMP_STRIDE, CMP_STRIDE, 2 * LANES)
    return (jnp.sum(x3 * w_ref[0:CMP_STRIDE, :][None], axis=1),
            jnp.sum(x3 * w_ref[CMP_STRIDE:CMP_BLOCK, :][None], axis=1))


def _compress_body(kv_ref, w_ref, o_ref):
    h0, h1 = _compress_rows(kv_ref[0], w_ref)
    o_ref[0, :, 0:2 * LANES] = h0
    o_ref[0, :, 2 * LANES:4 * LANES] = h1


def _compress_prompt(rows3, wtab):
    B, T, _ = rows3.shape
    return pl.pallas_call(
        _compress_body,
        grid=(B,),
        in_specs=[pl.BlockSpec((1, T, 2 * LANES), lambda b: (b, 0, 0)),
                  pl.BlockSpec((CMP_BLOCK, 2 * LANES), lambda b: (0, 0))],
        out_specs=pl.BlockSpec((1, T // CMP_STRIDE, 4 * LANES), lambda b: (b, 0, 0)),
        out_shape=jax.ShapeDtypeStruct((B, T // CMP_STRIDE, 4 * LANES), F32),
        compiler_params=_cparams("parallel"),
        name="compress_prompt",
    )(rows3, wtab)


def _overlap_matrix(n_rows, n_cmp, n_sel):
    s1 = np.arange(n_rows)[:, None] * CMP_STRIDE
    s2 = np.arange(LANES)[None, :] * SEL_BLOCK
    ov = np.clip(np.minimum(s1 + CMP_BLOCK, s2 + SEL_BLOCK) - np.maximum(s1, s2), 0, None) / CMP_BLOCK
    ov = ov * (np.arange(n_rows)[:, None] < n_cmp) * (np.arange(LANES)[None, :] < n_sel)
    return jnp.asarray(ov, dtype=F32)


def _select_blocks(psum, ov, t_col, n_sel):
    imp = jnp.dot(psum, ov, precision=HIGHEST, preferred_element_type=F32)
    blk = _iota(imp.shape, 1)
    cur = jnp.right_shift(t_col, int(np.log2(SEL_BLOCK)))
    future = blk * SEL_BLOCK > t_col
    forced = (blk == 0) | (blk == cur) | (blk == cur - 1)
    w = jnp.where(future, -1.0, jnp.where(forced, 1e6, imp))
    w = jnp.where(blk < n_sel, w, -jnp.inf)
    n_pad = -(-n_sel // SUBLANES) * SUBLANES
    wt = w.T[0:n_pad, :]
    sub = _iota((n_pad, 1), 0)
    rank = jnp.zeros(wt.shape, F32)
    for i in range(n_sel):
        wi = wt[i:i + 1, :]
        rank = rank + jnp.where((wi > wt) | ((wi == wt) & (sub > i)), 1.0, 0.0)
    sel_t = jnp.where((rank < min(N_SEL, n_sel)) & (sub < n_sel), 1.0, 0.0)
    if n_pad < LANES:
        sel_t = jnp.concatenate([sel_t, jnp.zeros((LANES - n_pad, sel_t.shape[1]), F32)], axis=0)
    return sel_t.T


def _stack_heads(q, h):
    G = NSA_GROUP
    qh = jnp.concatenate([q[:, (h * G + g) * HEAD_DIM:(h * G + g + 1) * HEAD_DIM] for g in range(G)], axis=0)
    return (qh * HEAD_DIM ** -0.5).astype(BF16)


def _expand_matrix(K):
    blk = jnp.right_shift(_iota((LANES, K), 1), int(np.log2(SEL_BLOCK)))
    return (blk == _iota((LANES, K), 0)).astype(BF16)


def _attend(qb, chunks, mask, tq):
    G = NSA_GROUP
    dh = HEAD_DIM
    bias = jnp.where(mask, 0.0, NEG)
    ks = [k.astype(BF16) for _, k, _ in chunks]
    vs = [jnp.concatenate([v.astype(BF16), jnp.ones(v.shape, BF16)], axis=0 if tr else 1) for tr, _, v in chunks]
    reps = G if tq >= LANES else 1
    rows = G * tq // reps
    outs = []
    for r in range(reps):
        q_r = qb[r * rows:(r + 1) * rows]
        s = [jnp.dot(q_r, k, preferred_element_type=F32) if tr else
             lax.dot_general(q_r, k, (((1,), (1,)), ((), ())), preferred_element_type=F32)
             for (tr, _, _), k in zip(chunks, ks)]
        s = s[0] if len(s) == 1 else jnp.concatenate(s, axis=-1)
        K = s.shape[-1]
        s = (s.reshape(rows // tq, tq, K) + bias[None]).reshape(rows, K)
        e = jnp.exp(s - jnp.max(s, -1, keepdims=True)).astype(BF16)
        ox, off = None, 0
        for (tr, _, _), v in zip(chunks, vs):
            kc = v.shape[1] if tr else v.shape[0]
            part = (lax.dot_general(e[:, off:off + kc], v, (((1,), (1,)), ((), ())), preferred_element_type=F32)
                    if tr else jnp.dot(e[:, off:off + kc], v, preferred_element_type=F32))
            ox = part if ox is None else ox + part
            off += kc
        outs.append(ox[:, :dh] / ox[:, dh:dh + 1])
    return outs[0] if reps == 1 else jnp.concatenate(outs, axis=0)


def _gate_store(o_ref, sg, h, tq, o_c, o_s, o_w):
    for g in range(NSA_GROUP):
        hh = h * NSA_GROUP + g
        r = slice(g * tq, (g + 1) * tq)
        o_ref[:, hh * HEAD_DIM:(hh + 1) * HEAD_DIM] = (
            sg[:, 3 * hh:3 * hh + 1] * o_c[r] + sg[:, 3 * hh + 1:3 * hh + 2] * o_s[r]
            + sg[:, 3 * hh + 2:3 * hh + 3] * o_w[r])


def _nsa_core(items, ov, t_col, n_sel):
    tq = t_col.shape[0]
    G = NSA_GROUP
    o_cs, psums = [], []
    for it in items:
        n_rows = it['ckk'].shape[0]
        p_c = _masked_softmax(_bdot_nt(it['qb'], it['ckk']).reshape(G, tq, n_rows), it['mask_c'][None])
        o_cs.append(_bdot(p_c.reshape(G * tq, n_rows), it['ckv']))
        psum = p_c[0]
        for g in range(1, G):
            psum = psum + p_c[g]
        psums.append(psum)
    o_ws = [it['win'](it['qb']) for it in items]
    n = len(items)
    sel = _select_blocks(jnp.concatenate(psums, axis=0), ov, jnp.concatenate([t_col] * n, axis=0), n_sel)
    for i, it in enumerate(items):
        o_s = it['slc'](it['qb'], sel[i * tq:(i + 1) * tq])
        it['finish'](o_cs[i], o_s, o_ws[i])


def _combine_compressed(h0, h1):
    return h0 + pltpu.roll(h1, h1.shape[0] - 1, 0)


def _nsa_prompt_body(q_ref, hg_ref, H_ref, rows_ref, win_ref, ov_ref, o_ref, *, T, tq):
    qi = pl.program_id(1)
    n_rows = T // CMP_STRIDE
    n_cmp = n_rows - CMP_BLOCK // CMP_STRIDE + 1
    n_sel = T // SEL_BLOCK
    t_col = qi * tq + _iota((tq, 1), 0)
    ck = _combine_compressed(H_ref[0, :, 0:2 * LANES], H_ref[0, :, 2 * LANES:4 * LANES])
    ncol = _iota((1, n_rows), 1)
    mask_c = (ncol * CMP_STRIDE + CMP_BLOCK - 1 <= t_col) & (ncol < n_cmp)
    band = WINDOW + tq
    wstart = pl.multiple_of(jnp.clip(qi * tq - WINDOW, 0, T - band), tq)
    wpos = wstart + _iota((1, band), 1)
    mask_w = (wpos <= t_col) & (wpos > t_col - WINDOW)
    q = q_ref[0]
    sg = jax.nn.sigmoid(hg_ref[0])
    kstep = min(T, 4 * LANES)

    def make_item(h):
        def slc_span(K):
            def run(qb, sel):
                allowed = jnp.dot(sel.astype(BF16), _expand_matrix(K), preferred_element_type=F32) > 0.5
                mask = allowed & (_iota((1, K), 1) <= t_col)
                k = rows_ref[0, 0:K, 2 * LANES + h * HEAD_DIM:2 * LANES + (h + 1) * HEAD_DIM]
                v = rows_ref[0, 0:K, 3 * LANES + h * HEAD_DIM:3 * LANES + (h + 1) * HEAD_DIM]
                return _attend(qb, [(False, k, v)], mask, tq)
            return run

        def slc(qb, sel):
            spans = [slc_span((i + 1) * kstep) for i in range(T // kstep)]
            if len(spans) == 1:
                return spans[0](qb, sel)
            return lax.switch(lax.div(qi * tq + (tq - 1), kstep), spans, qb, sel)

        def win(qb):
            k = win_ref[0, pl.ds(wstart, band), h * HEAD_DIM:(h + 1) * HEAD_DIM]
            v = win_ref[0, pl.ds(wstart, band), LANES + h * HEAD_DIM:LANES + (h + 1) * HEAD_DIM]
            return _attend(qb, [(False, k, v)], mask_w, tq)

        return dict(qb=_stack_heads(q, h), ckk=ck[:, h * HEAD_DIM:(h + 1) * HEAD_DIM],
                    ckv=ck[:, LANES + h * HEAD_DIM:LANES + (h + 1) * HEAD_DIM], mask_c=mask_c,
                    slc=slc, win=win, finish=functools.partial(_gate_store, o_ref.at[0], sg, h, tq))

    _nsa_core([make_item(h) for h in range(NSA_KV_HEADS)], ov_ref[...], t_col, n_sel)


def _nsa_prompt(q3, hg3, H3, rows3, win3, tq):
    B, T, _ = q3.shape
    n_rows = T // CMP_STRIDE
    ov = _overlap_matrix(n_rows, n_rows - 1, T // SEL_BLOCK)
    return pl.pallas_call(
        functools.partial(_nsa_prompt_body, T=T, tq=tq),
        grid=(B, T // tq),
        in_specs=[pl.BlockSpec((1, tq, EV_Q), lambda b, i: (b, i, 0)),
                  pl.BlockSpec((1, tq, LANES), lambda b, i: (b, i, 0)),
                  pl.BlockSpec((1, n_rows, 4 * LANES), lambda b, i: (b, 0, 0)),
                  pl.BlockSpec((1, T, 4 * LANES), lambda b, i: (b, 0, 0)),
                  pl.BlockSpec((1, T, 2 * LANES), lambda b, i: (b, 0, 0)),
                  pl.BlockSpec((n_rows, LANES), lambda b, i: (0, 0))],
        out_specs=pl.BlockSpec((1, tq, EV_Q), lambda b, i: (b, i, 0)),
        out_shape=jax.ShapeDtypeStruct((B, T, EV_Q), F32),
        compiler_params=_cparams("parallel", "arbitrary"),
        name="nsa_prompt",
    )(q3, hg3, H3, rows3, win3, ov)


def _nsa_sample_body(*refs, nb, n_pages, Tq, Wb):
    q_ref, hg_ref, rnew_ref, wnew_ref, ctab_ref, ov_ref = refs[1:7]
    page_refs = refs[7:7 + nb * n_pages]
    cwin_ref, o_ref, knew, wnew = refs[7 + nb * n_pages:]
    P = n_pages * PAGE_SIZE
    Ks = P + PAGE_SIZE
    Kw = Wb + PAGE_SIZE
    n_rows = P // CMP_STRIDE
    n_cmp = n_rows - CMP_BLOCK // CMP_STRIDE + 1
    n_sel = -(-(P + Tq) // SEL_BLOCK)
    tt = _iota((Tq, 1), 0)
    t_col = P + tt
    mask_c = (_iota((1, n_rows), 1) < n_cmp) & (tt >= 0)
    causal = _iota((1, Ks), 1) <= t_col
    wpos = _iota((1, Kw), 1)
    mask_w = (wpos <= Wb + tt) & (wpos > Wb + tt - WINDOW)
    expand = _expand_matrix(Ks)
    pad_rows = jnp.zeros((PAGE_SIZE - Tq, 2 * LANES), F32)

    n_grp = 2 * NSA_KV_HEADS
    per_page = PAGE_SIZE // CMP_STRIDE

    def make_item(i, h, cks, sg, pages):
        ksl = slice(h * HEAD_DIM, (h + 1) * HEAD_DIM)
        vsl = slice(LANES + h * HEAD_DIM, LANES + (h + 1) * HEAD_DIM)

        def slc(qb, sel):
            mask = (jnp.dot(sel.astype(BF16), expand, preferred_element_type=F32) > 0.5) & causal
            chunks = [(True, pr[0, 2 * LANES + ksl.start:2 * LANES + ksl.stop, :],
                       pr[0, 2 * LANES + vsl.start:2 * LANES + vsl.stop, :]) for pr in pages]
            chunks.append((False, knew[i, :, ksl], knew[i, :, vsl]))
            return _attend(qb, chunks, mask, Tq)

        def win(qb):
            chunks = [(True, cwin_ref[i, ksl, :], cwin_ref[i, vsl, :]), (False, wnew[i, :, ksl], wnew[i, :, vsl])]
            return _attend(qb, chunks, mask_w, Tq)

        return dict(qb=_stack_heads(q_ref[i], h), ckk=cks[h], ckv=cks[NSA_KV_HEADS + h], mask_c=mask_c,
                    slc=slc, win=win, finish=functools.partial(_gate_store, o_ref.at[i], sg, h, Tq))

    items = []
    for i in range(nb):
        knew[i, 0:Tq, :] = rnew_ref[i, :, 2 * LANES:4 * LANES]
        knew[i, Tq:PAGE_SIZE, :] = pad_rows
        wnew[i, 0:Tq, :] = wnew_ref[i]
        wnew[i, Tq:PAGE_SIZE, :] = pad_rows
        pages = page_refs[i * n_pages:(i + 1) * n_pages]
        parts = [_bdot_nt(ctab_ref[...], pr[0, 0:2 * LANES, :]) for pr in pages]
        cks = []
        for g in range(n_grp):
            r0, cs = g * 2 * per_page, slice(g * HEAD_DIM, (g + 1) * HEAD_DIM)
            cks.append(_combine_compressed(
                jnp.concatenate([pp[r0:r0 + per_page, cs] for pp in parts], axis=0),
                jnp.concatenate([pp[r0 + per_page:r0 + 2 * per_page, cs] for pp in parts], axis=0)))
        sg = jax.nn.sigmoid(hg_ref[i])
        items += [make_item(i, h, cks, sg, pages) for h in range(NSA_KV_HEADS)]
    _nsa_core(items, ov_ref[...], t_col, n_sel)


def _page_compress_table(cmp_w):
    per_page = PAGE_SIZE // CMP_STRIDE
    w = cmp_w.reshape(CMP_BLOCK, 2 * NSA_KV_HEADS).T
    s = jnp.arange(PAGE_SIZE)[None, :] - CMP_STRIDE * jnp.arange(per_page)[:, None]
    inside = (s >= 0) & (s < CMP_STRIDE)
    sc = jnp.clip(s, 0, CMP_STRIDE - 1)
    first = jnp.where(inside[None], w[:, sc], 0.0)
    second = jnp.where(inside[None], w[:, CMP_STRIDE + sc], 0.0)
    return jnp.concatenate([first, second], axis=1).reshape(-1, PAGE_SIZE).astype(BF16)


def _nsa_sample(q3, hg3, rows3, win3, cache_t, page_table, page_base, cwin_t, cwin_base, ctab, nb):
    B, Tq, _ = q3.shape
    n_pages = page_table.shape[1]
    P = n_pages * PAGE_SIZE
    Wb = cwin_t.shape[2]
    assert Wb % LANES == 0 and Wb > LANES and Tq <= SUBLANES and B % nb == 0
    n_rows = P // CMP_STRIDE
    n_sel = -(-(P + Tq) // SEL_BLOCK)
    ov = _overlap_matrix(n_rows, n_rows - 1, n_sel)
    cb0 = cwin_base // nb

    def page_spec(i, p):
        return pl.BlockSpec((1, 4 * LANES, PAGE_SIZE), lambda b, pt: (page_base + pt[b * nb + i, p], 0, 0))

    tok = lambda a: pl.BlockSpec((nb, Tq, a.shape[2]), lambda b, pt: (b, 0, 0))
    grid_spec = pltpu.PrefetchScalarGridSpec(
        num_scalar_prefetch=1,
        grid=(B // nb,),
        in_specs=[tok(q3), tok(hg3), tok(rows3), tok(win3),
                  pl.BlockSpec(ctab.shape, lambda b, pt: (0, 0)), pl.BlockSpec(ov.shape, lambda b, pt: (0, 0))]
        + [page_spec(i, p) for i in range(nb) for p in range(n_pages)]
        + [pl.BlockSpec((nb, 2 * LANES, Wb), lambda b, pt: (cb0 + b, 0, 0))],
        out_specs=tok(q3),
        scratch_shapes=[pltpu.VMEM((nb, PAGE_SIZE, 2 * LANES), F32), pltpu.VMEM((nb, PAGE_SIZE, 2 * LANES), F32)],
    )
    return pl.pallas_call(
        functools.partial(_nsa_sample_body, nb=nb, n_pages=n_pages, Tq=Tq, Wb=Wb),
        grid_spec=grid_spec,
        out_shape=jax.ShapeDtypeStruct((B, Tq, EV_Q), F32),
        compiler_params=_cparams("arbitrary"),
        name="nsa_sample",
    )(page_table, q3, hg3, rows3, win3, ctab, ov, *([cache_t] * (nb * n_pages)), cwin_t)


def _window_append_body(cwin_ref, new_ref, o_ref, pad, *, nb, Tq, Wb):
    new_lanes = _iota((1, LANES), 1) >= LANES - Tq
    pad[Tq:LANES, :] = jnp.zeros((LANES - Tq, pad.shape[1]), F32)
    for i in range(nb):
        pad[0:Tq, :] = new_ref[i]
        new_t = pltpu.roll(pad[...].T, LANES - Tq, 1)
        shifted = pltpu.roll(cwin_ref[i], Wb - Tq, 1)
        o_ref[i, :, 0:Wb - LANES] = shifted[:, 0:Wb - LANES]
        o_ref[i, :, Wb - LANES:Wb] = jnp.where(new_lanes, new_t, shifted[:, Wb - LANES:Wb])


def _window_append(cwin_t, new_rows, nb):
    S, C, Wb = cwin_t.shape
    Tq = new_rows.shape[1]
    assert Wb % LANES == 0 and Wb > LANES and Tq <= LANES and S % nb == 0
    return pl.pallas_call(
        functools.partial(_window_append_body, nb=nb, Tq=Tq, Wb=Wb),
        grid=(S // nb,),
        in_specs=[pl.BlockSpec((nb, C, Wb), lambda s: (s, 0, 0)), pl.BlockSpec((nb, Tq, C), lambda s: (s, 0, 0))],
        out_specs=pl.BlockSpec((nb, C, Wb), lambda s: (s, 0, 0)),
        out_shape=jax.ShapeDtypeStruct(cwin_t.shape, F32),
        scratch_shapes=[pltpu.VMEM((LANES, C), F32)],
        compiler_params=_cparams("parallel"),
        name="window_append",
    )(cwin_t, new_rows)


def _causal_conv(xp, w_ref, b_ref, tc, cols=None):
    K = w_ref.shape[0]
    cs = slice(None) if cols is None else cols
    acc = None
    for j in range(K):
        term = w_ref[j:j + 1, cs][None] * xp[:, HALO - (K - 1) + j:HALO - (K - 1) + j + tc, :]
        acc = term if acc is None else acc + term
    return b_ref[:, cs][None] + acc


def _lru_body(hx_ref, hgate_ref, conv0_ref, h0_ref, cw_ref, cb_ref, wg_ref, bg_ref, lam_ref,
              y_ref, hlast_ref, convn_ref, xp, *, nb, tc):
    K1 = SHORT_CONV - 1

    @pl.when(pl.program_id(1) == 0)
    def _():
        xp[:, HALO - K1:HALO, :] = conv0_ref[...]
        hlast_ref[...] = h0_ref[...]

    xp[:, HALO:HALO + tc, :] = hx_ref[...]
    xc = _causal_conv(xp, cw_ref, cb_ref, tc)
    tail = xp[:, HALO + tc - K1:HALO + tc, :]
    convn_ref[...] = tail
    xp[:, HALO - K1:HALO, :] = tail
    R = nb * tc
    xc2 = xc.reshape(R, LRU_WIDTH)
    gt = _bdot(xc2, wg_ref[...]) + bg_ref[...]
    r_gate = jax.nn.sigmoid(gt[:, :LRU_WIDTH])
    i_gate = jax.nn.sigmoid(gt[:, LRU_WIDTH:])
    log_a = -LRU_C * r_gate * jax.nn.softplus(-lam_ref[...])
    a = jnp.exp(log_a)
    th = jnp.tanh(log_a)
    u = jnp.sqrt(-2.0 * th / (1.0 - th)) * i_gate * xc2
    tpos = lax.rem(_iota((R, 1), 0), tc)
    d = 1
    while d < tc:
        valid = tpos >= d
        u = jnp.where(valid, a * pltpu.roll(u, d, 0) + u, u)
        a = jnp.where(valid, a * pltpu.roll(a, d, 0), a)
        d *= 2
    hprev = jnp.broadcast_to(hlast_ref[...], (nb, tc, LRU_WIDTH)).reshape(R, LRU_WIDTH)
    h = a * hprev + u
    y_ref[...] = (h * jax.nn.gelu(hgate_ref[...].reshape(R, LRU_WIDTH))).reshape(nb, tc, LRU_WIDTH)
    last = _iota((1, tc, 1), 1) == tc - 1
    hlast_ref[...] = jnp.sum(jnp.where(last, h.reshape(nb, tc, LRU_WIDTH), 0.0), axis=1, keepdims=True)


def _lru(hx3, hgate3, conv0, conv_base, h0, h_base, cw, cb, wg, bg, lam, nb, tc):
    B, T, C = hx3.shape
    K1 = SHORT_CONV - 1
    cb0 = conv_base // nb
    hb0 = h_base // nb
    tok = pl.BlockSpec((nb, tc, C), lambda b, t: (b, t, 0))
    full = lambda a: pl.BlockSpec(a.shape, lambda b, t: (0,) * a.ndim)
    return pl.pallas_call(
        functools.partial(_lru_body, nb=nb, tc=tc),
        grid=(B // nb, T // tc),
        in_specs=[tok, tok,
                  pl.BlockSpec((nb, K1, C), lambda b, t: (cb0 + b, 0, 0)),
                  pl.BlockSpec((nb, 1, C), lambda b, t: (hb0 + b, 0, 0)),
                  full(cw), full(cb), full(wg), full(bg), full(lam)],
        out_specs=[tok, pl.BlockSpec((nb, 1, C), lambda b, t: (b, 0, 0)),
                   pl.BlockSpec((nb, K1, C), lambda b, t: (b, 0, 0))],
        out_shape=[jax.ShapeDtypeStruct((B, T, C), F32), jax.ShapeDtypeStruct((B, 1, C), F32),
                   jax.ShapeDtypeStruct((B, K1, C), F32)],
        scratch_shapes=[pltpu.VMEM((nb, HALO + tc, C), F32)],
        compiler_params=_cparams("parallel", "arbitrary"),
        name="lru",
    )(hx3, hgate3, conv0, h0, cw, cb, wg, bg, lam)


def _ffn_body(x_ref, a1_ref, a2_ref, w1_ref, w2_ref, gm_ref, bm_ref, wup_ref, cw_ref, cb_ref, wdn_ref, buf0_ref,
              g_ref, b_ref, y_ref, bufn_ref, sg, sv, *, nb, tc, cw):
    K1 = FFN_CONV - 1

    @pl.when(pl.program_id(1) == 0)
    def _():
        bufn_ref[...] = buf0_ref[...]

    R = nb * tc
    mix = (_bdot(a1_ref[...].reshape(R, a1_ref.shape[2]), w1_ref[...])
           + _bdot(a2_ref[...].reshape(R, a2_ref.shape[2]), w2_ref[...]))
    x = _layer_norm(ALPHA * x_ref[...].reshape(R, D_MODEL) + mix, gm_ref[...], bm_ref[...])
    xb = x.astype(BF16)
    acc = jnp.zeros((R, D_MODEL), F32)
    for c in range(D_FF // cw):
        conv = []
        for half, scr in ((0, sg), (1, sv)):
            cols = slice(half * D_FF + c * cw, half * D_FF + (c + 1) * cw)
            u = jnp.dot(xb, wup_ref[:, cols], preferred_element_type=F32)
            scr[:, HALO:HALO + tc, :] = u.reshape(nb, tc, cw)
            scr[:, HALO - K1:HALO, :] = bufn_ref[:, :, cols]
            conv.append(_causal_conv(scr, cw_ref, cb_ref, tc, cols).reshape(R, cw))
            bufn_ref[:, :, cols] = scr[:, HALO + tc - K1:HALO + tc, :]
        act = jax.nn.gelu(conv[0]) * conv[1]
        acc = acc + jnp.dot(act.astype(BF16), wdn_ref[c * cw:(c + 1) * cw, :], preferred_element_type=F32)
    y = _layer_norm(ALPHA * x + acc, g_ref[...], b_ref[...])
    y_ref[...] = y.reshape(nb, tc, D_MODEL)


def _mix_ffn(x3, a1, a2, w1, w2, gm, bm, layer, wup, cw, cb, wdn, buf0, buf_base, g, b, nb, tc, cwid):
    B, T, _ = x3.shape
    U, K1 = 2 * D_FF, FFN_CONV - 1
    bb0 = buf_base // nb
    tokspec = lambda a: pl.BlockSpec((nb, tc, a.shape[2]), lambda bi, t: (bi, t, 0))
    tok = tokspec(x3)
    full = _resident
    of_layer = lambda a: pl.BlockSpec((None,) + a.shape[1:], lambda *_: (layer, 0, 0), pipeline_mode=pl.Buffered(1))
    return pl.pallas_call(
        functools.partial(_ffn_body, nb=nb, tc=tc, cw=cwid),
        grid=(B // nb, T // tc),
        in_specs=[tok, tokspec(a1), tokspec(a2), full(w1), full(w2), full(gm), full(bm),
                  of_layer(wup), full(cw), full(cb), of_layer(wdn),
                  pl.BlockSpec((nb, K1, U), lambda bi, t: (bb0 + bi, 0, 0)), full(g), full(b)],
        out_specs=[tok, pl.BlockSpec((nb, K1, U), lambda bi, t: (bi, 0, 0))],
        out_shape=[jax.ShapeDtypeStruct(x3.shape, F32), jax.ShapeDtypeStruct((B, K1, U), F32)],
        scratch_shapes=[pltpu.VMEM((nb, HALO + tc, cwid), F32), pltpu.VMEM((nb, HALO + tc, cwid), F32)],
        compiler_params=_cparams("parallel", "arbitrary"),
        name="mix_ffn",
    )(x3, a1, a2, w1, w2, gm, bm, wup, cw, cb, wdn, buf0, g, b)


def _ssd_body(*refs, nb, L, n_prev):
    xbc_ref, z_ref, sm_ref, conv0_ref, h0_ref, cw_ref, cb_ref, dtb_ref, alog_ref, dsk_ref, nw_ref = refs[:11]
    prev_ref = refs[11] if n_prev else None
    y_ref, hn_ref, convn_ref, xp, ysc = refs[11 + bool(n_prev):]
    K1 = SHORT_CONV - 1
    P, N = SSD_HEAD_DIM, SSD_STATE

    @pl.when(pl.program_id(1) == 0)
    def _():
        xp[:, HALO - K1:HALO, :] = conv0_ref[...]
        hn_ref[n_prev] = h0_ref[...]
        if n_prev:
            hn_ref[0:n_prev] = prev_ref[...]

    xp[:, HALO:HALO + L, :] = xbc_ref[...]
    xc = _causal_conv(xp, cw_ref, cb_ref, L)
    tail = xp[:, HALO + L - K1:HALO + L, :]
    convn_ref[...] = tail
    xp[:, HALO - K1:HALO, :] = tail
    head_lane = _iota((1, LANES), 1) < SSD_HEADS
    lower = _iota((L, L), 0) >= _iota((L, L), 1)
    J = SSD_HEADS // SSD_GROUPS
    seqs = []
    for i in range(nb):
        xa = jax.nn.silu(xc[i])
        xs = xa[:, :SSD_INNER]
        dt = jnp.where(head_lane, jax.nn.softplus(sm_ref[i] + dtb_ref[...]), 0.0)
        la = dt * -jnp.exp(alog_ref[...])
        acum = jnp.dot(lower.astype(F32), la, precision=HIGHEST, preferred_element_type=F32)
        seqs.append(dict(xs=xs, bm=xa[:, SSD_INNER:SSD_INNER + SSD_GROUPS * N], cm=xa[:, SSD_INNER + SSD_GROUPS * N:],
                         dt=dt, acum=acum, acum_t=acum.T, dt_t=dt.T, xs_t=xs.T, a_end=acum[L - 1:L, :]))
    for g in range(SSD_GROUPS):
        Bgs = [s['bm'][:, g * N:(g + 1) * N].astype(BF16) for s in seqs]
        Cgs = [s['cm'][:, g * N:(g + 1) * N].astype(BF16) for s in seqs]
        CBs = [_bdot_nt(Cg, Bg) for Cg, Bg in zip(Cgs, Bgs)]
        for j in range(J):
            h = g * J + j
            hs = slice(h * P, (h + 1) * P)
            for i, s in enumerate(seqs):
                col = s['acum'][:, h:h + 1]
                row = s['acum_t'][h:h + 1, :]
                decay = jnp.exp(jnp.where(lower, col - row, -jnp.inf))
                xh = s['xs'][:, hs]
                y_diag = _bdot(CBs[i] * decay, xh * s['dt'][:, h:h + 1])
                h_prev = hn_ref[n_prev, i, hs, :]
                y_off = _bdot_nt(Cgs[i], h_prev) * jnp.exp(col)
                e_end = s['a_end'][:, h:h + 1]
                xw_t = s['xs_t'][hs, :] * (s['dt_t'][h:h + 1, :] * jnp.exp(e_end - row))
                hn_ref[n_prev, i, hs, :] = jnp.exp(e_end) * h_prev + _bdot(xw_t, Bgs[i])
                ysc[i, :, hs] = y_diag + y_off + dsk_ref[:, h:h + 1] * xh
    for i in range(nb):
        y = ysc[i] * jax.nn.silu(z_ref[i])
        y_ref[i] = y * lax.rsqrt(jnp.mean(y * y, -1, keepdims=True) + LN_EPS) * nw_ref[...]


def _ssd(xbc3, z3, sm3, conv0, conv_base, h0, h_base, cw, cb, dtb, alog, dsk, nw, prev, nb, L):
    B, T, _ = xbc3.shape
    K1 = SHORT_CONV - 1
    n_prev = 0 if prev is None else prev.shape[0]
    tokspec = lambda a: pl.BlockSpec((nb, L, a.shape[2]), lambda b, t: (b, t, 0))
    full = lambda a: pl.BlockSpec(a.shape, lambda b, t: (0,) * a.ndim)
    in_specs = [tokspec(xbc3), tokspec(z3), tokspec(sm3),
                pl.BlockSpec((nb, K1, SSD_CONV_DIM), lambda b, t: (conv_base // nb + b, 0, 0)),
                pl.BlockSpec((nb, SSD_INNER, SSD_STATE), lambda b, t: (h_base // nb + b, 0, 0)),
                full(cw), full(cb), full(dtb), full(alog), full(dsk), full(nw)]
    args = [xbc3, z3, sm3, conv0, h0, cw, cb, dtb, alog, dsk, nw]
    if n_prev:
        in_specs.append(pl.BlockSpec((n_prev, nb, SSD_INNER, SSD_STATE), lambda b, t: (0, b, 0, 0)))
        args.append(prev)
    return pl.pallas_call(
        functools.partial(_ssd_body, nb=nb, L=L, n_prev=n_prev),
        grid=(B // nb, T // L),
        in_specs=in_specs,
        out_specs=[tokspec(z3), pl.BlockSpec((n_prev + 1, nb, SSD_INNER, SSD_STATE), lambda b, t: (0, b, 0, 0)),
                   pl.BlockSpec((nb, K1, SSD_CONV_DIM), lambda b, t: (b, 0, 0))],
        out_shape=[jax.ShapeDtypeStruct(z3.shape, F32),
                   jax.ShapeDtypeStruct((n_prev + 1, B, SSD_INNER, SSD_STATE), F32),
                   jax.ShapeDtypeStruct((B, K1, SSD_CONV_DIM), F32)],
        scratch_shapes=[pltpu.VMEM((nb, HALO + L, SSD_CONV_DIM), F32), pltpu.VMEM((nb, L, SSD_INNER), F32)],
        compiler_params=_cparams("parallel", "arbitrary"),
        name="ssd",
    )(*args)


def _gla_body(*refs, nb, tc, l, n_prev):
    q_ref, k_ref, v_ref, gg_ref, sm_ref, s0_ref, wa_ref, ba_ref, nw_ref = refs[:9]
    prev_ref = refs[9] if n_prev else None
    o_ref, sn_ref, st = refs[9 + bool(n_prev):]
    K, V = GLA_DK, GLA_DV

    @pl.when(pl.program_id(1) == 0)
    def _():
        for i in range(nb):
            for h in range(GLA_HEADS):
                st[i, h * V:(h + 1) * V, :] = s0_ref[i, h * K:(h + 1) * K, :].T
        if n_prev:
            sn_ref[0:n_prev] = prev_ref[...]

    ri, ci = _iota((tc, tc), 0), _iota((tc, tc), 1)
    shift = int(np.log2(l))
    lower = (ri >= ci) & (jnp.right_shift(ri, shift) == jnp.right_shift(ci, shift))
    seqs = []
    for i in range(nb):
        log_alpha = jax.nn.log_sigmoid(_bdot(sm_ref[i], wa_ref[...]) + ba_ref[...]) / GLA_TAU
        bc = jnp.dot(lower.astype(F32), log_alpha, precision=HIGHEST, preferred_element_type=F32)
        k = k_ref[i]
        seqs.append((bc, k, v_ref[i], q_ref[i] * GLA_DK ** -0.5 * jnp.exp(bc), k * jnp.exp(-bc)))
    for h in range(GLA_HEADS):
        cs = slice(h * K, (h + 1) * K)
        for i, (bc, k, v, qe, ke) in enumerate(seqs):
            att = jnp.where(lower, _bdot_nt(qe[:, cs], ke[:, cs]), 0.0)
            o_h = _bdot(att, v[:, cs])
            subs = [slice(c * l, (c + 1) * l) for c in range(tc // l)]
            b_ends = [bc[rs.stop - 1:rs.stop, cs] for rs in subs]
            kvs = [_bdot_tn(v[rs, cs], k[rs, cs] * jnp.exp(b_end - bc[rs, cs])) for rs, b_end in zip(subs, b_ends)]
            states = [st[i, h * V:(h + 1) * V, :]]
            for b_end, kv in zip(b_ends, kvs):
                states.append(states[-1] * jnp.exp(b_end) + kv)
            inter = [_bdot_nt(qe[rs, cs], s_prev) for rs, s_prev in zip(subs, states)]
            s_t = states[-1]
            st[i, h * V:(h + 1) * V, :] = s_t
            sn_ref[n_prev, i, h * K:(h + 1) * K, :] = s_t.T
            o_h = o_h + (inter[0] if len(inter) == 1 else jnp.concatenate(inter, axis=0))
            o_h = o_h * lax.rsqrt(jnp.mean(o_h * o_h, -1, keepdims=True) + LN_EPS) * nw_ref[...]
            o_ref[i, :, cs] = o_h * jax.nn.silu(gg_ref[i, :, cs])


def _gla(gq3, gk3, gv3, gg3, sm3, s0, s_base, wa, ba, nw, prev, nb, tc):
    B, T, _ = gq3.shape
    l = min(GLA_CHUNK, T)
    n_prev = 0 if prev is None else prev.shape[0]
    tokspec = lambda a: pl.BlockSpec((nb, tc, a.shape[2]), lambda b, t: (b, t, 0))
    full = lambda a: pl.BlockSpec(a.shape, lambda b, t: (0,) * a.ndim)
    in_specs = [tokspec(gq3), tokspec(gk3), tokspec(gv3), tokspec(gg3), tokspec(sm3),
                pl.BlockSpec((nb, GLA_W, GLA_DV), lambda b, t: (s_base // nb + b, 0, 0)),
                full(wa), full(ba), full(nw)]
    args = [gq3, gk3, gv3, gg3, sm3, s0, wa, ba, nw]
    if n_prev:
        in_specs.append(pl.BlockSpec((n_prev, nb, GLA_W, GLA_DV), lambda b, t: (0, b, 0, 0)))
        args.append(prev)
    return pl.pallas_call(
        functools.partial(_gla_body, nb=nb, tc=tc, l=l, n_prev=n_prev),
        grid=(B // nb, T // tc),
        in_specs=in_specs,
        out_specs=[tokspec(gq3), pl.BlockSpec((n_prev + 1, nb, GLA_W, GLA_DV), lambda b, t: (0, b, 0, 0))],
        out_shape=[jax.ShapeDtypeStruct(gq3.shape, F32), jax.ShapeDtypeStruct((n_prev + 1, B, GLA_W, GLA_DV), F32)],
        scratch_shapes=[pltpu.VMEM((nb, GLA_HEADS * GLA_DV, GLA_DK), F32)],
        compiler_params=_cparams("parallel", "arbitrary"),
        name="gla",
    )(*args)


EVEN_SEGS = (
    (0, EV_Q, (True,) * 4),
    (EV_Q, 4 * LANES, (True, False, True, False)),
    (EV_Q + 4 * LANES, 2 * LANES, (True, False)),
    (EV_Q + EV_KV, LRU_WIDTH, None),
    (EV_Q + EV_KV + LRU_WIDTH, LRU_WIDTH, None),
    (EV_Q + EV_KV + 2 * LRU_WIDTH, LANES, None),
)
ODD_SEGS = (
    (0, SSD_INNER, None),
    (SSD_INNER, SSD_CONV_DIM, None),
    (SSD_INNER + SSD_CONV_DIM, GLA_W, None),
    (SSD_INNER + SSD_CONV_DIM + GLA_W, GLA_W, None),
    (SSD_INNER + SSD_CONV_DIM + 2 * GLA_W, GLA_W, None),
    (SSD_INNER + SSD_CONV_DIM + 3 * GLA_W, GLA_W, None),
    (SSD_INNER + SSD_CONV_DIM + 4 * GLA_W, LANES, None),
)


def _even_w_in(w):
    a = EV_Q + EV_KV
    pad = jnp.zeros((D_MODEL, LANES - EV_GATE), w.dtype)
    return jnp.concatenate([w[:, :a], w[:, a + EV_GATE:], w[:, a:a + EV_GATE], pad], axis=1).astype(BF16)


def _odd_w_in(w):
    a = SSD_INNER + SSD_CONV_DIM
    dt = w[:, a:a + SSD_HEADS]
    rest = w[:, a + SSD_HEADS:a + SSD_HEADS + 4 * GLA_W]
    ga = w[:, a + SSD_HEADS + 4 * GLA_W:]
    pad = jnp.zeros((D_MODEL, LANES - SSD_HEADS - GLA_RANK), w.dtype)
    return jnp.concatenate([w[:, :a], rest, dt, ga, pad], axis=1).astype(BF16)


def _lane_pad(v):
    return jnp.pad(v.astype(F32), (0, LANES - v.shape[0]))[None, :]


def _prep_layer(layer, P):
    j = layer // 2
    d = dict(
        ffn_wup=P['ffn_w_up_bf16'], ffn_cw=P['ffn_conv_w'][layer], ffn_cb=P['ffn_conv_b'][layer][None, :],
        ffn_wdn=P['ffn_w_down_bf16'],
        ln_mix_g=P['ln_mix_g'][layer][None, :], ln_mix_b=P['ln_mix_b'][layer][None, :],
        ln_ffn_g=P['ln_ffn_g'][layer][None, :], ln_ffn_b=P['ln_ffn_b'][layer][None, :])
    if layer % 2 == 0:
        w_out = P['w_out_even'][j].astype(BF16)
        wg = P['lru_w_gates'][j]
        eye = jnp.eye(LRU_BLOCKS, dtype=F32)
        wg = jnp.einsum('knde,nm->kndme', wg, eye).reshape(2, LRU_WIDTH, LRU_WIDTH)
        d.update(
            w_in=_even_w_in(P['w_in_even'][j]), w_out1=w_out[:EV_Q], w_out2=w_out[EV_Q:],
            cmp_tab=jnp.repeat(P['nsa_cmp_w'][j].reshape(CMP_BLOCK, 2 * NSA_KV_HEADS), HEAD_DIM, axis=1),
            cmp_page_tab=_page_compress_table(P['nsa_cmp_w'][j]),
            lru_cw=P['lru_conv_w'][j], lru_cb=P['lru_conv_b'][j][None, :],
            lru_wg=jnp.concatenate([wg[0], wg[1]], axis=1).astype(BF16),
            lru_bg=P['lru_b_gates'][j].reshape(1, 2 * LRU_WIDTH), lru_lam=P['lru_lambda'][j][None, :])
    else:
        w_out = P['w_out_odd'][j].astype(BF16)
        wa = jnp.zeros((LANES, GLA_W), F32).at[SSD_HEADS:SSD_HEADS + GLA_RANK].set(P['gla_w_alpha'][j])
        d.update(
            w_in=_odd_w_in(P['w_in_odd'][j]), w_out1=w_out[:SSD_INNER], w_out2=w_out[SSD_INNER:],
            ssd_cw=P['ssd_conv_w'][j], ssd_cb=P['ssd_conv_b'][j][None, :],
            ssd_dtb=_lane_pad(P['ssd_dt_bias'][j]), ssd_alog=_lane_pad(P['ssd_a_log'][j]),
            ssd_d=_lane_pad(P['ssd_d'][j]), ssd_nw=P['ssd_norm_w'][j][None, :],
            gla_wa=wa.astype(BF16), gla_ba=P['gla_b_alpha'][j][None, :], gla_nw=P['gla_norm_w'][j][None, :])
    return d


def _trunk(x3, prm, st, cfg):
    B, T, _ = x3.shape
    M = B * T
    out = dict(kv=[], win=[], lh=[], lc=[], sc=[], fc=[])
    ssd_states = gla_states = None
    for layer in range(DEPTH):
        p = prm[layer]
        j = layer // 2
        x2 = x3.reshape(M, D_MODEL)
        if layer % 2 == 0:
            q, rows, win, hx, hgate, hg = _proj(x2, p['w_in'], EVEN_SEGS, cfg['tm'], cfg['rope'], cfg['rope_blocks'])
            to3 = lambda a: a.reshape(B, T, a.shape[1])
            rows3, win3 = to3(rows), to3(win)
            if cfg['sample']:
                o_nsa = _nsa_sample(to3(q), to3(hg), rows3, win3, st['cache_t'], st['page_table'],
                                    j * st['n_phys'], st['cwin_t'], j * B, p['cmp_page_tab'], cfg['nsa_nb'])
                win_keep = win3
            else:
                H3 = _compress_prompt(rows3, p['cmp_tab'])
                o_nsa = _nsa_prompt(to3(q), to3(hg), H3, rows3, win3, cfg['tq'])
                win_keep = win3[:, T - min(WINDOW, T):]
            y_lru, h_last, conv_n = _lru(to3(hx), to3(hgate), st['lru_conv'], j * B, st['lru_h'], j * B,
                                         p['lru_cw'], p['lru_cb'], p['lru_wg'], p['lru_bg'], p['lru_lam'],
                                         cfg['lru_nb'], cfg['lru_tc'])
            a1, a2 = o_nsa, y_lru
            out['kv'].append(rows3.reshape(B, T, 4, NSA_KV_HEADS, HEAD_DIM))
            out['win'].append(win_keep.reshape(B, -1, 2, NSA_KV_HEADS, HEAD_DIM))
            out['lh'].append(h_last.reshape(B, LRU_WIDTH))
            out['lc'].append(conv_n)
        else:
            z, xbc, gq, gk, gv, gg, sm = _proj(x2, p['w_in'], ODD_SEGS, cfg['tm'])
            to3 = lambda a: a.reshape(B, T, a.shape[1])
            sm3 = to3(sm)
            y_ssd, ssd_states, conv_n = _ssd(
                to3(xbc), to3(z), sm3, st['ssd_conv'], j * B, st['ssd_h'], j * B, p['ssd_cw'], p['ssd_cb'],
                p['ssd_dtb'], p['ssd_alog'], p['ssd_d'], p['ssd_nw'], ssd_states, cfg['rec_nb'], cfg['ssd_L'])
            o_gla, gla_states = _gla(to3(gq), to3(gk), to3(gv), to3(gg), sm3, st['gla_s'], j * B,
                                     p['gla_wa'], p['gla_ba'], p['gla_nw'], gla_states, cfg['rec_nb'], cfg['gla_tc'])
            a1, a2 = y_ssd, o_gla
            out['sc'].append(conv_n)
        x3, fbuf = _mix_ffn(x3, a1, a2, p['w_out1'], p['w_out2'], p['ln_mix_g'], p['ln_mix_b'], layer,
                            p['ffn_wup'], p['ffn_cw'], p['ffn_cb'], p['ffn_wdn'], st['ffn_conv'], layer * B,
                            p['ln_ffn_g'], p['ln_ffn_b'], cfg['ffn_nb'], cfg['ffn_tc'], cfg['ffn_cw'])
        out['fc'].append(fbuf)
    out = {k: jnp.stack(v) for k, v in out.items()}
    out['sh'] = ssd_states.reshape(-1, B, SSD_HEADS, SSD_HEAD_DIM, SSD_STATE)
    out['gs'] = gla_states.reshape(-1, B, GLA_HEADS, GLA_DK, GLA_DV)
    return x3, out


def _largest_tile(n, cap):
    t = min(n, cap)
    while n % t:
        t //= 2
    return t


def kernel(x_prompt, x_sample, cache_nsa_kv, cache_nsa_win, state_lru_h, state_lru_conv, state_ssd, state_ssd_conv, state_gla, state_ffn_conv, page_table, w_in_even, w_out_even, nsa_cmp_w, lru_conv_w, lru_conv_b, lru_w_gates, lru_b_gates, lru_lambda, w_in_odd, w_out_odd, ssd_conv_w, ssd_conv_b, ssd_dt_bias, ssd_a_log, ssd_d, ssd_norm_w, gla_w_alpha, gla_b_alpha, gla_norm_w, ffn_w_up, ffn_conv_w, ffn_conv_b, ffn_w_down, ln_mix_g, ln_mix_b, ln_ffn_g, ln_ffn_b):
    P = dict(w_in_even=w_in_even, w_out_even=w_out_even, nsa_cmp_w=nsa_cmp_w, lru_conv_w=lru_conv_w,
             lru_conv_b=lru_conv_b, lru_w_gates=lru_w_gates, lru_b_gates=lru_b_gates, lru_lambda=lru_lambda,
             w_in_odd=w_in_odd, w_out_odd=w_out_odd, ssd_conv_w=ssd_conv_w, ssd_conv_b=ssd_conv_b,
             ssd_dt_bias=ssd_dt_bias, ssd_a_log=ssd_a_log, ssd_d=ssd_d, ssd_norm_w=ssd_norm_w,
             gla_w_alpha=gla_w_alpha, gla_b_alpha=gla_b_alpha, gla_norm_w=gla_norm_w, ffn_w_up=ffn_w_up,
             ffn_conv_w=ffn_conv_w, ffn_conv_b=ffn_conv_b, ffn_w_down=ffn_w_down, ln_mix_g=ln_mix_g,
             ln_mix_b=ln_mix_b, ln_ffn_g=ln_ffn_g, ln_ffn_b=ln_ffn_b)
    P['ffn_w_up_bf16'] = ffn_w_up.astype(BF16)
    P['ffn_w_down_bf16'] = ffn_w_down.astype(BF16)
    prm = [_prep_layer(layer, P) for layer in range(DEPTH)]
    n_even, n_odd = w_in_even.shape[0], w_in_odd.shape[0]
    ffn_buf = (FFN_CONV - 1, 2 * D_FF)

    Bp, Tp, _ = x_prompt.shape
    tm_p = _largest_tile(Tp, 512)
    st_p = dict(
        lru_conv=jnp.zeros((n_even * Bp, SHORT_CONV - 1, LRU_WIDTH), F32), lru_h=jnp.zeros((n_even * Bp, 1, LRU_WIDTH), F32),
        ssd_conv=jnp.zeros((n_odd * Bp, SHORT_CONV - 1, SSD_CONV_DIM), F32),
        ssd_h=jnp.zeros((n_odd * Bp, SSD_INNER, SSD_STATE), F32), gla_s=jnp.zeros((n_odd * Bp, GLA_W, GLA_DV), F32),
        ffn_conv=jnp.zeros((DEPTH * Bp,) + ffn_buf, F32))
    cfg_p = dict(sample=False, tm=tm_p, rope=_rope_tables(jnp.arange(Tp)), rope_blocks=Tp // tm_p,
                 tq=_largest_tile(Tp, 256), lru_nb=1, lru_tc=_largest_tile(Tp, 512),
                 rec_nb=1, ssd_L=_largest_tile(Tp, 512), gla_tc=_largest_tile(Tp, 256),
                 ffn_nb=1, ffn_tc=_largest_tile(Tp, 512), ffn_cw=D_FF // 2)
    y_p, o_p = _trunk(x_prompt, prm, st_p, cfg_p)

    Bs, Ts, _ = x_sample.shape
    n_phys = cache_nsa_kv.shape[1]
    past_len = page_table.shape[1] * PAGE_SIZE
    tm_s = _largest_tile(Bs * Ts, 256)
    pos_s = past_len + jnp.arange(tm_s) % Ts
    nb_s = _largest_tile(Bs, 32)
    st_s = dict(
        cache_t=cache_nsa_kv.transpose(0, 1, 3, 4, 5, 2).reshape(n_even * n_phys, 4 * LANES, PAGE_SIZE),
        n_phys=n_phys, page_table=page_table,
        cwin_t=cache_nsa_win.transpose(0, 1, 3, 4, 5, 2).reshape(n_even * Bs, 2 * LANES, cache_nsa_win.shape[2]),
        lru_conv=state_lru_conv.reshape(n_even * Bs, SHORT_CONV - 1, LRU_WIDTH),
        lru_h=state_lru_h.reshape(n_even * Bs, 1, LRU_WIDTH),
        ssd_conv=state_ssd_conv.reshape(n_odd * Bs, SHORT_CONV - 1, SSD_CONV_DIM),
        ssd_h=state_ssd.reshape(n_odd * Bs, SSD_INNER, SSD_STATE),
        gla_s=state_gla.reshape(n_odd * Bs, GLA_W, GLA_DV),
        ffn_conv=state_ffn_conv.reshape((DEPTH * Bs,) + ffn_buf))
    cfg_s = dict(sample=True, tm=tm_s, rope=_rope_tables(pos_s), rope_blocks=1, nsa_nb=_largest_tile(Bs, 4),
                 rec_nb=_largest_tile(Bs, 4),
                 lru_nb=nb_s, lru_tc=Ts, ssd_L=Ts, gla_tc=Ts, ffn_nb=nb_s, ffn_tc=Ts, ffn_cw=D_FF // 2)
    y_s, o_s = _trunk(x_sample, prm, st_s, cfg_s)
    win_t = _window_append(st_s['cwin_t'], o_s['win'].reshape(n_even * Bs, Ts, 2 * LANES),
                           _largest_tile(n_even * Bs, SUBLANES))
    win_s = win_t.reshape(n_even, Bs, 2, NSA_KV_HEADS, HEAD_DIM, -1).transpose(0, 1, 5, 2, 3, 4)

    return (y_p, y_s, o_p['kv'], o_s['kv'], o_p['win'], win_s, o_p['lh'], o_s['lh'], o_p['lc'], o_s['lc'],
            o_p['sh'], o_s['sh'], o_p['sc'], o_s['sc'], o_p['gs'], o_s['gs'], o_p['fc'], o_s['fc'])
```

```python
import functools

import jax
import jax.numpy as jnp
import numpy as np
from jax import lax
from jax.experimental import pallas as pl
from jax.experimental.pallas import tpu as pltpu

F32 = jnp.float32
BF16 = jnp.bfloat16
HIGHEST = lax.Precision.HIGHEST

D_MODEL = 1024
DEPTH = 4
PAGE_SIZE = 128
HEAD_DIM = 64
ROPE_DIM = HEAD_DIM // 4
ROPE_THETA = 500000.0
NSA_HEADS = 8
NSA_KV_HEADS = 2
NSA_GROUP = NSA_HEADS // NSA_KV_HEADS
CMP_BLOCK = 32
CMP_STRIDE = 16
SEL_BLOCK = 64
N_SEL = 8
WINDOW = 512
LRU_WIDTH = D_MODEL // 2
LRU_BLOCKS = 8
LRU_BLOCK_DIM = LRU_WIDTH // LRU_BLOCKS
LRU_C = 8.0
SHORT_CONV = 4
SSD_HEADS = 16
SSD_HEAD_DIM = 64
SSD_INNER = SSD_HEADS * SSD_HEAD_DIM
SSD_GROUPS = 2
SSD_STATE = 128
SSD_CONV_DIM = SSD_INNER + 2 * SSD_GROUPS * SSD_STATE
GLA_HEADS = 4
GLA_DK = 128
GLA_DV = 128
GLA_RANK = 16
GLA_TAU = 16.0
GLA_CHUNK = 32
D_FF = 2816
FFN_CONV = 3
ALPHA = (2.0 * DEPTH) ** 0.25
LN_EPS = 1e-5
NEG = -1e30
EV_Q = NSA_HEADS * HEAD_DIM
EV_KV = 6 * NSA_KV_HEADS * HEAD_DIM
EV_GATE = 3 * NSA_HEADS
GLA_W = GLA_HEADS * GLA_DK

LANES = 128
SUBLANES = 8
V7X_VMEM_BYTES = 64 * 1024 * 1024
VMEM_LIMIT = V7X_VMEM_BYTES - 8 * 1024 * 1024
HALO = SUBLANES


def _cparams(*sem):
    return pltpu.CompilerParams(dimension_semantics=sem, vmem_limit_bytes=VMEM_LIMIT)


def _resident(a):
    return pl.BlockSpec(a.shape, lambda *_: (0,) * a.ndim, pipeline_mode=pl.Buffered(1))


def _bdot(a, b):
    return jnp.dot(a.astype(BF16), b.astype(BF16), preferred_element_type=F32)


def _bdot_nt(a, b):
    return lax.dot_general(a.astype(BF16), b.astype(BF16), (((1,), (1,)), ((), ())), preferred_element_type=F32)


def _bdot_tn(a, b):
    return lax.dot_general(a.astype(BF16), b.astype(BF16), (((0,), (0,)), ((), ())), preferred_element_type=F32)


def _iota(shape, axis):
    return lax.broadcasted_iota(jnp.int32, shape, axis)


def _layer_norm(y, g, b):
    mu = jnp.mean(y, -1, keepdims=True)
    d = y - mu
    var = jnp.mean(d * d, -1, keepdims=True)
    return d * lax.rsqrt(var + LN_EPS) * g + b


def _masked_softmax(s, mask):
    s = jnp.where(mask, s, NEG)
    m = jnp.max(s, -1, keepdims=True)
    e = jnp.exp(s - m)
    p = e / jnp.sum(e, -1, keepdims=True)
    return jnp.where(mask, p, 0.0)


def _proj_body(*refs, segs, has_rope):
    if has_rope:
        x_ref, w_ref, c_ref, s1_ref, s2_ref = refs[:5]
        out_refs = refs[5:]
    else:
        x_ref, w_ref = refs[:2]
        out_refs = refs[2:]
    xb = x_ref[...].astype(BF16)
    for o_ref, (start, width, rope) in zip(out_refs, segs):
        acc = jnp.dot(xb, w_ref[:, start:start + width], preferred_element_type=F32)
        if rope is None:
            o_ref[...] = acc
            continue
        for c, flag in enumerate(rope):
            chunk = acc[:, c * LANES:(c + 1) * LANES]
            if flag:
                chunk = (chunk * c_ref[...] + pltpu.roll(chunk, LANES - ROPE_DIM // 2, 1) * s2_ref[...]
                         + pltpu.roll(chunk, ROPE_DIM // 2, 1) * s1_ref[...])
            o_ref[:, c * LANES:(c + 1) * LANES] = chunk


def _proj(x2d, w, segs, tm, tabs=None, tab_blocks=1):
    M, K = x2d.shape
    N = w.shape[1]
    has_rope = tabs is not None
    in_specs = [pl.BlockSpec((tm, K), lambda i: (i, 0)), _resident(w)]
    args = [x2d, w]
    if has_rope:
        in_specs += [pl.BlockSpec((tm, LANES), lambda i: (i % tab_blocks, 0))] * 3
        args += list(tabs)
    return pl.pallas_call(
        functools.partial(_proj_body, segs=segs, has_rope=has_rope),
        grid=(M // tm,),
        in_specs=in_specs,
        out_specs=[pl.BlockSpec((tm, wd), lambda i: (i, 0)) for _, wd, _ in segs],
        out_shape=[jax.ShapeDtypeStruct((M, wd), F32) for _, wd, _ in segs],
        compiler_params=_cparams("parallel"),
        name="proj",
    )(*args)


def _rope_tables(pos):
    half = ROPE_DIM // 2
    inv = jnp.power(ROPE_THETA, -2.0 * jnp.arange(half, dtype=F32) / ROPE_DIM)
    ang = pos.astype(F32)[:, None] * inv[None, :]
    cos, sin = jnp.cos(ang), jnp.sin(ang)
    R = pos.shape[0]
    rest = HEAD_DIM - ROPE_DIM
    c64 = jnp.concatenate([cos, cos, jnp.ones((R, rest), F32)], 1)
    s1 = jnp.concatenate([jnp.zeros((R, half), F32), sin, jnp.zeros((R, rest), F32)], 1)
    s2 = jnp.concatenate([-sin, jnp.zeros((R, half + rest), F32)], 1)
    return tuple(jnp.concatenate([t, t], 1) for t in (c64, s1, s2))


def _compress_rows(x, w_ref):
    x3 = x.reshape(x.shape[0] // CMP_STRIDE, CMP_STRIDE, 2 * LANES)
    return (jnp.sum(x3 * w_ref[0:CMP_STRIDE, :][None], axis=1),
            jnp.sum(x3 * w_ref[CMP_STRIDE:CMP_BLOCK, :][None], axis=1))


def _compress_body(kv_ref, w_ref, o_ref):
    h0, h1 = _compress_rows(kv_ref[0], w_ref)
    o_ref[0, :, 0:2 * LANES] = h0
    o_ref[0, :, 2 * LANES:4 * LANES] = h1


def _compress_prompt(rows3, wtab):
    B, T, _ = rows3.shape
    return pl.pallas_call(
        _compress_body,
        grid=(B,),
        in_specs=[pl.BlockSpec((1, T, 2 * LANES), lambda b: (b, 0, 0)),
                  pl.BlockSpec((CMP_BLOCK, 2 * LANES), lambda b: (0, 0))],
        out_specs=pl.BlockSpec((1, T // CMP_STRIDE, 4 * LANES), lambda b: (b, 0, 0)),
        out_shape=jax.ShapeDtypeStruct((B, T // CMP_STRIDE, 4 * LANES), F32),
        compiler_params=_cparams("parallel"),
        name="compress_prompt",
    )(rows3, wtab)


def _overlap_matrix(n_rows, n_cmp, n_sel):
    s1 = np.arange(n_rows)[:, None] * CMP_STRIDE
    s2 = np.arange(LANES)[None, :] * SEL_BLOCK
    ov = np.clip(np.minimum(s1 + CMP_BLOCK, s2 + SEL_BLOCK) - np.maximum(s1, s2), 0, None) / CMP_BLOCK
    ov = ov * (np.arange(n_rows)[:, None] < n_cmp) * (np.arange(LANES)[None, :] < n_sel)
    return jnp.asarray(ov, dtype=F32)


def _select_blocks(psum, ov, t_col, n_sel):
    imp = jnp.dot(psum, ov, precision=HIGHEST, preferred_element_type=F32)
    blk = _iota(imp.shape, 1)
    cur = jnp.right_shift(t_col, int(np.log2(SEL_BLOCK)))
    future = blk * SEL_BLOCK > t_col
    forced = (blk == 0) | (blk == cur) | (blk == cur - 1)
    w = jnp.where(future, -1.0, jnp.where(forced, 1e6, imp))
    w = jnp.where(blk < n_sel, w, -jnp.inf)
    n_pad = -(-n_sel // SUBLANES) * SUBLANES
    wt = w.T[0:n_pad, :]
    sub = _iota((n_pad, 1), 0)
    rank = jnp.zeros(wt.shape, F32)
    for i in range(n_sel):
        wi = wt[i:i + 1, :]
        rank = rank + jnp.where((wi > wt) | ((wi == wt) & (sub > i)), 1.0, 0.0)
    sel_t = jnp.where((rank < min(N_SEL, n_sel)) & (sub < n_sel), 1.0, 0.0)
    if n_pad < LANES:
        sel_t = jnp.concatenate([sel_t, jnp.zeros((LANES - n_pad, sel_t.shape[1]), F32)], axis=0)
    return sel_t.T


def _stack_heads(q, h):
    G = NSA_GROUP
    qh = jnp.concatenate([q[:, (h * G + g) * HEAD_DIM:(h * G + g + 1) * HEAD_DIM] for g in range(G)], axis=0)
    return (qh * HEAD_DIM ** -0.5).astype(BF16)


def _expand_matrix(K):
    blk = jnp.right_shift(_iota((LANES, K), 1), int(np.log2(SEL_BLOCK)))
    return (blk == _iota((LANES, K), 0)).astype(BF16)


def _attend(qb, chunks, mask, tq):
    G = NSA_GROUP
    dh = HEAD_DIM
    bias = jnp.where(mask, 0.0, NEG)
    ks = [k.astype(BF16) for _, k, _ in chunks]
    vs = [jnp.concatenate([v.astype(BF16), jnp.ones(v.shape, BF16)], axis=0 if tr else 1) for tr, _, v in chunks]
    reps = G if tq >= LANES else 1
    rows = G * tq // reps
    outs = []
    for r in range(reps):
        q_r = qb[r * rows:(r + 1) * rows]
        s = [jnp.dot(q_r, k, preferred_element_type=F32) if tr else
             lax.dot_general(q_r, k, (((1,), (1,)), ((), ())), preferred_element_type=F32)
             for (tr, _, _), k in zip(chunks, ks)]
        s = s[0] if len(s) == 1 else jnp.concatenate(s, axis=-1)
        K = s.shape[-1]
        s = (s.reshape(rows // tq, tq, K) + bias[None]).reshape(rows, K)
        e = jnp.exp(s - jnp.max(s, -1, keepdims=True)).astype(BF16)
        ox, off = None, 0
        for (tr, _, _), v in zip(chunks, vs):
            kc = v.shape[1] if tr else v.shape[0]
            part = (lax.dot_general(e[:, off:off + kc], v, (((1,), (1,)), ((), ())), preferred_element_type=F32)
                    if tr else jnp.dot(e[:, off:off + kc], v, preferred_element_type=F32))
            ox = part if ox is None else ox + part
            off += kc
        outs.append(ox[:, :dh] / ox[:, dh:dh + 1])
    return outs[0] if reps == 1 else jnp.concatenate(outs, axis=0)


def _gate_store(o_ref, sg, h, tq, o_c, o_s, o_w):
    for g in range(NSA_GROUP):
        hh = h * NSA_GROUP + g
        r = slice(g * tq, (g + 1) * tq)
        o_ref[:, hh * HEAD_DIM:(hh + 1) * HEAD_DIM] = (
            sg[:, 3 * hh:3 * hh + 1] * o_c[r] + sg[:, 3 * hh + 1:3 * hh + 2] * o_s[r]
            + sg[:, 3 * hh + 2:3 * hh + 3] * o_w[r])


def _nsa_core(items, ov, t_col, n_sel):
    tq = t_col.shape[0]
    G = NSA_GROUP
    o_cs, psums = [], []
    for it in items:
        n_rows = it['ckk'].shape[0]
        p_c = _masked_softmax(_bdot_nt(it['qb'], it['ckk']).reshape(G, tq, n_rows), it['mask_c'][None])
        o_cs.append(_bdot(p_c.reshape(G * tq, n_rows), it['ckv']))
        psum = p_c[0]
        for g in range(1, G):
            psum = psum + p_c[g]
        psums.append(psum)
    o_ws = [it['win'](it['qb']) for it in items]
    n = len(items)
    sel = _select_blocks(jnp.concatenate(psums, axis=0), ov, jnp.concatenate([t_col] * n, axis=0), n_sel)
    for i, it in enumerate(items):
        o_s = it['slc'](it['qb'], sel[i * tq:(i + 1) * tq])
        it['finish'](o_cs[i], o_s, o_ws[i])


def _combine_compressed(h0, h1):
    return h0 + pltpu.roll(h1, h1.shape[0] - 1, 0)


def _nsa_prompt_body(q_ref, hg_ref, H_ref, rows_ref, win_ref, ov_ref, o_ref, *, T, tq):
    qi = pl.program_id(1)
    n_rows = T // CMP_STRIDE
    n_cmp = n_rows - CMP_BLOCK // CMP_STRIDE + 1
    n_sel = T // SEL_BLOCK
    t_col = qi * tq + _iota((tq, 1), 0)
    ck = _combine_compressed(H_ref[0, :, 0:2 * LANES], H_ref[0, :, 2 * LANES:4 * LANES])
    ncol = _iota((1, n_rows), 1)
    mask_c = (ncol * CMP_STRIDE + CMP_BLOCK - 1 <= t_col) & (ncol < n_cmp)
    band = WINDOW + tq
    wstart = pl.multiple_of(jnp.clip(qi * tq - WINDOW, 0, T - band), tq)
    wpos = wstart + _iota((1, band), 1)
    mask_w = (wpos <= t_col) & (wpos > t_col - WINDOW)
    q = q_ref[0]
    sg = jax.nn.sigmoid(hg_ref[0])
    kstep = min(T, 4 * LANES)

    def make_item(h):
        def slc_span(K):
            def run(qb, sel):
                allowed = jnp.dot(sel.astype(BF16), _expand_matrix(K), preferred_element_type=F32) > 0.5
                mask = allowed & (_iota((1, K), 1) <= t_col)
                k = rows_ref[0, 0:K, 2 * LANES + h * HEAD_DIM:2 * LANES + (h + 1) * HEAD_DIM]
                v = rows_ref[0, 0:K, 3 * LANES + h * HEAD_DIM:3 * LANES + (h + 1) * HEAD_DIM]
                return _attend(qb, [(False, k, v)], mask, tq)
            return run

        def slc(qb, sel):
            spans = [slc_span((i + 1) * kstep) for i in range(T // kstep)]
            if len(spans) == 1:
                return spans[0](qb, sel)
            return lax.switch(lax.div(qi * tq + (tq - 1), kstep), spans, qb, sel)

        def win(qb):
            k = win_ref[0, pl.ds(wstart, band), h * HEAD_DIM:(h + 1) * HEAD_DIM]
            v = win_ref[0, pl.ds(wstart, band), LANES + h * HEAD_DIM:LANES + (h + 1) * HEAD_DIM]
            return _attend(qb, [(False, k, v)], mask_w, tq)

        return dict(qb=_stack_heads(q, h), ckk=ck[:, h * HEAD_DIM:(h + 1) * HEAD_DIM],
                    ckv=ck[:, LANES + h * HEAD_DIM:LANES + (h + 1) * HEAD_DIM], mask_c=mask_c,
                    slc=slc, win=win, finish=functools.partial(_gate_store, o_ref.at[0], sg, h, tq))

    _nsa_core([make_item(h) for h in range(NSA_KV_HEADS)], ov_ref[...], t_col, n_sel)


def _nsa_prompt(q3, hg3, H3, rows3, win3, tq):
    B, T, _ = q3.shape
    n_rows = T // CMP_STRIDE
    ov = _overlap_matrix(n_rows, n_rows - 1, T // SEL_BLOCK)
    return pl.pallas_call(
        functools.partial(_nsa_prompt_body, T=T, tq=tq),
        grid=(B, T // tq),
        in_specs=[pl.BlockSpec((1, tq, EV_Q), lambda b, i: (b, i, 0)),
                  pl.BlockSpec((1, tq, LANES), lambda b, i: (b, i, 0)),
                  pl.BlockSpec((1, n_rows, 4 * LANES), lambda b, i: (b, 0, 0)),
                  pl.BlockSpec((1, T, 4 * LANES), lambda b, i: (b, 0, 0)),
                  pl.BlockSpec((1, T, 2 * LANES), lambda b, i: (b, 0, 0)),
                  pl.BlockSpec((n_rows, LANES), lambda b, i: (0, 0))],
        out_specs=pl.BlockSpec((1, tq, EV_Q), lambda b, i: (b, i, 0)),
        out_shape=jax.ShapeDtypeStruct((B, T, EV_Q), F32),
        compiler_params=_cparams("parallel", "arbitrary"),
        name="nsa_prompt",
    )(q3, hg3, H3, rows3, win3, ov)


def _nsa_sample_body(*refs, nb, n_pages, Tq, Wb):
    q_ref, hg_ref, rnew_ref, wnew_ref, ctab_ref, ov_ref = refs[1:7]
    page_refs = refs[7:7 + nb * n_pages]
    cwin_ref, o_ref, knew, wnew = refs[7 + nb * n_pages:]
    P = n_pages * PAGE_SIZE
    Ks = P + PAGE_SIZE
    Kw = Wb + PAGE_SIZE
    n_rows = P // CMP_STRIDE
    n_cmp = n_rows - CMP_BLOCK // CMP_STRIDE + 1
    n_sel = -(-(P + Tq) // SEL_BLOCK)
    tt = _iota((Tq, 1), 0)
    t_col = P + tt
    mask_c = (_iota((1, n_rows), 1) < n_cmp) & (tt >= 0)
    causal = _iota((1, Ks), 1) <= t_col
    wpos = _iota((1, Kw), 1)
    mask_w = (wpos <= Wb + tt) & (wpos > Wb + tt - WINDOW)
    expand = _expand_matrix(Ks)
    pad_rows = jnp.zeros((PAGE_SIZE - Tq, 2 * LANES), F32)

    n_grp = 2 * NSA_KV_HEADS
    per_page = PAGE_SIZE // CMP_STRIDE

    def make_item(i, h, cks, sg, pages):
        ksl = slice(h * HEAD_DIM, (h + 1) * HEAD_DIM)
        vsl = slice(LANES + h * HEAD_DIM, LANES + (h + 1) * HEAD_DIM)

        def slc(qb, sel):
            mask = (jnp.dot(sel.astype(BF16), expand, preferred_element_type=F32) > 0.5) & causal
            chunks = [(True, pr[0, 2 * LANES + ksl.start:2 * LANES + ksl.stop, :],
                       pr[0, 2 * LANES + vsl.start:2 * LANES + vsl.stop, :]) for pr in pages]
            chunks.append((False, knew[i, :, ksl], knew[i, :, vsl]))
            return _attend(qb, chunks, mask, Tq)

        def win(qb):
            chunks = [(True, cwin_ref[i, ksl, :], cwin_ref[i, vsl, :]), (False, wnew[i, :, ksl], wnew[i, :, vsl])]
            return _attend(qb, chunks, mask_w, Tq)

        return dict(qb=_stack_heads(q_ref[i], h), ckk=cks[h], ckv=cks[NSA_KV_HEADS + h], mask_c=mask_c,
                    slc=slc, win=win, finish=functools.partial(_gate_store, o_ref.at[i], sg, h, Tq))

    items = []
    for i in range(nb):
        knew[i, 0:Tq, :] = rnew_ref[i, :, 2 * LANES:4 * LANES]
        knew[i, Tq:PAGE_SIZE, :] = pad_rows
        wnew[i, 0:Tq, :] = wnew_ref[i]
        wnew[i, Tq:PAGE_SIZE, :] = pad_rows
        pages = page_refs[i * n_pages:(i + 1) * n_pages]
        parts = [_bdot_nt(ctab_ref[...], pr[0, 0:2 * LANES, :]) for pr in pages]
        cks = []
        for g in range(n_grp):
            r0, cs = g * 2 * per_page, slice(g * HEAD_DIM, (g + 1) * HEAD_DIM)
            cks.append(_combine_compressed(
                jnp.concatenate([pp[r0:r0 + per_page, cs] for pp in parts], axis=0),
                jnp.concatenate([pp[r0 + per_page:r0 + 2 * per_page, cs] for pp in parts], axis=0)))
        sg = jax.nn.sigmoid(hg_ref[i])
        items += [make_item(i, h, cks, sg, pages) for h in range(NSA_KV_HEADS)]
    _nsa_core(items, ov_ref[...], t_col, n_sel)


def _page_compress_table(cmp_w):
    per_page = PAGE_SIZE // CMP_STRIDE
    w = cmp_w.reshape(CMP_BLOCK, 2 * NSA_KV_HEADS).T
    s = jnp.arange(PAGE_SIZE)[None, :] - CMP_STRIDE * jnp.arange(per_page)[:, None]
    inside = (s >= 0) & (s < CMP_STRIDE)
    sc = jnp.clip(s, 0, CMP_STRIDE - 1)
    first = jnp.where(inside[None], w[:, sc], 0.0)
    second = jnp.where(inside[None], w[:, CMP_STRIDE + sc], 0.0)
    return jnp.concatenate([first, second], axis=1).reshape(-1, PAGE_SIZE).astype(BF16)


def _nsa_sample(q3, hg3, rows3, win3, cache_t, page_table, page_base, cwin_t, cwin_base, ctab, nb):
    B, Tq, _ = q3.shape
    n_pages = page_table.shape[1]
    P = n_pages * PAGE_SIZE
    Wb = cwin_t.shape[2]
    assert Wb % LANES == 0 and Wb > LANES and Tq <= SUBLANES and B % nb == 0
    n_rows = P // CMP_STRIDE
    n_sel = -(-(P + Tq) // SEL_BLOCK)
    ov = _overlap_matrix(n_rows, n_rows - 1, n_sel)
    cb0 = cwin_base // nb

    def page_spec(i, p):
        return pl.BlockSpec((1, 4 * LANES, PAGE_SIZE), lambda b, pt: (page_base + pt[b * nb + i, p], 0, 0))

    tok = lambda a: pl.BlockSpec((nb, Tq, a.shape[2]), lambda b, pt: (b, 0, 0))
    grid_spec = pltpu.PrefetchScalarGridSpec(
        num_scalar_prefetch=1,
        grid=(B // nb,),
        in_specs=[tok(q3), tok(hg3), tok(rows3), tok(win3),
                  pl.BlockSpec(ctab.shape, lambda b, pt: (0, 0)), pl.BlockSpec(ov.shape, lambda b, pt: (0, 0))]
        + [page_spec(i, p) for i in range(nb) for p in range(n_pages)]
        + [pl.BlockSpec((nb, 2 * LANES, Wb), lambda b, pt: (cb0 + b, 0, 0))],
        out_specs=tok(q3),
        scratch_shapes=[pltpu.VMEM((nb, PAGE_SIZE, 2 * LANES), F32), pltpu.VMEM((nb, PAGE_SIZE, 2 * LANES), F32)],
    )
    return pl.pallas_call(
        functools.partial(_nsa_sample_body, nb=nb, n_pages=n_pages, Tq=Tq, Wb=Wb),
        grid_spec=grid_spec,
        out_shape=jax.ShapeDtypeStruct((B, Tq, EV_Q), F32),
        compiler_params=_cparams("arbitrary"),
        name="nsa_sample",
    )(page_table, q3, hg3, rows3, win3, ctab, ov, *([cache_t] * (nb * n_pages)), cwin_t)


def _window_append_body(cwin_ref, new_ref, o_ref, pad, *, nb, Tq, Wb):
    new_lanes = _iota((1, LANES), 1) >= LANES - Tq
    pad[Tq:LANES, :] = jnp.zeros((LANES - Tq, pad.shape[1]), F32)
    for i in range(nb):
        pad[0:Tq, :] = new_ref[i]
        new_t = pltpu.roll(pad[...].T, LANES - Tq, 1)
        shifted = pltpu.roll(cwin_ref[i], Wb - Tq, 1)
        o_ref[i, :, 0:Wb - LANES] = shifted[:, 0:Wb - LANES]
        o_ref[i, :, Wb - LANES:Wb] = jnp.where(new_lanes, new_t, shifted[:, Wb - LANES:Wb])


def _window_append(cwin_t, new_rows, nb):
    S, C, Wb = cwin_t.shape
    Tq = new_rows.shape[1]
    assert Wb % LANES == 0 and Wb > LANES and Tq <= LANES and S % nb == 0
    return pl.pallas_call(
        functools.partial(_window_append_body, nb=nb, Tq=Tq, Wb=Wb),
        grid=(S // nb,),
        in_specs=[pl.BlockSpec((nb, C, Wb), lambda s: (s, 0, 0)), pl.BlockSpec((nb, Tq, C), lambda s: (s, 0, 0))],
        out_specs=pl.BlockSpec((nb, C, Wb), lambda s: (s, 0, 0)),
        out_shape=jax.ShapeDtypeStruct(cwin_t.shape, F32),
        scratch_shapes=[pltpu.VMEM((LANES, C), F32)],
        compiler_params=_cparams("parallel"),
        name="window_append",
    )(cwin_t, new_rows)


def _causal_conv(xp, w_ref, b_ref, tc, cols=None):
    K = w_ref.shape[0]
    cs = slice(None) if cols is None else cols
    acc = None
    for j in range(K):
        term = w_ref[j:j + 1, cs][None] * xp[:, HALO - (K - 1) + j:HALO - (K - 1) + j + tc, :]
        acc = term if acc is None else acc + term
    return b_ref[:, cs][None] + acc


def _lru_body(hx_ref, hgate_ref, conv0_ref, h0_ref, cw_ref, cb_ref, wg_ref, bg_ref, lam_ref,
              y_ref, hlast_ref, convn_ref, xp, *, nb, tc):
    K1 = SHORT_CONV - 1

    @pl.when(pl.program_id(1) == 0)
    def _():
        xp[:, HALO - K1:HALO, :] = conv0_ref[...]
        hlast_ref[...] = h0_ref[...]

    xp[:, HALO:HALO + tc, :] = hx_ref[...]
    xc = _causal_conv(xp, cw_ref, cb_ref, tc)
    tail = xp[:, HALO + tc - K1:HALO + tc, :]
    convn_ref[...] = tail
    xp[:, HALO - K1:HALO, :] = tail
    R = nb * tc
    xc2 = xc.reshape(R, LRU_WIDTH)
    gt = _bdot(xc2, wg_ref[...]) + bg_ref[...]
    r_gate = jax.nn.sigmoid(gt[:, :LRU_WIDTH])
    i_gate = jax.nn.sigmoid(gt[:, LRU_WIDTH:])
    log_a = -LRU_C * r_gate * jax.nn.softplus(-lam_ref[...])
    a = jnp.exp(log_a)
    th = jnp.tanh(log_a)
    u = jnp.sqrt(-2.0 * th / (1.0 - th)) * i_gate * xc2
    tpos = lax.rem(_iota((R, 1), 0), tc)
    d = 1
    while d < tc:
        valid = tpos >= d
        u = jnp.where(valid, a * pltpu.roll(u, d, 0) + u, u)
        a = jnp.where(valid, a * pltpu.roll(a, d, 0), a)
        d *= 2
    hprev = jnp.broadcast_to(hlast_ref[...], (nb, tc, LRU_WIDTH)).reshape(R, LRU_WIDTH)
    h = a * hprev + u
    y_ref[...] = (h * jax.nn.gelu(hgate_ref[...].reshape(R, LRU_WIDTH))).reshape(nb, tc, LRU_WIDTH)
    last = _iota((1, tc, 1), 1) == tc - 1
    hlast_ref[...] = jnp.sum(jnp.where(last, h.reshape(nb, tc, LRU_WIDTH), 0.0), axis=1, keepdims=True)


def _lru(hx3, hgate3, conv0, conv_base, h0, h_base, cw, cb, wg, bg, lam, nb, tc):
    B, T, C = hx3.shape
    K1 = SHORT_CONV - 1
    cb0 = conv_base // nb
    hb0 = h_base // nb
    tok = pl.BlockSpec((nb, tc, C), lambda b, t: (b, t, 0))
    full = lambda a: pl.BlockSpec(a.shape, lambda b, t: (0,) * a.ndim)
    return pl.pallas_call(
        functools.partial(_lru_body, nb=nb, tc=tc),
        grid=(B // nb, T // tc),
        in_specs=[tok, tok,
                  pl.BlockSpec((nb, K1, C), lambda b, t: (cb0 + b, 0, 0)),
                  pl.BlockSpec((nb, 1, C), lambda b, t: (hb0 + b, 0, 0)),
                  full(cw), full(cb), full(wg), full(bg), full(lam)],
        out_specs=[tok, pl.BlockSpec((nb, 1, C), lambda b, t: (b, 0, 0)),
                   pl.BlockSpec((nb, K1, C), lambda b, t: (b, 0, 0))],
        out_shape=[jax.ShapeDtypeStruct((B, T, C), F32), jax.ShapeDtypeStruct((B, 1, C), F32),
                   jax.ShapeDtypeStruct((B, K1, C), F32)],
        scratch_shapes=[pltpu.VMEM((nb, HALO + tc, C), F32)],
        compiler_params=_cparams("parallel", "arbitrary"),
        name="lru",
    )(hx3, hgate3, conv0, h0, cw, cb, wg, bg, lam)


def _ffn_body(x_ref, a1_ref, a2_ref, w1_ref, w2_ref, gm_ref, bm_ref, wup_ref, cw_ref, cb_ref, wdn_ref, buf0_ref,
              g_ref, b_ref, y_ref, bufn_ref, sg, sv, *, nb, tc, cw):
    K1 = FFN_CONV - 1

    @pl.when(pl.program_id(1) == 0)
    def _():
        bufn_ref[...] = buf0_ref[...]

    R = nb * tc
    mix = (_bdot(a1_ref[...].reshape(R, a1_ref.shape[2]), w1_ref[...])
           + _bdot(a2_ref[...].reshape(R, a2_ref.shape[2]), w2_ref[...]))
    x = _layer_norm(ALPHA * x_ref[...].reshape(R, D_MODEL) + mix, gm_ref[...], bm_ref[...])
    xb = x.astype(BF16)
    acc = jnp.zeros((R, D_MODEL), F32)
    for c in range(D_FF // cw):
        conv = []
        for half, scr in ((0, sg), (1, sv)):
            cols = slice(half * D_FF + c * cw, half * D_FF + (c + 1) * cw)
            u = jnp.dot(xb, wup_ref[:, cols], preferred_element_type=F32)
            scr[:, HALO:HALO + tc, :] = u.reshape(nb, tc, cw)
            scr[:, HALO - K1:HALO, :] = bufn_ref[:, :, cols]
            conv.append(_causal_conv(scr, cw_ref, cb_ref, tc, cols).reshape(R, cw))
            bufn_ref[:, :, cols] = scr[:, HALO + tc - K1:HALO + tc, :]
        act = jax.nn.gelu(conv[0]) * conv[1]
        acc = acc + jnp.dot(act.astype(BF16), wdn_ref[c * cw:(c + 1) * cw, :], preferred_element_type=F32)
    y = _layer_norm(ALPHA * x + acc, g_ref[...], b_ref[...])
    y_ref[...] = y.reshape(nb, tc, D_MODEL)


def _mix_ffn(x3, a1, a2, w1, w2, gm, bm, layer, wup, cw, cb, wdn, buf0, buf_base, g, b, nb, tc, cwid):
    B, T, _ = x3.shape
    U, K1 = 2 * D_FF, FFN_CONV - 1
    bb0 = buf_base // nb
    tokspec = lambda a: pl.BlockSpec((nb, tc, a.shape[2]), lambda bi, t: (bi, t, 0))
    tok = tokspec(x3)
    full = _resident
    of_layer = lambda a: pl.BlockSpec((None,) + a.shape[1:], lambda *_: (layer, 0, 0), pipeline_mode=pl.Buffered(1))
    return pl.pallas_call(
        functools.partial(_ffn_body, nb=nb, tc=tc, cw=cwid),
        grid=(B // nb, T // tc),
        in_specs=[tok, tokspec(a1), tokspec(a2), full(w1), full(w2), full(gm), full(bm),
                  of_layer(wup), full(cw), full(cb), of_layer(wdn),
                  pl.BlockSpec((nb, K1, U), lambda bi, t: (bb0 + bi, 0, 0)), full(g), full(b)],
        out_specs=[tok, pl.BlockSpec((nb, K1, U), lambda bi, t: (bi, 0, 0))],
        out_shape=[jax.ShapeDtypeStruct(x3.shape, F32), jax.ShapeDtypeStruct((B, K1, U), F32)],
        scratch_shapes=[pltpu.VMEM((nb, HALO + tc, cwid), F32), pltpu.VMEM((nb, HALO + tc, cwid), F32)],
        compiler_params=_cparams("parallel", "arbitrary"),
        name="mix_ffn",
    )(x3, a1, a2, w1, w2, gm, bm, wup, cw, cb, wdn, buf0, g, b)


def _ssd_body(*refs, nb, L, n_prev):
    xbc_ref, z_ref, sm_ref, conv0_ref, h0_ref, cw_ref, cb_ref, dtb_ref, alog_ref, dsk_ref, nw_ref = refs[:11]
    prev_ref = refs[11] if n_prev else None
    y_ref, hn_ref, convn_ref, xp, ysc = refs[11 + bool(n_prev):]
    K1 = SHORT_CONV - 1
    P, N = SSD_HEAD_DIM, SSD_STATE

    @pl.when(pl.program_id(1) == 0)
    def _():
        xp[:, HALO - K1:HALO, :] = conv0_ref[...]
        hn_ref[n_prev] = h0_ref[...]
        if n_prev:
            hn_ref[0:n_prev] = prev_ref[...]

    xp[:, HALO:HALO + L, :] = xbc_ref[...]
    xc = _causal_conv(xp, cw_ref, cb_ref, L)
    tail = xp[:, HALO + L - K1:HALO + L, :]
    convn_ref[...] = tail
    xp[:, HALO - K1:HALO, :] = tail
    head_lane = _iota((1, LANES), 1) < SSD_HEADS
    lower = _iota((L, L), 0) >= _iota((L, L), 1)
    J = SSD_HEADS // SSD_GROUPS
    seqs = []
    for i in range(nb):
        xa = jax.nn.silu(xc[i])
        xs = xa[:, :SSD_INNER]
        dt = jnp.where(head_lane, jax.nn.softplus(sm_ref[i] + dtb_ref[...]), 0.0)
        la = dt * -jnp.exp(alog_ref[...])
        acum = jnp.dot(lower.astype(F32), la, precision=HIGHEST, preferred_element_type=F32)
        seqs.append(dict(xs=xs, bm=xa[:, SSD_INNER:SSD_INNER + SSD_GROUPS * N], cm=xa[:, SSD_INNER + SSD_GROUPS * N:],
                         dt=dt, acum=acum, acum_t=acum.T, dt_t=dt.T, xs_t=xs.T, a_end=acum[L - 1:L, :]))
    for g in range(SSD_GROUPS):
        Bgs = [s['bm'][:, g * N:(g + 1) * N].astype(BF16) for s in seqs]
        Cgs = [s['cm'][:, g * N:(g + 1) * N].astype(BF16) for s in seqs]
        CBs = [_bdot_nt(Cg, Bg) for Cg, Bg in zip(Cgs, Bgs)]
        for j in range(J):
            h = g * J + j
            hs = slice(h * P, (h + 1) * P)
            for i, s in enumerate(seqs):
                col = s['acum'][:, h:h + 1]
                row = s['acum_t'][h:h + 1, :]
                decay = jnp.exp(jnp.where(lower, col - row, -jnp.inf))
                xh = s['xs'][:, hs]
                y_diag = _bdot(CBs[i] * decay, xh * s['dt'][:, h:h + 1])
                h_prev = hn_ref[n_prev, i, hs, :]
                y_off = _bdot_nt(Cgs[i], h_prev) * jnp.exp(col)
                e_end = s['a_end'][:, h:h + 1]
                xw_t = s['xs_t'][hs, :] * (s['dt_t'][h:h + 1, :] * jnp.exp(e_end - row))
                hn_ref[n_prev, i, hs, :] = jnp.exp(e_end) * h_prev + _bdot(xw_t, Bgs[i])
                ysc[i, :, hs] = y_diag + y_off + dsk_ref[:, h:h + 1] * xh
    for i in range(nb):
        y = ysc[i] * jax.nn.silu(z_ref[i])
        y_ref[i] = y * lax.rsqrt(jnp.mean(y * y, -1, keepdims=True) + LN_EPS) * nw_ref[...]


def _ssd(xbc3, z3, sm3, conv0, conv_base, h0, h_base, cw, cb, dtb, alog, dsk, nw, prev, nb, L):
    B, T, _ = xbc3.shape
    K1 = SHORT_CONV - 1
    n_prev = 0 if prev is None else prev.shape[0]
    tokspec = lambda a: pl.BlockSpec((nb, L, a.shape[2]), lambda b, t: (b, t, 0))
    full = lambda a: pl.BlockSpec(a.shape, lambda b, t: (0,) * a.ndim)
    in_specs = [tokspec(xbc3), tokspec(z3), tokspec(sm3),
                pl.BlockSpec((nb, K1, SSD_CONV_DIM), lambda b, t: (conv_base // nb + b, 0, 0)),
                pl.BlockSpec((nb, SSD_INNER, SSD_STATE), lambda b, t: (h_base // nb + b, 0, 0)),
                full(cw), full(cb), full(dtb), full(alog), full(dsk), full(nw)]
    args = [xbc3, z3, sm3, conv0, h0, cw, cb, dtb, alog, dsk, nw]
    if n_prev:
        in_specs.append(pl.BlockSpec((n_prev, nb, SSD_INNER, SSD_STATE), lambda b, t: (0, b, 0, 0)))
        args.append(prev)
    return pl.pallas_call(
        functools.partial(_ssd_body, nb=nb, L=L, n_prev=n_prev),
        grid=(B // nb, T // L),
        in_specs=in_specs,
        out_specs=[tokspec(z3), pl.BlockSpec((n_prev + 1, nb, SSD_INNER, SSD_STATE), lambda b, t: (0, b, 0, 0)),
                   pl.BlockSpec((nb, K1, SSD_CONV_DIM), lambda b, t: (b, 0, 0))],
        out_shape=[jax.ShapeDtypeStruct(z3.shape, F32),
                   jax.ShapeDtypeStruct((n_prev + 1, B, SSD_INNER, SSD_STATE), F32),
                   jax.ShapeDtypeStruct((B, K1, SSD_CONV_DIM), F32)],
        scratch_shapes=[pltpu.VMEM((nb, HALO + L, SSD_CONV_DIM), F32), pltpu.VMEM((nb, L, SSD_INNER), F32)],
        compiler_params=_cparams("parallel", "arbitrary"),
        name="ssd",
    )(*args)


def _gla_body(*refs, nb, tc, l, n_prev):
    q_ref, k_ref, v_ref, gg_ref, sm_ref, s0_ref, wa_ref, ba_ref, nw_ref = refs[:9]
    prev_ref = refs[9] if n_prev else None
    o_ref, sn_ref, st = refs[9 + bool(n_prev):]
    K, V = GLA_DK, GLA_DV

    @pl.when(pl.program_id(1) == 0)
    def _():
        for i in range(nb):
            for h in range(GLA_HEADS):
                st[i, h * V:(h + 1) * V, :] = s0_ref[i, h * K:(h + 1) * K, :].T
        if n_prev:
            sn_ref[0:n_prev] = prev_ref[...]

    ri, ci = _iota((tc, tc), 0), _iota((tc, tc), 1)
    shift = int(np.log2(l))
    lower = (ri >= ci) & (jnp.right_shift(ri, shift) == jnp.right_shift(ci, shift))
    seqs = []
    for i in range(nb):
        log_alpha = jax.nn.log_sigmoid(_bdot(sm_ref[i], wa_ref[...]) + ba_ref[...]) / GLA_TAU
        bc = jnp.dot(lower.astype(F32), log_alpha, precision=HIGHEST, preferred_element_type=F32)
        k = k_ref[i]
        seqs.append((bc, k, v_ref[i], q_ref[i] * GLA_DK ** -0.5 * jnp.exp(bc), k * jnp.exp(-bc)))
    for h in range(GLA_HEADS):
        cs = slice(h * K, (h + 1) * K)
        for i, (bc, k, v, qe, ke) in enumerate(seqs):
            att = jnp.where(lower, _bdot_nt(qe[:, cs], ke[:, cs]), 0.0)
            o_h = _bdot(att, v[:, cs])
            subs = [slice(c * l, (c + 1) * l) for c in range(tc // l)]
            b_ends = [bc[rs.stop - 1:rs.stop, cs] for rs in subs]
            kvs = [_bdot_tn(v[rs, cs], k[rs, cs] * jnp.exp(b_end - bc[rs, cs])) for rs, b_end in zip(subs, b_ends)]
            states = [st[i, h * V:(h + 1) * V, :]]
            for b_end, kv in zip(b_ends, kvs):
                states.append(states[-1] * jnp.exp(b_end) + kv)
            inter = [_bdot_nt(qe[rs, cs], s_prev) for rs, s_prev in zip(subs, states)]
            s_t = states[-1]
            st[i, h * V:(h + 1) * V, :] = s_t
            sn_ref[n_prev, i, h * K:(h + 1) * K, :] = s_t.T
            o_h = o_h + (inter[0] if len(inter) == 1 else jnp.concatenate(inter, axis=0))
            o_h = o_h * lax.rsqrt(jnp.mean(o_h * o_h, -1, keepdims=True) + LN_EPS) * nw_ref[...]
            o_ref[i, :, cs] = o_h * jax.nn.silu(gg_ref[i, :, cs])


def _gla(gq3, gk3, gv3, gg3, sm3, s0, s_base, wa, ba, nw, prev, nb, tc):
    B, T, _ = gq3.shape
    l = min(GLA_CHUNK, T)
    n_prev = 0 if prev is None else prev.shape[0]
    tokspec = lambda a: pl.BlockSpec((nb, tc, a.shape[2]), lambda b, t: (b, t, 0))
    full = lambda a: pl.BlockSpec(a.shape, lambda b, t: (0,) * a.ndim)
    in_specs = [tokspec(gq3), tokspec(gk3), tokspec(gv3), tokspec(gg3), tokspec(sm3),
                pl.BlockSpec((nb, GLA_W, GLA_DV), lambda b, t: (s_base // nb + b, 0, 0)),
                full(wa), full(ba), full(nw)]
    args = [gq3, gk3, gv3, gg3, sm3, s0, wa, ba, nw]
    if n_prev:
        in_specs.append(pl.BlockSpec((n_prev, nb, GLA_W, GLA_DV), lambda b, t: (0, b, 0, 0)))
        args.append(prev)
    return pl.pallas_call(
        functools.partial(_gla_body, nb=nb, tc=tc, l=l, n_prev=n_prev),
        grid=(B // nb, T // tc),
        in_specs=in_specs,
        out_specs=[tokspec(gq3), pl.BlockSpec((n_prev + 1, nb, GLA_W, GLA_DV), lambda b, t: (0, b, 0, 0))],
        out_shape=[jax.ShapeDtypeStruct(gq3.shape, F32), jax.ShapeDtypeStruct((n_prev + 1, B, GLA_W, GLA_DV), F32)],
        scratch_shapes=[pltpu.VMEM((nb, GLA_HEADS * GLA_DV, GLA_DK), F32)],
        compiler_params=_cparams("parallel", "arbitrary"),
        name="gla",
    )(*args)


EVEN_SEGS = (
    (0, EV_Q, (True,) * 4),
    (EV_Q, 4 * LANES, (True, False, True, False)),
    (EV_Q + 4 * LANES, 2 * LANES, (True, False)),
    (EV_Q + EV_KV, LRU_WIDTH, None),
    (EV_Q + EV_KV + LRU_WIDTH, LRU_WIDTH, None),
    (EV_Q + EV_KV + 2 * LRU_WIDTH, LANES, None),
)
ODD_SEGS = (
    (0, SSD_INNER, None),
    (SSD_INNER, SSD_CONV_DIM, None),
    (SSD_INNER + SSD_CONV_DIM, GLA_W, None),
    (SSD_INNER + SSD_CONV_DIM + GLA_W, GLA_W, None),
    (SSD_INNER + SSD_CONV_DIM + 2 * GLA_W, GLA_W, None),
    (SSD_INNER + SSD_CONV_DIM + 3 * GLA_W, GLA_W, None),
    (SSD_INNER + SSD_CONV_DIM + 4 * GLA_W, LANES, None),
)


def _even_w_in(w):
    a = EV_Q + EV_KV
    pad = jnp.zeros((D_MODEL, LANES - EV_GATE), w.dtype)
    return jnp.concatenate([w[:, :a], w[:, a + EV_GATE:], w[:, a:a + EV_GATE], pad], axis=1).astype(BF16)


def _odd_w_in(w):
    a = SSD_INNER + SSD_CONV_DIM
    dt = w[:, a:a + SSD_HEADS]
    rest = w[:, a + SSD_HEADS:a + SSD_HEADS + 4 * GLA_W]
    ga = w[:, a + SSD_HEADS + 4 * GLA_W:]
    pad = jnp.zeros((D_MODEL, LANES - SSD_HEADS - GLA_RANK), w.dtype)
    return jnp.concatenate([w[:, :a], rest, dt, ga, pad], axis=1).astype(BF16)


def _lane_pad(v):
    return jnp.pad(v.astype(F32), (0, LANES - v.shape[0]))[None, :]


def _prep_layer(layer, P):
    j = layer // 2
    d = dict(
        ffn_wup=P['ffn_w_up_bf16'], ffn_cw=P['ffn_conv_w'][layer], ffn_cb=P['ffn_conv_b'][layer][None, :],
        ffn_wdn=P['ffn_w_down_bf16'],
        ln_mix_g=P['ln_mix_g'][layer][None, :], ln_mix_b=P['ln_mix_b'][layer][None, :],
        ln_ffn_g=P['ln_ffn_g'][layer][None, :], ln_ffn_b=P['ln_ffn_b'][layer][None, :])
    if layer % 2 == 0:
        w_out = P['w_out_even'][j].astype(BF16)
        wg = P['lru_w_gates'][j]
        eye = jnp.eye(LRU_BLOCKS, dtype=F32)
        wg = jnp.einsum('knde,nm->kndme', wg, eye).reshape(2, LRU_WIDTH, LRU_WIDTH)
        d.update(
            w_in=_even_w_in(P['w_in_even'][j]), w_out1=w_out[:EV_Q], w_out2=w_out[EV_Q:],
            cmp_tab=jnp.repeat(P['nsa_cmp_w'][j].reshape(CMP_BLOCK, 2 * NSA_KV_HEADS), HEAD_DIM, axis=1),
            cmp_page_tab=_page_compress_table(P['nsa_cmp_w'][j]),
            lru_cw=P['lru_conv_w'][j], lru_cb=P['lru_conv_b'][j][None, :],
            lru_wg=jnp.concatenate([wg[0], wg[1]], axis=1).astype(BF16),
            lru_bg=P['lru_b_gates'][j].reshape(1, 2 * LRU_WIDTH), lru_lam=P['lru_lambda'][j][None, :])
    else:
        w_out = P['w_out_odd'][j].astype(BF16)
        wa = jnp.zeros((LANES, GLA_W), F32).at[SSD_HEADS:SSD_HEADS + GLA_RANK].set(P['gla_w_alpha'][j])
        d.update(
            w_in=_odd_w_in(P['w_in_odd'][j]), w_out1=w_out[:SSD_INNER], w_out2=w_out[SSD_INNER:],
            ssd_cw=P['ssd_conv_w'][j], ssd_cb=P['ssd_conv_b'][j][None, :],
            ssd_dtb=_lane_pad(P['ssd_dt_bias'][j]), ssd_alog=_lane_pad(P['ssd_a_log'][j]),
            ssd_d=_lane_pad(P['ssd_d'][j]), ssd_nw=P['ssd_norm_w'][j][None, :],
            gla_wa=wa.astype(BF16), gla_ba=P['gla_b_alpha'][j][None, :], gla_nw=P['gla_norm_w'][j][None, :])
    return d


def _trunk(x3, prm, st, cfg):
    B, T, _ = x3.shape
    M = B * T
    out = dict(kv=[], win=[], lh=[], lc=[], sc=[], fc=[])
    ssd_states = gla_states = None
    for layer in range(DEPTH):
        p = prm[layer]
        j = layer // 2
        x2 = x3.reshape(M, D_MODEL)
        if layer % 2 == 0:
            q, rows, win, hx, hgate, hg = _proj(x2, p['w_in'], EVEN_SEGS, cfg['tm'], cfg['rope'], cfg['rope_blocks'])
            to3 = lambda a: a.reshape(B, T, a.shape[1])
            rows3, win3 = to3(rows), to3(win)
            if cfg['sample']:
                o_nsa = _nsa_sample(to3(q), to3(hg), rows3, win3, st['cache_t'], st['page_table'],
                                    j * st['n_phys'], st['cwin_t'], j * B, p['cmp_page_tab'], cfg['nsa_nb'])
                win_keep = win3
            else:
                H3 = _compress_prompt(rows3, p['cmp_tab'])
                o_nsa = _nsa_prompt(to3(q), to3(hg), H3, rows3, win3, cfg['tq'])
                win_keep = win3[:, T - min(WINDOW, T):]
            y_lru, h_last, conv_n = _lru(to3(hx), to3(hgate), st['lru_conv'], j * B, st['lru_h'], j * B,
                                         p['lru_cw'], p['lru_cb'], p['lru_wg'], p['lru_bg'], p['lru_lam'],
                                         cfg['lru_nb'], cfg['lru_tc'])
            a1, a2 = o_nsa, y_lru
            out['kv'].append(rows3.reshape(B, T, 4, NSA_KV_HEADS, HEAD_DIM))
            out['win'].append(win_keep.reshape(B, -1, 2, NSA_KV_HEADS, HEAD_DIM))
            out['lh'].append(h_last.reshape(B, LRU_WIDTH))
            out['lc'].append(conv_n)
        else:
            z, xbc, gq, gk, gv, gg, sm = _proj(x2, p['w_in'], ODD_SEGS, cfg['tm'])
            to3 = lambda a: a.reshape(B, T, a.shape[1])
            sm3 = to3(sm)
            y_ssd, ssd_states, conv_n = _ssd(
                to3(xbc), to3(z), sm3, st['ssd_conv'], j * B, st['ssd_h'], j * B, p['ssd_cw'], p['ssd_cb'],
                p['ssd_dtb'], p['ssd_alog'], p['ssd_d'], p['ssd_nw'], ssd_states, cfg['rec_nb'], cfg['ssd_L'])
            o_gla, gla_states = _gla(to3(gq), to3(gk), to3(gv), to3(gg), sm3, st['gla_s'], j * B,
                                     p['gla_wa'], p['gla_ba'], p['gla_nw'], gla_states, cfg['rec_nb'], cfg['gla_tc'])
            a1, a2 = y_ssd, o_gla
            out['sc'].append(conv_n)
        x3, fbuf = _mix_ffn(x3, a1, a2, p['w_out1'], p['w_out2'], p['ln_mix_g'], p['ln_mix_b'], layer,
                            p['ffn_wup'], p['ffn_cw'], p['ffn_cb'], p['ffn_wdn'], st['ffn_conv'], layer * B,
                            p['ln_ffn_g'], p['ln_ffn_b'], cfg['ffn_nb'], cfg['ffn_tc'], cfg['ffn_cw'])
        out['fc'].append(fbuf)
    out = {k: jnp.stack(v) for k, v in out.items()}
    out['sh'] = ssd_states.reshape(-1, B, SSD_HEADS, SSD_HEAD_DIM, SSD_STATE)
    out['gs'] = gla_states.reshape(-1, B, GLA_HEADS, GLA_DK, GLA_DV)
    return x3, out


def _largest_tile(n, cap):
    t = min(n, cap)
    while n % t:
        t //= 2
    return t


def kernel(x_prompt, x_sample, cache_nsa_kv, cache_nsa_win, state_lru_h, state_lru_conv, state_ssd, state_ssd_conv, state_gla, state_ffn_conv, page_table, w_in_even, w_out_even, nsa_cmp_w, lru_conv_w, lru_conv_b, lru_w_gates, lru_b_gates, lru_lambda, w_in_odd, w_out_odd, ssd_conv_w, ssd_conv_b, ssd_dt_bias, ssd_a_log, ssd_d, ssd_norm_w, gla_w_alpha, gla_b_alpha, gla_norm_w, ffn_w_up, ffn_conv_w, ffn_conv_b, ffn_w_down, ln_mix_g, ln_mix_b, ln_ffn_g, ln_ffn_b):
    P = dict(w_in_even=w_in_even, w_out_even=w_out_even, nsa_cmp_w=nsa_cmp_w, lru_conv_w=lru_conv_w,
             lru_conv_b=lru_conv_b, lru_w_gates=lru_w_gates, lru_b_gates=lru_b_gates, lru_lambda=lru_lambda,
             w_in_odd=w_in_odd, w_out_odd=w_out_odd, ssd_conv_w=ssd_conv_w, ssd_conv_b=ssd_conv_b,
             ssd_dt_bias=ssd_dt_bias, ssd_a_log=ssd_a_log, ssd_d=ssd_d, ssd_norm_w=ssd_norm_w,
             gla_w_alpha=gla_w_alpha, gla_b_alpha=gla_b_alpha, gla_norm_w=gla_norm_w, ffn_w_up=ffn_w_up,
             ffn_conv_w=ffn_conv_w, ffn_conv_b=ffn_conv_b, ffn_w_down=ffn_w_down, ln_mix_g=ln_mix_g,
             ln_mix_b=ln_mix_b, ln_ffn_g=ln_ffn_g, ln_ffn_b=ln_ffn_b)
    P['ffn_w_up_bf16'] = ffn_w_up.astype(BF16)
    P['ffn_w_down_bf16'] = ffn_w_down.astype(BF16)
    prm = [_prep_layer(layer, P) for layer in range(DEPTH)]
    n_even, n_odd = w_in_even.shape[0], w_in_odd.shape[0]
    ffn_buf = (FFN_CONV - 1, 2 * D_FF)

    Bp, Tp, _ = x_prompt.shape
    tm_p = _largest_tile(Tp, 512)
    st_p = dict(
        lru_conv=jnp.zeros((n_even * Bp, SHORT_CONV - 1, LRU_WIDTH), F32), lru_h=jnp.zeros((n_even * Bp, 1, LRU_WIDTH), F32),
        ssd_conv=jnp.zeros((n_odd * Bp, SHORT_CONV - 1, SSD_CONV_DIM), F32),
        ssd_h=jnp.zeros((n_odd * Bp, SSD_INNER, SSD_STATE), F32), gla_s=jnp.zeros((n_odd * Bp, GLA_W, GLA_DV), F32),
        ffn_conv=jnp.zeros((DEPTH * Bp,) + ffn_buf, F32))
    cfg_p = dict(sample=False, tm=tm_p, rope=_rope_tables(jnp.arange(Tp)), rope_blocks=Tp // tm_p,
                 tq=_largest_tile(Tp, 256), lru_nb=1, lru_tc=_largest_tile(Tp, 512),
                 rec_nb=1, ssd_L=_largest_tile(Tp, 512), gla_tc=_largest_tile(Tp, 256),
                 ffn_nb=1, ffn_tc=_largest_tile(Tp, 512), ffn_cw=D_FF // 2)
    y_p, o_p = _trunk(x_prompt, prm, st_p, cfg_p)

    Bs, Ts, _ = x_sample.shape
    n_phys = cache_nsa_kv.shape[1]
    past_len = page_table.shape[1] * PAGE_SIZE
    tm_s = _largest_tile(Bs * Ts, 256)
    pos_s = past_len + jnp.arange(tm_s) % Ts
    nb_s = _largest_tile(Bs, 32)
    st_s = dict(
        cache_t=cache_nsa_kv.transpose(0, 1, 3, 4, 5, 2).reshape(n_even * n_phys, 4 * LANES, PAGE_SIZE),
        n_phys=n_phys, page_table=page_table,
        cwin_t=cache_nsa_win.transpose(0, 1, 3, 4, 5, 2).reshape(n_even * Bs, 2 * LANES, cache_nsa_win.shape[2]),
        lru_conv=state_lru_conv.reshape(n_even * Bs, SHORT_CONV - 1, LRU_WIDTH),
        lru_h=state_lru_h.reshape(n_even * Bs, 1, LRU_WIDTH),
        ssd_conv=state_ssd_conv.reshape(n_odd * Bs, SHORT_CONV - 1, SSD_CONV_DIM),
        ssd_h=state_ssd.reshape(n_odd * Bs, SSD_INNER, SSD_STATE),
        gla_s=state_gla.reshape(n_odd * Bs, GLA_W, GLA_DV),
        ffn_conv=state_ffn_conv.reshape((DEPTH * Bs,) + ffn_buf))
    cfg_s = dict(sample=True, tm=tm_s, rope=_rope_tables(pos_s), rope_blocks=1, nsa_nb=_largest_tile(Bs, 4),
                 rec_nb=_largest_tile(Bs, 8),
                 lru_nb=nb_s, lru_tc=Ts, ssd_L=Ts, gla_tc=Ts, ffn_nb=nb_s, ffn_tc=Ts, ffn_cw=D_FF // 2)
    y_s, o_s = _trunk(x_sample, prm, st_s, cfg_s)
    win_t = _window_append(st_s['cwin_t'], o_s['win'].reshape(n_even * Bs, Ts, 2 * LANES),
                           _largest_tile(n_even * Bs, SUBLANES))
    win_s = win_t.reshape(n_even, Bs, 2, NSA_KV_HEADS, HEAD_DIM, -1).transpose(0, 1, 5, 2, 3, 4)

    return (y_p, y_s, o_p['kv'], o_s['kv'], o_p['win'], win_s, o_p['lh'], o_s['lh'], o_p['lc'], o_s['lc'],
            o_p['sh'], o_s['sh'], o_p['sc'], o_s['sc'], o_p['gs'], o_s['gs'], o_p['fc'], o_s['fc'])
```
